```python
import jax, jax.numpy as jnp
from jax import lax
import numpy as np

D_MODEL = 1024
BATCH = 16
SEQ = 2048
DEPTH = 4

CTX_LEN = 256
GRID_W = 64

RWKV_HEAD_DIM = 64
RWKV_DIM = D_MODEL
RWKV_HEADS = RWKV_DIM // RWKV_HEAD_DIM
DECAY_LORA = 64
ICLR_LORA = 64
GATE_LORA = 128
CONV_DIM = D_MODEL // 2
CONV_WIDTH = 31
SGU_DIM = D_MODEL // 2
SGU_GROUPS = 8
CHUNK = 128
D_FF = ((8 * D_MODEL // 3 + 255) // 256) * 256
N_BRANCH = 3
NORM_EPS = 1e-6
LN_EPS = 1e-5
GN_EPS = 64e-5

OFF_R = 0
OFF_K = OFF_R + RWKV_DIM
OFF_V = OFF_K + RWKV_DIM
OFF_WF = OFF_V + RWKV_DIM
OFF_WB = OFF_WF + DECAY_LORA
OFF_AF = OFF_WB + DECAY_LORA
OFF_AB = OFF_AF + ICLR_LORA
OFF_G = OFF_AB + ICLR_LORA
RWKV_COLS = OFF_G + GATE_LORA
OFF_CONV = RWKV_COLS
OFF_SGU = OFF_CONV + 2 * CONV_DIM
OFF_GATE = OFF_SGU + 2 * SGU_DIM
P_IN = OFF_GATE + N_BRANCH * D_MODEL

kernel_name = "hybrid_rwkv7_conformer_sgu_prefix_dit"


def _standardize(x, eps):
    xf = x.astype(jnp.float32)
    xc = xf - jnp.mean(xf, -1, keepdims=True)
    return xc * lax.rsqrt(jnp.mean(xc * xc, -1, keepdims=True) + eps)


def rms_norm(x, g):
    xf = x.astype(jnp.float32)
    y = xf * lax.rsqrt(jnp.mean(xf * xf, -1, keepdims=True) + NORM_EPS)
    return (y * g.astype(jnp.float32)).astype(x.dtype)


def layer_norm(x, g, b):
    return (_standardize(x, LN_EPS) * g + b).astype(x.dtype)


def modulate(h, shift, scale):
    return h * (1 + scale) + shift


def short_conv(z, w):
    zp = jnp.pad(z, ((0, 0), (1, 1), (0, 0)))
    return zp[:, :-2] * w[0] + zp[:, 1:-1] * w[1] + zp[:, 2:] * w[2]


def to_heads(t):
    B, T = t.shape[:2]
    return t.astype(jnp.float32).reshape(B, T, RWKV_HEADS, RWKV_HEAD_DIM)


def rwkv_inputs(z, w0, w_up, a0, a_up, k_k, k_a):
    k = z[..., OFF_K:OFF_V]
    v = to_heads(z[..., OFF_V:OFF_WF])
    kk = to_heads(k * k_k)
    kk = kk * lax.rsqrt(jnp.maximum(jnp.sum(kk * kk, -1, keepdims=True), 1e-24))
    kf = k.astype(jnp.float32)
    dirs = []
    for d, (ow, oa) in enumerate(((OFF_WF, OFF_AF), (OFF_WB, OFF_AB))):
        lw = (w0[d] + jnp.tanh(z[..., ow:ow + DECAY_LORA]) @ w_up[d]).astype(jnp.float32)
        decay = jnp.exp(-jnp.exp(-jax.nn.softplus(-lw) - 0.5))
        a = jax.nn.sigmoid((a0[d] + z[..., oa:oa + ICLR_LORA] @ a_up[d]).astype(jnp.float32))
        k_d = kf * (1.0 + (a - 1.0) * k_a.astype(jnp.float32))
        dirs.append((to_heads(decay), to_heads(k_d), kk * to_heads(a)))
    return to_heads(k), v, kk, dirs


def wkv_scan(S0, decay, k, v, kk, kka, r, reverse):
    want_y = r is not None

    def step(S, inp):
        w_t, k_t, v_t, kk_t, kka_t = inp[:5]
        sa = jnp.einsum('bhvk,bhk->bhv', S, kk_t)
        S = (S * w_t[:, :, None, :] - sa[..., None] * kka_t[:, :, None, :]
             + v_t[..., None] * k_t[:, :, None, :])
        y = jnp.einsum('bhvk,bhk->bhv', S, inp[5]) if want_y else None
        return S, y

    seq = (decay, k, v, kk, kka) + ((r,) if want_y else ())
    xs = tuple(jnp.swapaxes(t, 0, 1) for t in seq)
    S, ys = lax.scan(step, S0, xs, reverse=reverse)
    return S, (jnp.swapaxes(ys, 0, 1) if want_y else None)


def rwkv_output(y, r, k, v, gz, gn_g, gn_b, r_k, g_up, w_out, dtype):
    B, T = y.shape[:2]
    yn = _standardize(y, GN_EPS).reshape(B, T, RWKV_DIM) * gn_g + gn_b
    bonus = jnp.sum(r * k * r_k.reshape(RWKV_HEADS, RWKV_HEAD_DIM), -1, keepdims=True) * v
    g = jax.nn.sigmoid(gz) @ g_up
    return ((yn + bonus.reshape(B, T, RWKV_DIM)).astype(dtype) * g) @ w_out


def conformer_branch(z, dw, dw_b, ln_g, ln_b, w_out, grid, horizontal):
    u = z[..., :CONV_DIM] * jax.nn.sigmoid(z[..., CONV_DIM:])
    B, T, C = u.shape
    if grid:
        u4 = u.reshape(B, T // GRID_W, GRID_W, C)
        w = dw[None, :, None, :] if horizontal else dw[:, None, None, :]
    else:
        u4 = u.reshape(B, 1, T, C)
        w = dw[None, :, None, :]
    y = lax.conv_general_dilated(u4, w.astype(u4.dtype), (1, 1), 'SAME',
                                 dimension_numbers=('NHWC', 'HWIO', 'NHWC'),
                                 feature_group_count=C)
    y = layer_norm(y.reshape(B, T, C) + dw_b, ln_g, ln_b)
    return jax.nn.silu(y) @ w_out


def sgu_branch(z, ln_g, ln_b, w_s, b_s, w_out):
    z = jax.nn.gelu(z)
    u, v = z[..., :SGU_DIM], z[..., SGU_DIM:]
    v = layer_norm(v, ln_g, ln_b)
    B, T, _ = v.shape
    v = v.reshape(B, T // CHUNK, CHUNK, SGU_GROUPS, SGU_DIM // SGU_GROUPS)
    v = jnp.einsum('gpq,bnqgd->bnpgd', w_s, v) + b_s.T[:, :, None]
    return (u * v.reshape(B, T, SGU_DIM)) @ w_out


def merge(z, branches, w_o):
    B, T, _ = z.shape
    g = jax.nn.sigmoid(z[..., OFF_GATE:].reshape(B, T, N_BRANCH, D_MODEL))
    m = branches[0] * g[:, :, 0] + branches[1] * g[:, :, 1] + branches[2] * g[:, :, 2]
    return m @ w_o


def swiglu(h, w_in, w_out):
    gu = h @ w_in
    return (jax.nn.silu(gu[..., :D_FF]) * gu[..., D_FF:]) @ w_out


def setup_inputs(seed: int = 0) -> dict:
    key = jax.random.key(seed)
    k = jax.random.split(key, 36)
    L, D, RD = DEPTH, D_MODEL, RWKV_DIM

    def nrm(i, shape, scale):
        return jax.random.normal(k[i], shape, jnp.float32) * scale

    shift_base = jnp.array([0.25, 1.0, 0.25], jnp.float32)[None, :, None]
    return {
        'x': nrm(0, (BATCH, SEQ, D), 1.0),
        'c': nrm(1, (BATCH, D), 1.0),
        'ctx': nrm(2, (BATCH, CTX_LEN, D), 1.0),
        'c_ctx': nrm(3, (D,), 1.0),
        'w_mod': nrm(4, (L, D, 6 * D), 0.5 * D ** -0.5),
        'b_mod': nrm(5, (L, 6 * D), 0.01),
        'norm1_g': 1.0 + nrm(6, (L, D), 0.05),
        'norm2_g': 1.0 + nrm(7, (L, D), 0.05),
        'w_in': nrm(8, (L, D, P_IN), D ** -0.5),
        'rwkv_shift': shift_base + nrm(9, (L, 3, RWKV_COLS), 0.1),
        'rwkv_w0': jax.random.uniform(k[10], (L, 2, RD), jnp.float32, -4.0, 1.0),
        'rwkv_w_up': nrm(11, (L, 2, DECAY_LORA, RD), 0.5 * DECAY_LORA ** -0.5),
        'rwkv_a0': nrm(12, (L, 2, RD), 0.5),
        'rwkv_a_up': nrm(13, (L, 2, ICLR_LORA, RD), 0.5 * ICLR_LORA ** -0.5),
        'rwkv_g_up': nrm(14, (L, GATE_LORA, RD), GATE_LORA ** -0.5),
        'rwkv_k_k': 0.85 + nrm(15, (L, RD), 0.05),
        'rwkv_k_a': 1.0 + nrm(16, (L, RD), 0.05),
        'rwkv_r_k': nrm(17, (L, RD), 0.1),
        'rwkv_gn_g': 1.0 + nrm(18, (L, RD), 0.05),
        'rwkv_gn_b': nrm(19, (L, RD), 0.01),
        'rwkv_out': nrm(20, (L, RD, D), RD ** -0.5),
        'conv_dw': nrm(21, (L, CONV_WIDTH, CONV_DIM), CONV_WIDTH ** -0.5),
        'conv_dw_b': nrm(22, (L, CONV_DIM), 0.01),
        'conv_ln_g': 1.0 + nrm(23, (L, CONV_DIM), 0.05),
        'conv_ln_b': nrm(24, (L, CONV_DIM), 0.01),
        'conv_out': nrm(25, (L, CONV_DIM, D), CONV_DIM ** -0.5),
        'sgu_ln_g': 1.0 + nrm(26, (L, SGU_DIM), 0.05),
        'sgu_ln_b': nrm(27, (L, SGU_DIM), 0.01),
        'sgu_w': nrm(28, (L, SGU_GROUPS, CHUNK, CHUNK), 0.5 * CHUNK ** -0.5),
        'sgu_b': 1.0 + nrm(29, (L, SGU_GROUPS, CHUNK), 0.1),
        'sgu_out': nrm(30, (L, SGU_DIM, D), SGU_DIM ** -0.5),
        'w_merge': nrm(31, (L, D, D), D ** -0.5),
        'ffn_w_in': nrm(32, (L, D, 2 * D_FF), D ** -0.5),
        'ffn_w_out': nrm(33, (L, D_FF, D), D_FF ** -0.5),
        'final_norm_g': 1.0 + nrm(34, (D,), 0.05),
    }


def reference(x, c, ctx, c_ctx, w_mod, b_mod, norm1_g, norm2_g, w_in, rwkv_shift,
              rwkv_w0, rwkv_w_up, rwkv_a0, rwkv_a_up, rwkv_g_up, rwkv_k_k, rwkv_k_a,
              rwkv_r_k, rwkv_gn_g, rwkv_gn_b, rwkv_out, conv_dw, conv_dw_b, conv_ln_g,
              conv_ln_b, conv_out, sgu_ln_g, sgu_ln_b, sgu_w, sgu_b, sgu_out, w_merge,
              ffn_w_in, ffn_w_out, final_norm_g):
    s0 = jnp.zeros((x.shape[0], RWKV_HEADS, RWKV_HEAD_DIM, RWKV_HEAD_DIM), jnp.float32)
    sc_lat = jax.nn.silu(c)
    sc_ctx = jax.nn.silu(c_ctx)
    h_ctx = ctx
    for l in range(DEPTH):
        need_ctx = l < DEPTH - 1
        sh1, sc1, g1, sh2, sc2, g2 = jnp.split((sc_lat @ w_mod[l] + b_mod[l])[:, None, :], 6, axis=-1)
        n_c = 6 if need_ctx else 2
        mc = jnp.split(sc_ctx @ w_mod[l][:, :n_c * D_MODEL] + b_mod[l][:n_c * D_MODEL], n_c)

        hx = modulate(rms_norm(x, norm1_g[l]), sh1, sc1)
        hc = modulate(rms_norm(h_ctx, norm1_g[l]), mc[0], mc[1])
        zx = hx @ w_in[l]
        zc = hc @ (w_in[l] if need_ctx else w_in[l][:, :RWKV_COLS])

        zr_x = short_conv(zx[..., :RWKV_COLS], rwkv_shift[l])
        zr_c = short_conv(zc[..., :RWKV_COLS], rwkv_shift[l])
        lora = (rwkv_w0[l], rwkv_w_up[l], rwkv_a0[l], rwkv_a_up[l], rwkv_k_k[l], rwkv_k_a[l])
        k_x, v_x, kk_x, dirs_x = rwkv_inputs(zr_x, *lora)
        k_c, v_c, kk_c, dirs_c = rwkv_inputs(zr_c, *lora)
        r_x = to_heads(zr_x[..., OFF_R:OFF_K])
        r_c = to_heads(zr_c[..., OFF_R:OFF_K]) if need_ctx else None
        y_lat, y_ctx = [], []
        for d, rev in enumerate((False, True)):
            dec_c, kd_c, kka_c = dirs_c[d]
            s_ctx, yc = wkv_scan(s0, dec_c, kd_c, v_c, kk_c, kka_c, r_c, rev)
            dec_x, kd_x, kka_x = dirs_x[d]
            _, yx = wkv_scan(s_ctx, dec_x, kd_x, v_x, kk_x, kka_x, r_x, rev)
            y_lat.append(yx)
            y_ctx.append(yc)
        out_par = (rwkv_gn_g[l], rwkv_gn_b[l], rwkv_r_k[l], rwkv_g_up[l], rwkv_out[l])
        a_x = rwkv_output(y_lat[0] + y_lat[1], r_x, k_x, v_x, zr_x[..., OFF_G:RWKV_COLS],
                          *out_par, x.dtype)
        conv_par = (conv_dw[l], conv_dw_b[l], conv_ln_g[l], conv_ln_b[l], conv_out[l])
        b_x = conformer_branch(zx[..., OFF_CONV:OFF_SGU], *conv_par, True, l % 2 == 0)
        sgu_par = (sgu_ln_g[l], sgu_ln_b[l], sgu_w[l], sgu_b[l], sgu_out[l])
        c_x = sgu_branch(zx[..., OFF_SGU:OFF_GATE], *sgu_par)
        x_new = x + g1 * merge(zx, (a_x, b_x, c_x), w_merge[l])
        if need_ctx:
            a_c = rwkv_output(y_ctx[0] + y_ctx[1], r_c, k_c, v_c, zr_c[..., OFF_G:RWKV_COLS],
                              *out_par, h_ctx.dtype)
            b_c = conformer_branch(zc[..., OFF_CONV:OFF_SGU], *conv_par, False, True)
            c_c = sgu_branch(zc[..., OFF_SGU:OFF_GATE], *sgu_par)
            h_ctx = h_ctx + mc[2] * merge(zc, (a_c, b_c, c_c), w_merge[l])
        x = x_new

        x = x + g2 * swiglu(modulate(rms_norm(x, norm2_g[l]), sh2, sc2), ffn_w_in[l], ffn_w_out[l])
        if need_ctx:
            h_ctx = h_ctx + mc[5] * swiglu(modulate(rms_norm(h_ctx, norm2_g[l]), mc[3], mc[4]),
                                           ffn_w_in[l], ffn_w_out[l])
    return rms_norm(x, final_norm_g)
```

```python
import functools
import math

import jax
import jax.numpy as jnp
from jax import lax
from jax.experimental import pallas as pl
from jax.experimental.pallas import tpu as pltpu

D_MODEL = 1024
SEQ = 2048
CTX_LEN = 256
T_ALL = SEQ + CTX_LEN
GRID_W = 64
GRID_H = SEQ // GRID_W

HEAD_DIM = 64
HEADS = D_MODEL // HEAD_DIM
HEAD_PAIRS = HEADS // 2
DECAY_LORA = 64
ICLR_LORA = 64
GATE_LORA = 128
CONV_DIM = D_MODEL // 2
CONV_WIDTH = 31
CONV_HALF = CONV_WIDTH // 2
SGU_DIM = D_MODEL // 2
SGU_GROUPS = 8
SGU_GROUP_DIM = SGU_DIM // SGU_GROUPS
SGU_CHUNK = 128
D_FF = ((8 * D_MODEL // 3 + 255) // 256) * 256
NORM_EPS = 1e-6
LN_EPS = 1e-5
GN_EPS = 64e-5

RWKV_COLS = 3 * D_MODEL + 2 * DECAY_LORA + 2 * ICLR_LORA + GATE_LORA

LANES = 128
Z_GATE = 0
Z_RWKV = 3 * D_MODEL
Z_CONV = Z_RWKV + RWKV_COLS + 128
Z_SGU = Z_CONV + 2 * CONV_DIM
Z_COLS = Z_SGU + 2 * SGU_DIM
RWKV_TILES = RWKV_COLS // LANES
TILE_W = 3 * HEAD_PAIRS
TILE_A = TILE_W + 1
TILE_G = TILE_A + 1

ROW_TILE = 256
N_ROW_TILES = T_ALL // ROW_TILE
LAT_ROW_TILES = SEQ // ROW_TILE
SCAN_CHUNK = 64
N_CHUNKS = T_ALL // SCAN_CHUNK
LAT_CHUNKS = SEQ // SCAN_CHUNK
FF_CHUNK = 256

VMEM_LIMIT = 56 * 1024 * 1024

_BF = jnp.bfloat16
_F32 = jnp.float32


def _mm(a, b):
    return jnp.dot(a.astype(_BF), b.astype(_BF), preferred_element_type=_F32)


def _mm_f32(a, b):
    return jnp.dot(a, b, preferred_element_type=_F32, precision=lax.Precision.HIGHEST)


def _mm_nt(a, b):
    return lax.dot_general(a.astype(_BF), b.astype(_BF), (((1,), (1,)), ((), ())),
                           preferred_element_type=_F32)


def _mm_tn(a, b):
    return lax.dot_general(a.astype(_BF), b.astype(_BF), (((0,), (0,)), ((), ())),
                           preferred_element_type=_F32)


def _sigmoid(x):
    return jax.nn.sigmoid(x)


def _silu(x):
    return x * jax.nn.sigmoid(x)


def _gelu_tanh(x):
    return 0.5 * x * (1.0 + jnp.tanh(math.sqrt(2.0 / math.pi) * (x + 0.044715 * (x * x * x))))


def _standardize(x, eps):
    xc = x - jnp.mean(x, -1, keepdims=True)
    return xc * lax.rsqrt(jnp.mean(xc * xc, -1, keepdims=True) + eps)


def _rms_mod(x, g, shift, scale):
    y = x * lax.rsqrt(jnp.mean(x * x, -1, keepdims=True) + NORM_EPS) * g
    return y * (1.0 + scale) + shift


def _params(*sem):
    return pltpu.CompilerParams(dimension_semantics=sem, vmem_limit_bytes=VMEM_LIMIT)


def _mod_kernel(c_ref, w_ref, b_ref, o_ref):
    o_ref[...] = _mm(_silu(c_ref[...]), w_ref[...]) + b_ref[...]


def _modulation(cvec, w_mod, b_mod):
    depth = w_mod.shape[0]
    rows = cvec.shape[0]
    return pl.pallas_call(
        _mod_kernel,
        grid=(depth, 6),
        in_specs=[
            pl.BlockSpec((rows, D_MODEL), lambda l, j: (0, 0)),
            pl.BlockSpec((None, D_MODEL, D_MODEL), lambda l, j: (l, 0, j)),
            pl.BlockSpec((None, 1, D_MODEL), lambda l, j: (l, 0, j)),
        ],
        out_specs=pl.BlockSpec((None, rows, D_MODEL), lambda l, j: (l, 0, j)),
        out_shape=jax.ShapeDtypeStruct((depth, rows, 6 * D_MODEL), _F32),
        compiler_params=_params("arbitrary", "arbitrary"),
        name="modulation",
    )(cvec, w_mod, b_mod)


def _mod_spec(grid_rank):
    if grid_rank == 2:
        return pl.BlockSpec((None, 1, D_MODEL), lambda b, i: (2 * b + i // LAT_ROW_TILES, 0, 0))
    return pl.BlockSpec((None, 1, D_MODEL), lambda j, b, i: (2 * b + i // LAT_ROW_TILES, 0, 0))


def _inproj_kernel(x_ref, g_ref, sh_ref, sc_ref, w_ref, o_ref):
    h = _rms_mod(x_ref[...], g_ref[...], sh_ref[...], sc_ref[...])
    o_ref[...] = _mm(h, w_ref[...])


def _in_projection(x_all, g, shift, scale, w):
    batch = x_all.shape[0]
    n_col = 2
    tn = Z_COLS // n_col
    return pl.pallas_call(
        _inproj_kernel,
        grid=(n_col, batch, N_ROW_TILES),
        in_specs=[
            pl.BlockSpec((None, ROW_TILE, D_MODEL), lambda j, b, i: (b, i, 0)),
            pl.BlockSpec((1, D_MODEL), lambda j, b, i: (0, 0)),
            _mod_spec(3),
            _mod_spec(3),
            pl.BlockSpec((D_MODEL, tn), lambda j, b, i: (0, j)),
        ],
        out_specs=pl.BlockSpec((None, ROW_TILE, tn), lambda j, b, i: (b, i, j)),
        out_shape=jax.ShapeDtypeStruct((batch, T_ALL, Z_COLS), _F32),
        compiler_params=_params("arbitrary", "arbitrary", "arbitrary"),
        name="in_projection",
    )(x_all, g, shift, scale, w)


def _shift_kernel(z_ref, w_ref, o_ref):
    z = z_ref[...]
    t = lax.broadcasted_iota(jnp.int32, z.shape, 0)
    prev = pltpu.roll(z, 1, 0)
    nxt = pltpu.roll(z, T_ALL - 1, 0)
    prev = jnp.where((t == 0) | (t == SEQ), 0.0, prev)
    nxt = jnp.where((t == SEQ - 1) | (t == T_ALL - 1), 0.0, nxt)
    w = w_ref[...]
    o_ref[...] = prev * w[0:1] + z * w[1:2] + nxt * w[2:3]


def _token_shift(z, w_shift):
    batch = z.shape[0]
    first = Z_RWKV // LANES
    return pl.pallas_call(
        _shift_kernel,
        grid=(batch, RWKV_TILES),
        in_specs=[
            pl.BlockSpec((None, T_ALL, LANES), lambda b, j: (b, 0, first + j)),
            pl.BlockSpec((3, LANES), lambda b, j: (0, j)),
        ],
        out_specs=pl.BlockSpec((None, None, T_ALL, LANES), lambda b, j: (b, j, 0, 0)),
        out_shape=jax.ShapeDtypeStruct((batch, RWKV_TILES, T_ALL, LANES), _F32),
        compiler_params=_params("arbitrary", "arbitrary"),
        name="token_shift",
    )(z, w_shift)


def _head_sum(x):
    lane = lax.broadcasted_iota(jnp.int32, x.shape, 1)
    lo = jnp.sum(jnp.where(lane < HEAD_DIM, x, 0.0), -1, keepdims=True)
    tot = jnp.sum(x, -1, keepdims=True)
    return jnp.where(lane < HEAD_DIM, lo, tot - lo)


def _scan_kernel(r_ref, k_ref, v_ref, lw_ref, la_ref, wup_ref, w0_ref, aup_ref, a0_ref,
                 kk_ref, ka_ref, mi_ref, ms_ref, y_ref, s_ref):
    c = SCAN_CHUNK

    @pl.when(pl.program_id(3) == 0)
    def _():
        s_ref[...] = jnp.zeros_like(s_ref)

    r = r_ref[...]
    k = k_ref[...]
    v = v_ref[...]
    m_incl = mi_ref[...]
    m_strict = ms_ref[...]

    xw = w0_ref[...] + _mm_f32(jnp.tanh(lw_ref[...]), wup_ref[...])
    ld = -math.exp(-0.5) * _sigmoid(xw)
    a = _sigmoid(a0_ref[...] + _mm_f32(la_ref[...], aup_ref[...]))
    kk = k * kk_ref[...]
    kk = kk * lax.rsqrt(jnp.maximum(_head_sum(kk * kk), 1e-24))
    k_d = k * (1.0 + (a - 1.0) * ka_ref[...])
    kka = kk * a

    cl = _mm_f32(m_incl, ld)
    ld_tot = jnp.sum(ld, 0, keepdims=True)
    e_in = jnp.exp(cl)
    e_out = jnp.exp(-cl)
    kkt = kk * jnp.exp(cl - ld)
    rt = r * e_in
    kh = k_d * e_out
    kkah = kka * e_out
    e_end = jnp.exp(ld_tot - cl)
    k_end = k_d * e_end
    kka_end = kka * e_end
    dec_tot = jnp.exp(ld_tot)

    keep_strict = m_strict > 0.5
    keep_incl = m_incl > 0.5
    ys = []
    for h in range(2):
        sl = slice(h * HEAD_DIM, (h + 1) * HEAD_DIM)
        s_h = s_ref[h]
        v_h = v[:, sl]
        lhs = jnp.concatenate([kkt[:, sl], rt[:, sl]], axis=0)
        rhs = jnp.concatenate([kh[:, sl], kkah[:, sl]], axis=0)
        amat = _mm_nt(lhs, rhs)
        a_ak = jnp.where(keep_strict, amat[:c, :c], 0.0)
        a_ab = jnp.where(keep_strict, amat[:c, c:], 0.0)
        a_rk = jnp.where(keep_incl, amat[c:, :c], 0.0)
        a_rb = jnp.where(keep_incl, amat[c:, c:], 0.0)
        ls = _mm_nt(lhs, s_h)
        u = ls[:c] + _mm(a_ak, v_h)
        u = u - _mm(a_ab, u)
        n_pow = a_ab
        for _ in range(int(math.log2(c)) - 1):
            n_pow = _mm(n_pow, n_pow)
            u = u + _mm(n_pow, u)
        ys.append(ls[c:] + _mm(a_rk, v_h) - _mm(a_rb, u))
        upd = _mm_tn(jnp.concatenate([v_h, -u], axis=0),
                     jnp.concatenate([k_end[:, sl], kka_end[:, sl]], axis=0))
        s_ref[h] = s_h * dec_tot[:, sl] + upd
    y_ref[...] = jnp.concatenate(ys, axis=1)


def _scan_chunk_index(d, s):
    return jnp.where(d == 0, (s + LAT_CHUNKS) % N_CHUNKS, N_CHUNKS - 1 - s)


def _wkv_scan(zr, wup, w0, aup, a0, k_k, k_a, m_incl, m_strict):
    batch = zr.shape[0]
    c = SCAN_CHUNK

    def tile(first):
        return pl.BlockSpec((None, None, c, LANES),
                            lambda b, d, hp, s: (b, first + hp, _scan_chunk_index(d, s), 0))

    def lora_tile(idx):
        return pl.BlockSpec((None, None, c, LANES),
                            lambda b, d, hp, s: (b, idx, _scan_chunk_index(d, s), 0))

    return pl.pallas_call(
        _scan_kernel,
        grid=(batch, 2, HEAD_PAIRS, N_CHUNKS),
        in_specs=[
            tile(0), tile(HEAD_PAIRS), tile(2 * HEAD_PAIRS), lora_tile(TILE_W), lora_tile(TILE_A),
            pl.BlockSpec((None, LANES, LANES), lambda b, d, hp, s: (d, 0, hp)),
            pl.BlockSpec((None, 1, LANES), lambda b, d, hp, s: (d, 0, hp)),
            pl.BlockSpec((None, LANES, LANES), lambda b, d, hp, s: (d, 0, hp)),
            pl.BlockSpec((None, 1, LANES), lambda b, d, hp, s: (d, 0, hp)),
            pl.BlockSpec((1, LANES), lambda b, d, hp, s: (0, hp)),
            pl.BlockSpec((1, LANES), lambda b, d, hp, s: (0, hp)),
            pl.BlockSpec((None, c, c), lambda b, d, hp, s: (d, 0, 0)),
            pl.BlockSpec((None, c, c), lambda b, d, hp, s: (d, 0, 0)),
        ],
        out_specs=pl.BlockSpec((None, None, c, LANES),
                               lambda b, d, hp, s: (d, b, _scan_chunk_index(d, s), hp)),
        out_shape=jax.ShapeDtypeStruct((2, batch, T_ALL, D_MODEL), _F32),
        scratch_shapes=[pltpu.VMEM((2, HEAD_DIM, HEAD_DIM), _F32)],
        compiler_params=_params("arbitrary", "arbitrary", "arbitrary", "arbitrary"),
        name="wkv_scan",
    )(zr, zr, zr, zr, zr, wup, w0, aup, a0, k_k, k_a, m_incl, m_strict)


def _rwkv_post_kernel(y_ref, r_ref, k_ref, v_ref, gz_ref, gup_ref, gng_ref, gnb_ref, rk_ref, o_ref):
    gate = _mm(_sigmoid(gz_ref[...]), gup_ref[...])
    for hp in range(HEAD_PAIRS):
        cols = slice(hp * LANES, (hp + 1) * LANES)
        y = y_ref[0, :, cols] + y_ref[1, :, cols]
        mean = _head_sum(y) * (1.0 / HEAD_DIM)
        yc = y - mean
        var = _head_sum(yc * yc) * (1.0 / HEAD_DIM)
        yn = yc * lax.rsqrt(var + GN_EPS) * gng_ref[:, cols] + gnb_ref[:, cols]
        bonus = _head_sum(r_ref[hp] * k_ref[hp] * rk_ref[:, cols]) * v_ref[hp]
        o_ref[:, cols] = (yn + bonus) * gate[:, cols]


def _rwkv_post(y, zr, g_up, gn_g, gn_b, r_k):
    batch = zr.shape[0]

    def group(idx):
        return pl.BlockSpec((None, HEAD_PAIRS, ROW_TILE, LANES), lambda b, i: (b, idx, i, 0))

    vec = pl.BlockSpec((1, D_MODEL), lambda b, i: (0, 0))
    return pl.pallas_call(
        _rwkv_post_kernel,
        grid=(batch, N_ROW_TILES),
        in_specs=[
            pl.BlockSpec((2, None, ROW_TILE, D_MODEL), lambda b, i: (0, b, i, 0)),
            group(0), group(1), group(2),
            pl.BlockSpec((None, None, ROW_TILE, LANES), lambda b, i: (b, TILE_G, i, 0)),
            pl.BlockSpec((GATE_LORA, D_MODEL), lambda b, i: (0, 0)),
            vec, vec, vec,
        ],
        out_specs=pl.BlockSpec((None, ROW_TILE, D_MODEL), lambda b, i: (b, i, 0)),
        out_shape=jax.ShapeDtypeStruct((batch, T_ALL, D_MODEL), _F32),
        compiler_params=_params("arbitrary", "arbitrary"),
        name="rwkv_post",
    )(y, zr, zr, zr, zr, g_up, gn_g, gn_b, r_k)


_H_SLOT = GRID_W + 16
_H_LEAD = 16
_V_PAD = CONV_HALF * GRID_W
_CTX_LEAD = 16


def _conv_taps(pad_ref, w, base, length, stride):
    acc = None
    for j in range(CONV_WIDTH):
        term = pad_ref[pl.ds(base + (j - CONV_HALF) * stride, length), :] * w[j:j + 1]
        acc = term if acc is None else acc + term
    return acc


def _conv_kernel(horizontal, zv_ref, zg_ref, w_ref, b_ref, o_ref, lat_ref, ctx_ref):
    w = w_ref[...]
    bias = b_ref[...]
    lat_ref[...] = jnp.zeros_like(lat_ref)
    ctx_ref[...] = jnp.zeros_like(ctx_ref)
    ctx_ref[pl.ds(_CTX_LEAD, CTX_LEN), :] = (
        zv_ref[pl.ds(SEQ, CTX_LEN), :] * _sigmoid(zg_ref[pl.ds(SEQ, CTX_LEN), :]))
    o_ref[pl.ds(SEQ, CTX_LEN), :] = _conv_taps(ctx_ref, w, _CTX_LEAD, CTX_LEN, 1) + bias
    if horizontal:
        for row in range(GRID_H):
            src = pl.ds(row * GRID_W, GRID_W)
            lat_ref[pl.ds(_H_LEAD + row * _H_SLOT, GRID_W), :] = (
                zv_ref[src, :] * _sigmoid(zg_ref[src, :]))
        for row in range(GRID_H):
            o_ref[pl.ds(row * GRID_W, GRID_W), :] = (
                _conv_taps(lat_ref, w, _H_LEAD + row * _H_SLOT, GRID_W, 1) + bias)
    else:
        lat_ref[pl.ds(_V_PAD, SEQ), :] = (
            zv_ref[pl.ds(0, SEQ), :] * _sigmoid(zg_ref[pl.ds(0, SEQ), :]))
        blk = 4 * GRID_W
        for i in range(SEQ // blk):
            o_ref[pl.ds(i * blk, blk), :] = _conv_taps(lat_ref, w, _V_PAD + i * blk, blk, GRID_W) + bias


def _conformer_conv(z, dw, dw_b, horizontal):
    batch = z.shape[0]
    first = Z_CONV // LANES
    n_tiles = CONV_DIM // LANES
    lat_rows = (_H_LEAD + GRID_H * _H_SLOT) if horizontal else (SEQ + 2 * _V_PAD)
    return pl.pallas_call(
        functools.partial(_conv_kernel, horizontal),
        grid=(batch, n_tiles),
        in_specs=[
            pl.BlockSpec((None, T_ALL, LANES), lambda b, j: (b, 0, first + j)),
            pl.BlockSpec((None, T_ALL, LANES), lambda b, j: (b, 0, first + n_tiles + j)),
            pl.BlockSpec((CONV_WIDTH, LANES), lambda b, j: (0, j)),
            pl.BlockSpec((1, LANES), lambda b, j: (0, j)),
        ],
        out_specs=pl.BlockSpec((None, T_ALL, LANES), lambda b, j: (b, 0, j)),
        out_shape=jax.ShapeDtypeStruct((batch, T_ALL, CONV_DIM), _F32),
        scratch_shapes=[pltpu.VMEM((lat_rows, LANES), _F32),
                        pltpu.VMEM((CTX_LEN + 2 * _CTX_LEAD, LANES), _F32)],
        compiler_params=_params("arbitrary", "arbitrary"),
        name="conformer_conv_h" if horizontal else "conformer_conv_v",
    )(z, z, dw, dw_b)


def _sgu_kernel(zu_ref, zv_ref, lng_ref, lnb_ref, ws_ref, bs_ref, o_ref):
    v = _standardize(_gelu_tanh(zv_ref[...]), LN_EPS) * lng_ref[...] + lnb_ref[...]
    group = lax.broadcasted_iota(jnp.int32, (SGU_CHUNK, SGU_DIM), 1) // SGU_GROUP_DIM
    for n in range(ROW_TILE // SGU_CHUNK):
        rows = slice(n * SGU_CHUNK, (n + 1) * SGU_CHUNK)
        vc = v[rows]
        mixed = bs_ref[...]
        for g in range(SGU_GROUPS):
            mixed = jnp.where(group == g, mixed + _mm(ws_ref[g], vc), mixed)
        o_ref[rows, :] = _gelu_tanh(zu_ref[rows, :]) * mixed


def _sgu(z, ln_g, ln_b, w_s, b_tile):
    batch = z.shape[0]
    first = Z_SGU // SGU_DIM
    vec = pl.BlockSpec((1, SGU_DIM), lambda b, i: (0, 0))
    return pl.pallas_call(
        _sgu_kernel,
        grid=(batch, N_ROW_TILES),
        in_specs=[
            pl.BlockSpec((None, ROW_TILE, SGU_DIM), lambda b, i: (b, i, first)),
            pl.BlockSpec((None, ROW_TILE, SGU_DIM), lambda b, i: (b, i, first + 1)),
            vec, vec,
            pl.BlockSpec((SGU_GROUPS, SGU_CHUNK, SGU_CHUNK), lambda b, i: (0, 0, 0)),
            pl.BlockSpec((SGU_CHUNK, SGU_DIM), lambda b, i: (0, 0)),
        ],
        out_specs=pl.BlockSpec((None, ROW_TILE, SGU_DIM), lambda b, i: (b, i, 0)),
        out_shape=jax.ShapeDtypeStruct((batch, T_ALL, SGU_DIM), _F32),
        compiler_params=_params("arbitrary", "arbitrary"),
        name="sgu",
    )(z, z, ln_g, ln_b, w_s, b_tile)


def _merge_kernel(x_ref, a_ref, cv_ref, c_ref, g0_ref, g1_ref, g2_ref, gate_ref, lng_ref, lnb_ref,
                  wro_ref, wco_ref, wso_ref, wm_ref, o_ref):
    a = _mm(a_ref[...], wro_ref[...])
    cb = _silu(_standardize(cv_ref[...], LN_EPS) * lng_ref[...] + lnb_ref[...])
    b = _mm(cb, wco_ref[...])
    c = _mm(c_ref[...], wso_ref[...])
    m = a * _sigmoid(g0_ref[...]) + b * _sigmoid(g1_ref[...]) + c * _sigmoid(g2_ref[...])
    o_ref[...] = x_ref[...] + gate_ref[...] * _mm(m, wm_ref[...])


def _merge(x_all, a_pre, conv, c_pre, z, gate1, ln_g, ln_b, w_ro, w_co, w_so, w_m):
    batch = x_all.shape[0]

    def rows(width, col=0):
        return pl.BlockSpec((None, ROW_TILE, width), lambda b, i: (b, i, col))

    def whole(shape):
        return pl.BlockSpec(shape, lambda b, i: (0, 0))

    return pl.pallas_call(
        _merge_kernel,
        grid=(batch, N_ROW_TILES),
        in_specs=[
            rows(D_MODEL), rows(D_MODEL), rows(CONV_DIM), rows(SGU_DIM),
            rows(D_MODEL, 0), rows(D_MODEL, 1), rows(D_MODEL, 2),
            _mod_spec(2), whole((1, CONV_DIM)), whole((1, CONV_DIM)),
            whole((D_MODEL, D_MODEL)), whole((CONV_DIM, D_MODEL)), whole((SGU_DIM, D_MODEL)),
            whole((D_MODEL, D_MODEL)),
        ],
        out_specs=rows(D_MODEL),
        out_shape=jax.ShapeDtypeStruct((batch, T_ALL, D_MODEL), _F32),
        compiler_params=_params("arbitrary", "arbitrary"),
        name="merge",
    )(x_all, a_pre, conv, c_pre, z, z, z, gate1, ln_g, ln_b, w_ro, w_co, w_so, w_m)


def _ffn_kernel(x_ref, g_ref, sh_ref, sc_ref, gate_ref, win_ref, wout_ref, o_ref):
    x = x_ref[...]
    h = _rms_mod(x, g_ref[...], sh_ref[...], sc_ref[...]).astype(_BF)
    acc = jnp.zeros((ROW_TILE, D_MODEL), _F32)
    for f in range(D_FF // FF_CHUNK):
        cols = slice(f * FF_CHUNK, (f + 1) * FF_CHUNK)
        up_cols = slice(D_FF + f * FF_CHUNK, D_FF + (f + 1) * FF_CHUNK)
        act = _silu(_mm(h, win_ref[:, cols])) * _mm(h, win_ref[:, up_cols])
        acc = acc + _mm(act, wout_ref[cols, :])
    o_ref[...] = x + gate_ref[...] * acc


def _ffn(x_all, g, shift, scale, gate2, w_in, w_out):
    batch = x_all.shape[0]
    rows = pl.BlockSpec((None, ROW_TILE, D_MODEL), lambda b, i: (b, i, 0))
    return pl.pallas_call(
        _ffn_kernel,
        grid=(batch, N_ROW_TILES),
        in_specs=[
            rows,
            pl.BlockSpec((1, D_MODEL), lambda b, i: (0, 0)),
            _mod_spec(2), _mod_spec(2), _mod_spec(2),
            pl.BlockSpec((D_MODEL, 2 * D_FF), lambda b, i: (0, 0)),
            pl.BlockSpec((D_FF, D_MODEL), lambda b, i: (0, 0)),
        ],
        out_specs=rows,
        out_shape=jax.ShapeDtypeStruct((batch, T_ALL, D_MODEL), _F32),
        compiler_params=_params("arbitrary", "arbitrary"),
        name="swiglu",
    )(x_all, g, shift, scale, gate2, w_in, w_out)


def _final_norm_kernel(x_ref, g_ref, o_ref):
    x = x_ref[...]
    o_ref[...] = x * lax.rsqrt(jnp.mean(x * x, -1, keepdims=True) + NORM_EPS) * g_ref[...]


def _final_norm(x_all, g):
    batch = x_all.shape[0]
    rows = pl.BlockSpec((None, ROW_TILE, D_MODEL), lambda b, i: (b, i, 0))
    return pl.pallas_call(
        _final_norm_kernel,
        grid=(batch, LAT_ROW_TILES),
        in_specs=[rows, pl.BlockSpec((1, D_MODEL), lambda b, i: (0, 0))],
        out_specs=rows,
        out_shape=jax.ShapeDtypeStruct((batch, SEQ, D_MODEL), _F32),
        compiler_params=_params("arbitrary", "arbitrary"),
        name="final_norm",
    )(x_all, g)


def _permute_w_in(w):
    off_conv = RWKV_COLS
    off_sgu = off_conv + 2 * CONV_DIM
    off_gate = off_sgu + 2 * SGU_DIM
    pad = jnp.zeros((D_MODEL, Z_CONV - Z_RWKV - RWKV_COLS), w.dtype)
    return jnp.concatenate([w[:, off_gate:], w[:, :off_conv], pad, w[:, off_conv:off_sgu],
                            w[:, off_sgu:off_gate]], axis=1)


def _direction_padded(w_up):
    zero = jnp.zeros_like(w_up[0])
    return jnp.stack([jnp.concatenate([w_up[0], zero], 0), jnp.concatenate([zero, w_up[1]], 0)])


def kernel(x, c, ctx, c_ctx, w_mod, b_mod, norm1_g, norm2_g, w_in, rwkv_shift, rwkv_w0, rwkv_w_up, rwkv_a0, rwkv_a_up, rwkv_g_up, rwkv_k_k, rwkv_k_a, rwkv_r_k, rwkv_gn_g, rwkv_gn_b, rwkv_out, conv_dw, conv_dw_b, conv_ln_g, conv_ln_b, conv_out, sgu_ln_g, sgu_ln_b, sgu_w, sgu_b, sgu_out, w_merge, ffn_w_in, ffn_w_out, final_norm_g):
    batch = x.shape[0]
    depth = w_mod.shape[0]
    assert x.shape[1:] == (SEQ, D_MODEL) and ctx.shape[1:] == (CTX_LEN, D_MODEL)

    rows = -(-(batch + 1) // 8) * 8
    cvec = jnp.zeros((rows, D_MODEL), _F32).at[:batch].set(c).at[batch].set(c_ctx)
    mod = _modulation(cvec, w_mod.astype(_BF), b_mod[:, None, :])
    mod_lat = mod[:, :batch].reshape(depth, batch, 6, D_MODEL)
    mod_ctx = jnp.broadcast_to(mod[:, batch].reshape(depth, 1, 6, D_MODEL), mod_lat.shape)
    mod_tab = jnp.stack([mod_lat, mod_ctx], axis=2).transpose(0, 3, 1, 2, 4)
    mod_tab = mod_tab.reshape(depth, 6, 2 * batch, 1, D_MODEL)

    idx = jnp.arange(SCAN_CHUNK)
    fwd = idx[:, None] >= idx[None, :]
    m_incl = jnp.stack([fwd, fwd.T]).astype(_F32)
    m_strict = jnp.stack([idx[:, None] > idx[None, :], idx[:, None] < idx[None, :]]).astype(_F32)

    x_all = jnp.concatenate([x, ctx], axis=1)
    for l in range(depth):
        sh1, sc1, g1, sh2, sc2, g2 = (mod_tab[l, i] for i in range(6))
        z = _in_projection(x_all, norm1_g[l][None], sh1, sc1, _permute_w_in(w_in[l]).astype(_BF))
        zr = _token_shift(z, rwkv_shift[l])
        y = _wkv_scan(zr, _direction_padded(rwkv_w_up[l]), rwkv_w0[l][:, None, :],
                      _direction_padded(rwkv_a_up[l]), rwkv_a0[l][:, None, :],
                      rwkv_k_k[l][None], rwkv_k_a[l][None], m_incl, m_strict)
        a_pre = _rwkv_post(y, zr, rwkv_g_up[l].astype(_BF), rwkv_gn_g[l][None], rwkv_gn_b[l][None],
                           rwkv_r_k[l][None])
        conv = _conformer_conv(z, conv_dw[l], conv_dw_b[l][None], l % 2 == 0)
        b_tile = jnp.repeat(sgu_b[l].T, SGU_GROUP_DIM, axis=1)
        c_pre = _sgu(z, sgu_ln_g[l][None], sgu_ln_b[l][None], sgu_w[l].astype(_BF), b_tile)
        x_all = _merge(x_all, a_pre, conv, c_pre, z, g1, conv_ln_g[l][None], conv_ln_b[l][None],
                       rwkv_out[l].astype(_BF), conv_out[l].astype(_BF), sgu_out[l].astype(_BF),
                       w_merge[l].astype(_BF))
        x_all = _ffn(x_all, norm2_g[l][None], sh2, sc2, g2, ffn_w_in[l].astype(_BF),
                     ffn_w_out[l].astype(_BF))
    return _final_norm(x_all, final_norm_g[None])
```

```python
import functools
import math

import jax
import jax.numpy as jnp
from jax import lax
from jax.experimental import pallas as pl
from jax.experimental.pallas import tpu as pltpu

D_MODEL = 1024
SEQ = 2048
CTX_LEN = 256
T_ALL = SEQ + CTX_LEN
GRID_W = 64
GRID_H = SEQ // GRID_W

HEAD_DIM = 64
HEADS = D_MODEL // HEAD_DIM
HEAD_PAIRS = HEADS // 2
DECAY_LORA = 64
ICLR_LORA = 64
GATE_LORA = 128
CONV_DIM = D_MODEL // 2
CONV_WIDTH = 31
CONV_HALF = CONV_WIDTH // 2
SGU_DIM = D_MODEL // 2
SGU_GROUPS = 8
SGU_GROUP_DIM = SGU_DIM // SGU_GROUPS
SGU_CHUNK = 128
D_FF = ((8 * D_MODEL // 3 + 255) // 256) * 256
NORM_EPS = 1e-6
LN_EPS = 1e-5
GN_EPS = 64e-5

RWKV_COLS = 3 * D_MODEL + 2 * DECAY_LORA + 2 * ICLR_LORA + GATE_LORA

LANES = 128
Z_GATE = 0
Z_RWKV = 3 * D_MODEL
Z_CONV = Z_RWKV + RWKV_COLS + 128
Z_SGU = Z_CONV + 2 * CONV_DIM
Z_COLS = Z_SGU + 2 * SGU_DIM
RWKV_TILES = RWKV_COLS // LANES
TILE_W = 3 * HEAD_PAIRS
TILE_A = TILE_W + 1
TILE_G = TILE_A + 1

ROW_TILE = 256
N_ROW_TILES = T_ALL // ROW_TILE
LAT_ROW_TILES = SEQ // ROW_TILE
SCAN_CHUNK = 64
N_CHUNKS = T_ALL // SCAN_CHUNK
LAT_CHUNKS = SEQ // SCAN_CHUNK
FF_CHUNK = 256

VMEM_LIMIT = 56 * 1024 * 1024

_BF = jnp.bfloat16
_F32 = jnp.float32


def _mm(a, b):
    return jnp.dot(a.astype(_BF), b.astype(_BF), preferred_element_type=_F32)


def _mm_f32(a, b):
    return jnp.dot(a, b, preferred_element_type=_F32, precision=lax.Precision.HIGHEST)


def _mm_nt(a, b):
    return lax.dot_general(a.astype(_BF), b.astype(_BF), (((1,), (1,)), ((), ())),
                           preferred_element_type=_F32)


def _mm_tn(a, b):
    return lax.dot_general(a.astype(_BF), b.astype(_BF), (((0,), (0,)), ((), ())),
                           preferred_element_type=_F32)


def _sigmoid(x):
    return jax.nn.sigmoid(x)


def _silu(x):
    return x * jax.nn.sigmoid(x)


def _gelu_tanh(x):
    return 0.5 * x * (1.0 + jnp.tanh(math.sqrt(2.0 / math.pi) * (x + 0.044715 * (x * x * x))))


def _standardize(x, eps):
    xc = x - jnp.mean(x, -1, keepdims=True)
    return xc * lax.rsqrt(jnp.mean(xc * xc, -1, keepdims=True) + eps)


def _rms_mod(x, g, shift, scale):
    y = x * lax.rsqrt(jnp.mean(x * x, -1, keepdims=True) + NORM_EPS) * g
    return y * (1.0 + scale) + shift


def _params(*sem):
    return pltpu.CompilerParams(dimension_semantics=sem, vmem_limit_bytes=VMEM_LIMIT)


def _mod_kernel(c_ref, w_ref, b_ref, o_ref):
    o_ref[...] = _mm(_silu(c_ref[...]), w_ref[...]) + b_ref[...]


def _modulation(cvec, w_mod, b_mod):
    depth = w_mod.shape[0]
    rows = cvec.shape[0]
    return pl.pallas_call(
        _mod_kernel,
        grid=(depth, 6),
        in_specs=[
            pl.BlockSpec((rows, D_MODEL), lambda l, j: (0, 0)),
            pl.BlockSpec((None, D_MODEL, D_MODEL), lambda l, j: (l, 0, j)),
            pl.BlockSpec((None, 1, D_MODEL), lambda l, j: (l, 0, j)),
        ],
        out_specs=pl.BlockSpec((None, rows, D_MODEL), lambda l, j: (l, 0, j)),
        out_shape=jax.ShapeDtypeStruct((depth, rows, 6 * D_MODEL), _F32),
        compiler_params=_params("arbitrary", "arbitrary"),
        name="modulation",
    )(cvec, w_mod, b_mod)


def _mod_spec(grid_rank):
    if grid_rank == 2:
        return pl.BlockSpec((None, 1, D_MODEL), lambda b, i: (2 * b + i // LAT_ROW_TILES, 0, 0))
    return pl.BlockSpec((None, 1, D_MODEL), lambda j, b, i: (2 * b + i // LAT_ROW_TILES, 0, 0))


def _inproj_kernel(x_ref, g_ref, sh_ref, sc_ref, w_ref, o_ref):
    h = _rms_mod(x_ref[...], g_ref[...], sh_ref[...], sc_ref[...])
    o_ref[...] = _mm(h, w_ref[...])


def _in_projection(x_all, g, shift, scale, w):
    batch = x_all.shape[0]
    n_col = 2
    tn = Z_COLS // n_col
    return pl.pallas_call(
        _inproj_kernel,
        grid=(n_col, batch, N_ROW_TILES),
        in_specs=[
            pl.BlockSpec((None, ROW_TILE, D_MODEL), lambda j, b, i: (b, i, 0)),
            pl.BlockSpec((1, D_MODEL), lambda j, b, i: (0, 0)),
            _mod_spec(3),
            _mod_spec(3),
            pl.BlockSpec((D_MODEL, tn), lambda j, b, i: (0, j)),
        ],
        out_specs=pl.BlockSpec((None, ROW_TILE, tn), lambda j, b, i: (b, i, j)),
        out_shape=jax.ShapeDtypeStruct((batch, T_ALL, Z_COLS), _F32),
        compiler_params=_params("arbitrary", "arbitrary", "arbitrary"),
        name="in_projection",
    )(x_all, g, shift, scale, w)


def _shift_kernel(z_ref, w_ref, o_ref):
    z = z_ref[...]
    t = lax.broadcasted_iota(jnp.int32, z.shape, 0)
    prev = pltpu.roll(z, 1, 0)
    nxt = pltpu.roll(z, T_ALL - 1, 0)
    prev = jnp.where((t == 0) | (t == SEQ), 0.0, prev)
    nxt = jnp.where((t == SEQ - 1) | (t == T_ALL - 1), 0.0, nxt)
    w = w_ref[...]
    o_ref[...] = prev * w[0:1] + z * w[1:2] + nxt * w[2:3]


def _token_shift(z, w_shift):
    batch = z.shape[0]
    first = Z_RWKV // LANES
    return pl.pallas_call(
        _shift_kernel,
        grid=(batch, RWKV_TILES),
        in_specs=[
            pl.BlockSpec((None, T_ALL, LANES), lambda b, j: (b, 0, first + j)),
            pl.BlockSpec((3, LANES), lambda b, j: (0, j)),
        ],
        out_specs=pl.BlockSpec((None, None, T_ALL, LANES), lambda b, j: (b, j, 0, 0)),
        out_shape=jax.ShapeDtypeStruct((batch, RWKV_TILES, T_ALL, LANES), _F32),
        compiler_params=_params("arbitrary", "arbitrary"),
        name="token_shift",
    )(z, w_shift)


def _head_sum(x):
    lane = lax.broadcasted_iota(jnp.int32, x.shape, x.ndim - 1)
    lo = jnp.sum(jnp.where(lane < HEAD_DIM, x, 0.0), -1, keepdims=True)
    tot = jnp.sum(x, -1, keepdims=True)
    return jnp.where(lane < HEAD_DIM, lo, tot - lo)


def _bmm(a, b):
    return lax.dot_general(a.astype(_BF), b.astype(_BF), (((2,), (1,)), ((0,), (0,))),
                           preferred_element_type=_F32)


def _bmm_nt(a, b):
    return lax.dot_general(a.astype(_BF), b.astype(_BF), (((2,), (2,)), ((0,), (0,))),
                           preferred_element_type=_F32)


def _bmm_tn(a, b):
    return lax.dot_general(a.astype(_BF), b.astype(_BF), (((1,), (1,)), ((0,), (0,))),
                           preferred_element_type=_F32)


def _lane_tiles(x):
    return jnp.stack([x[:, i * LANES:(i + 1) * LANES] for i in range(HEAD_PAIRS)])


def _scan_kernel(r_ref, k_ref, v_ref, lw_ref, la_ref, wup_ref, w0_ref, aup_ref, a0_ref,
                 kk_ref, ka_ref, mi_ref, me_ref, mo_ref, y_ref, s_ref):
    c = SCAN_CHUNK

    @pl.when(pl.program_id(2) == 0)
    def _():
        s_ref[...] = jnp.zeros_like(s_ref)

    xw = w0_ref[...] + _mm(jnp.tanh(lw_ref[...]), wup_ref[...])
    ld = -math.exp(-0.5) * _sigmoid(xw)
    a = _lane_tiles(_sigmoid(a0_ref[...] + _mm(la_ref[...], aup_ref[...])))
    hi = ld.astype(_BF)
    rem = ld - hi.astype(_F32)
    mid = rem.astype(_BF)
    lo = (rem - mid.astype(_F32)).astype(_BF)
    m_incl = mi_ref[...]
    cl = (jnp.dot(m_incl, hi, preferred_element_type=_F32)
          + jnp.dot(m_incl, mid, preferred_element_type=_F32)
          + jnp.dot(m_incl, lo, preferred_element_type=_F32))
    ld_tot = jnp.sum(ld, 0, keepdims=True)
    e_in = _lane_tiles(jnp.exp(cl))
    e_out = _lane_tiles(jnp.exp(-cl))
    e_ex = _lane_tiles(jnp.exp(cl - ld))
    e_end = _lane_tiles(jnp.exp(ld_tot - cl))
    dec_tot = _lane_tiles(jnp.exp(ld_tot))

    r = r_ref[...]
    k = k_ref[...]
    v = v_ref[...]
    kk = k * _lane_tiles(kk_ref[...])
    kk = kk * lax.rsqrt(jnp.maximum(_head_sum(kk * kk), 1e-24))
    k_d = k * (1.0 + (a - 1.0) * _lane_tiles(ka_ref[...]))
    kka = kk * a

    even = lax.broadcasted_iota(jnp.int32, (1, 1, LANES), 2) < HEAD_DIM
    zc = jnp.zeros((HEAD_PAIRS, c, LANES), _F32)
    lhs = jnp.concatenate([kk * e_ex, r * e_in], axis=1)
    kh = k_d * e_out
    kkah = kka * e_out
    amat_e = _bmm_nt(jnp.where(even, lhs, 0.0), jnp.concatenate([kh, kkah], axis=1)) * me_ref[...]
    amat_o = _bmm_nt(jnp.where(even, 0.0, lhs), jnp.concatenate([kkah, kh], axis=1)) * mo_ref[...]
    sbd = s_ref[...]
    ls = _bmm_nt(lhs, sbd)
    z_e = jnp.where(even, ls[:, :c] + _bmm(amat_e[:, :c], jnp.concatenate([v, zc], axis=1)),
                    amat_e[:, :c])
    z_o = jnp.where(even, amat_o[:, :c],
                    ls[:, :c] + _bmm(amat_o[:, :c], jnp.concatenate([zc, v], axis=1)))
    for _ in range(int(math.log2(c))):
        z_e = _bmm(z_e, jnp.concatenate([zc, z_e], axis=1)) + jnp.where(even, z_e, 0.0)
        z_o = _bmm(z_o, jnp.concatenate([z_o, zc], axis=1)) + jnp.where(even, 0.0, z_o)
    u = jnp.where(even, z_e, z_o)
    y_ref[...] = ls[:, c:] + jnp.where(even,
                                       _bmm(amat_e[:, c:], jnp.concatenate([v, u], axis=1)),
                                       _bmm(amat_o[:, c:], jnp.concatenate([u, v], axis=1)))
    upd = _bmm_tn(jnp.concatenate([v, u], axis=1),
                  jnp.concatenate([k_d * e_end, -(kka * e_end)], axis=1))
    row_even = lax.broadcasted_iota(jnp.int32, (1, LANES, LANES), 1) < HEAD_DIM
    col_even = lax.broadcasted_iota(jnp.int32, (1, LANES, LANES), 2) < HEAD_DIM
    s_ref[...] = jnp.where(row_even == col_even, sbd * dec_tot + upd, 0.0)


def _scan_chunk_index(d, s):
    return jnp.where(d == 0, (s + LAT_CHUNKS) % N_CHUNKS, N_CHUNKS - 1 - s)


def _wkv_scan(zr, wup, w0, aup, a0, k_k, k_a, m_incl, m_even, m_odd):
    batch = zr.shape[0]
    c = SCAN_CHUNK

    def group(idx):
        return pl.BlockSpec((None, HEAD_PAIRS, c, LANES),
                            lambda b, d, s: (b, idx, _scan_chunk_index(d, s), 0))

    def lora_tile(idx):
        return pl.BlockSpec((None, None, c, LANES),
                            lambda b, d, s: (b, idx, _scan_chunk_index(d, s), 0))

    def per_dir(*shape):
        return pl.BlockSpec((None,) + shape, lambda b, d, s: (d, 0, 0))

    vec = pl.BlockSpec((1, D_MODEL), lambda b, d, s: (0, 0))
    return pl.pallas_call(
        _scan_kernel,
        grid=(batch, 2, N_CHUNKS),
        in_specs=[
            group(0), group(1), group(2), lora_tile(TILE_W), lora_tile(TILE_A),
            per_dir(LANES, D_MODEL), per_dir(1, D_MODEL),
            per_dir(LANES, D_MODEL), per_dir(1, D_MODEL),
            vec, vec,
            per_dir(c, c), per_dir(2 * c, 2 * c), per_dir(2 * c, 2 * c),
        ],
        out_specs=pl.BlockSpec((None, None, HEAD_PAIRS, c, LANES),
                               lambda b, d, s: (d, b, 0, _scan_chunk_index(d, s), 0)),
        out_shape=jax.ShapeDtypeStruct((2, batch, HEAD_PAIRS, T_ALL, LANES), _F32),
        scratch_shapes=[pltpu.VMEM((HEAD_PAIRS, LANES, LANES), _F32)],
        compiler_params=_params("arbitrary", "arbitrary", "arbitrary"),
        name="wkv_scan",
    )(zr, zr, zr, zr, zr, wup, w0, aup, a0, k_k, k_a, m_incl, m_even, m_odd)


def _rwkv_post_kernel(y_ref, r_ref, k_ref, v_ref, gz_ref, gup_ref, gng_ref, gnb_ref, rk_ref, o_ref):
    gate = _mm(_sigmoid(gz_ref[...]), gup_ref[...])
    for hp in range(HEAD_PAIRS):
        cols = slice(hp * LANES, (hp + 1) * LANES)
        y = y_ref[0, hp] + y_ref[1, hp]
        mean = _head_sum(y) * (1.0 / HEAD_DIM)
        yc = y - mean
        var = _head_sum(yc * yc) * (1.0 / HEAD_DIM)
        yn = yc * lax.rsqrt(var + GN_EPS) * gng_ref[:, cols] + gnb_ref[:, cols]
        bonus = _head_sum(r_ref[hp] * k_ref[hp] * rk_ref[:, cols]) * v_ref[hp]
        o_ref[:, cols] = (yn + bonus) * gate[:, cols]


def _rwkv_post(y, zr, g_up, gn_g, gn_b, r_k):
    batch = zr.shape[0]

    def group(idx):
        return pl.BlockSpec((None, HEAD_PAIRS, ROW_TILE, LANES), lambda b, i: (b, idx, i, 0))

    vec = pl.BlockSpec((1, D_MODEL), lambda b, i: (0, 0))
    return pl.pallas_call(
        _rwkv_post_kernel,
        grid=(batch, N_ROW_TILES),
        in_specs=[
            pl.BlockSpec((2, None, HEAD_PAIRS, ROW_TILE, LANES), lambda b, i: (0, b, 0, i, 0)),
            group(0), group(1), group(2),
            pl.BlockSpec((None, None, ROW_TILE, LANES), lambda b, i: (b, TILE_G, i, 0)),
            pl.BlockSpec((GATE_LORA, D_MODEL), lambda b, i: (0, 0)),
            vec, vec, vec,
        ],
        out_specs=pl.BlockSpec((None, ROW_TILE, D_MODEL), lambda b, i: (b, i, 0)),
        out_shape=jax.ShapeDtypeStruct((batch, T_ALL, D_MODEL), _F32),
        compiler_params=_params("arbitrary", "arbitrary"),
        name="rwkv_post",
    )(y, zr, zr, zr, zr, g_up, gn_g, gn_b, r_k)


_H_SLOT = GRID_W + 16
_H_LEAD = 16
_V_PAD = CONV_HALF * GRID_W
_CTX_LEAD = 16


def _conv_taps(pad_ref, w, base, length, stride):
    acc = None
    for j in range(CONV_WIDTH):
        term = pad_ref[pl.ds(base + (j - CONV_HALF) * stride, length), :] * w[j:j + 1]
        acc = term if acc is None else acc + term
    return acc


def _conv_kernel(horizontal, zv_ref, zg_ref, w_ref, b_ref, o_ref, lat_ref, ctx_ref):
    w = w_ref[...]
    bias = b_ref[...]
    lat_ref[...] = jnp.zeros_like(lat_ref)
    ctx_ref[...] = jnp.zeros_like(ctx_ref)
    ctx_ref[pl.ds(_CTX_LEAD, CTX_LEN), :] = (
        zv_ref[pl.ds(SEQ, CTX_LEN), :] * _sigmoid(zg_ref[pl.ds(SEQ, CTX_LEN), :]))
    o_ref[pl.ds(SEQ, CTX_LEN), :] = _conv_taps(ctx_ref, w, _CTX_LEAD, CTX_LEN, 1) + bias
    if horizontal:
        for row in range(GRID_H):
            src = pl.ds(row * GRID_W, GRID_W)
            lat_ref[pl.ds(_H_LEAD + row * _H_SLOT, GRID_W), :] = (
                zv_ref[src, :] * _sigmoid(zg_ref[src, :]))
        for row in range(GRID_H):
            o_ref[pl.ds(row * GRID_W, GRID_W), :] = (
                _conv_taps(lat_ref, w, _H_LEAD + row * _H_SLOT, GRID_W, 1) + bias)
    else:
        lat_ref[pl.ds(_V_PAD, SEQ), :] = (
            zv_ref[pl.ds(0, SEQ), :] * _sigmoid(zg_ref[pl.ds(0, SEQ), :]))
        blk = 4 * GRID_W
        for i in range(SEQ // blk):
            o_ref[pl.ds(i * blk, blk), :] = _conv_taps(lat_ref, w, _V_PAD + i * blk, blk, GRID_W) + bias


def _conformer_conv(z, dw, dw_b, horizontal):
    batch = z.shape[0]
    first = Z_CONV // LANES
    n_tiles = CONV_DIM // LANES
    lat_rows = (_H_LEAD + GRID_H * _H_SLOT) if horizontal else (SEQ + 2 * _V_PAD)
    return pl.pallas_call(
        functools.partial(_conv_kernel, horizontal),
        grid=(batch, n_tiles),
        in_specs=[
            pl.BlockSpec((None, T_ALL, LANES), lambda b, j: (b, 0, first + j)),
            pl.BlockSpec((None, T_ALL, LANES), lambda b, j: (b, 0, first + n_tiles + j)),
            pl.BlockSpec((CONV_WIDTH, LANES), lambda b, j: (0, j)),
            pl.BlockSpec((1, LANES), lambda b, j: (0, j)),
        ],
        out_specs=pl.BlockSpec((None, T_ALL, LANES), lambda b, j: (b, 0, j)),
        out_shape=jax.ShapeDtypeStruct((batch, T_ALL, CONV_DIM), _F32),
        scratch_shapes=[pltpu.VMEM((lat_rows, LANES), _F32),
                        pltpu.VMEM((CTX_LEN + 2 * _CTX_LEAD, LANES), _F32)],
        compiler_params=_params("arbitrary", "arbitrary"),
        name="conformer_conv_h" if horizontal else "conformer_conv_v",
    )(z, z, dw, dw_b)


def _sgu_kernel(zu_ref, zv_ref, lng_ref, lnb_ref, ws_ref, bs_ref, o_ref):
    v = _standardize(_gelu_tanh(zv_ref[...]), LN_EPS) * lng_ref[...] + lnb_ref[...]
    group = lax.broadcasted_iota(jnp.int32, (SGU_CHUNK, SGU_DIM), 1) // SGU_GROUP_DIM
    for n in range(ROW_TILE // SGU_CHUNK):
        rows = slice(n * SGU_CHUNK, (n + 1) * SGU_CHUNK)
        vc = v[rows]
        mixed = bs_ref[...]
        for g in range(SGU_GROUPS):
            mixed = jnp.where(group == g, mixed + _mm(ws_ref[g], vc), mixed)
        o_ref[rows, :] = _gelu_tanh(zu_ref[rows, :]) * mixed


def _sgu(z, ln_g, ln_b, w_s, b_tile):
    batch = z.shape[0]
    first = Z_SGU // SGU_DIM
    vec = pl.BlockSpec((1, SGU_DIM), lambda b, i: (0, 0))
    return pl.pallas_call(
        _sgu_kernel,
        grid=(batch, N_ROW_TILES),
        in_specs=[
            pl.BlockSpec((None, ROW_TILE, SGU_DIM), lambda b, i: (b, i, first)),
            pl.BlockSpec((None, ROW_TILE, SGU_DIM), lambda b, i: (b, i, first + 1)),
            vec, vec,
            pl.BlockSpec((SGU_GROUPS, SGU_CHUNK, SGU_CHUNK), lambda b, i: (0, 0, 0)),
            pl.BlockSpec((SGU_CHUNK, SGU_DIM), lambda b, i: (0, 0)),
        ],
        out_specs=pl.BlockSpec((None, ROW_TILE, SGU_DIM), lambda b, i: (b, i, 0)),
        out_shape=jax.ShapeDtypeStruct((batch, T_ALL, SGU_DIM), _F32),
        compiler_params=_params("arbitrary", "arbitrary"),
        name="sgu",
    )(z, z, ln_g, ln_b, w_s, b_tile)


def _merge_kernel(x_ref, a_ref, cv_ref, c_ref, g0_ref, g1_ref, g2_ref, gate_ref, lng_ref, lnb_ref,
                  wro_ref, wco_ref, wso_ref, wm_ref, o_ref):
    a = _mm(a_ref[...], wro_ref[...])
    cb = _silu(_standardize(cv_ref[...], LN_EPS) * lng_ref[...] + lnb_ref[...])
    b = _mm(cb, wco_ref[...])
    c = _mm(c_ref[...], wso_ref[...])
    m = a * _sigmoid(g0_ref[...]) + b * _sigmoid(g1_ref[...]) + c * _sigmoid(g2_ref[...])
    o_ref[...] = x_ref[...] + gate_ref[...] * _mm(m, wm_ref[...])


def _merge(x_all, a_pre, conv, c_pre, z, gate1, ln_g, ln_b, w_ro, w_co, w_so, w_m):
    batch = x_all.shape[0]

    def rows(width, col=0):
        return pl.BlockSpec((None, ROW_TILE, width), lambda b, i: (b, i, col))

    def whole(shape):
        return pl.BlockSpec(shape, lambda b, i: (0, 0))

    return pl.pallas_call(
        _merge_kernel,
        grid=(batch, N_ROW_TILES),
        in_specs=[
            rows(D_MODEL), rows(D_MODEL), rows(CONV_DIM), rows(SGU_DIM),
            rows(D_MODEL, 0), rows(D_MODEL, 1), rows(D_MODEL, 2),
            _mod_spec(2), whole((1, CONV_DIM)), whole((1, CONV_DIM)),
            whole((D_MODEL, D_MODEL)), whole((CONV_DIM, D_MODEL)), whole((SGU_DIM, D_MODEL)),
            whole((D_MODEL, D_MODEL)),
        ],
        out_specs=rows(D_MODEL),
        out_shape=jax.ShapeDtypeStruct((batch, T_ALL, D_MODEL), _F32),
        compiler_params=_params("arbitrary", "arbitrary"),
        name="merge",
    )(x_all, a_pre, conv, c_pre, z, z, z, gate1, ln_g, ln_b, w_ro, w_co, w_so, w_m)


def _ffn_kernel(x_ref, g_ref, sh_ref, sc_ref, gate_ref, win_ref, wout_ref, o_ref):
    x = x_ref[...]
    h = _rms_mod(x, g_ref[...], sh_ref[...], sc_ref[...]).astype(_BF)
    acc = jnp.zeros((ROW_TILE, D_MODEL), _F32)
    for f in range(D_FF // FF_CHUNK):
        cols = slice(f * FF_CHUNK, (f + 1) * FF_CHUNK)
        up_cols = slice(D_FF + f * FF_CHUNK, D_FF + (f + 1) * FF_CHUNK)
        act = _silu(_mm(h, win_ref[:, cols])) * _mm(h, win_ref[:, up_cols])
        acc = acc + _mm(act, wout_ref[cols, :])
    o_ref[...] = x + gate_ref[...] * acc


def _ffn(x_all, g, shift, scale, gate2, w_in, w_out):
    batch = x_all.shape[0]
    rows = pl.BlockSpec((None, ROW_TILE, D_MODEL), lambda b, i: (b, i, 0))
    return pl.pallas_call(
        _ffn_kernel,
        grid=(batch, N_ROW_TILES),
        in_specs=[
            rows,
            pl.BlockSpec((1, D_MODEL), lambda b, i: (0, 0)),
            _mod_spec(2), _mod_spec(2), _mod_spec(2),
            pl.BlockSpec((D_MODEL, 2 * D_FF), lambda b, i: (0, 0)),
            pl.BlockSpec((D_FF, D_MODEL), lambda b, i: (0, 0)),
        ],
        out_specs=rows,
        out_shape=jax.ShapeDtypeStruct((batch, T_ALL, D_MODEL), _F32),
        compiler_params=_params("arbitrary", "arbitrary"),
        name="swiglu",
    )(x_all, g, shift, scale, gate2, w_in, w_out)


def _final_norm_kernel(x_ref, g_ref, o_ref):
    x = x_ref[...]
    o_ref[...] = x * lax.rsqrt(jnp.mean(x * x, -1, keepdims=True) + NORM_EPS) * g_ref[...]


def _final_norm(x_all, g):
    batch = x_all.shape[0]
    rows = pl.BlockSpec((None, ROW_TILE, D_MODEL), lambda b, i: (b, i, 0))
    return pl.pallas_call(
        _final_norm_kernel,
        grid=(batch, LAT_ROW_TILES),
        in_specs=[rows, pl.BlockSpec((1, D_MODEL), lambda b, i: (0, 0))],
        out_specs=rows,
        out_shape=jax.ShapeDtypeStruct((batch, SEQ, D_MODEL), _F32),
        compiler_params=_params("arbitrary", "arbitrary"),
        name="final_norm",
    )(x_all, g)


def _permute_w_in(w):
    off_conv = RWKV_COLS
    off_sgu = off_conv + 2 * CONV_DIM
    off_gate = off_sgu + 2 * SGU_DIM
    pad = jnp.zeros((D_MODEL, Z_CONV - Z_RWKV - RWKV_COLS), w.dtype)
    return jnp.concatenate([w[:, off_gate:], w[:, :off_conv], pad, w[:, off_conv:off_sgu],
                            w[:, off_sgu:off_gate]], axis=1)


def _direction_padded(w_up):
    zero = jnp.zeros_like(w_up[0])
    return jnp.stack([jnp.concatenate([w_up[0], zero], 0), jnp.concatenate([zero, w_up[1]], 0)])


def kernel(x, c, ctx, c_ctx, w_mod, b_mod, norm1_g, norm2_g, w_in, rwkv_shift, rwkv_w0, rwkv_w_up, rwkv_a0, rwkv_a_up, rwkv_g_up, rwkv_k_k, rwkv_k_a, rwkv_r_k, rwkv_gn_g, rwkv_gn_b, rwkv_out, conv_dw, conv_dw_b, conv_ln_g, conv_ln_b, conv_out, sgu_ln_g, sgu_ln_b, sgu_w, sgu_b, sgu_out, w_merge, ffn_w_in, ffn_w_out, final_norm_g):
    batch = x.shape[0]
    depth = w_mod.shape[0]
    assert x.shape[1:] == (SEQ, D_MODEL) and ctx.shape[1:] == (CTX_LEN, D_MODEL)

    rows = -(-(batch + 1) // 8) * 8
    cvec = jnp.zeros((rows, D_MODEL), _F32).at[:batch].set(c).at[batch].set(c_ctx)
    mod = _modulation(cvec, w_mod.astype(_BF), b_mod[:, None, :])
    mod_lat = mod[:, :batch].reshape(depth, batch, 6, D_MODEL)
    mod_ctx = jnp.broadcast_to(mod[:, batch].reshape(depth, 1, 6, D_MODEL), mod_lat.shape)
    mod_tab = jnp.stack([mod_lat, mod_ctx], axis=2).transpose(0, 3, 1, 2, 4)
    mod_tab = mod_tab.reshape(depth, 6, 2 * batch, 1, D_MODEL)

    idx = jnp.arange(SCAN_CHUNK)
    incl = jnp.stack([idx[:, None] >= idx[None, :], idx[:, None] <= idx[None, :]]).astype(_F32)
    strict = jnp.stack([idx[:, None] > idx[None, :], idx[:, None] < idx[None, :]]).astype(_F32)
    m_even = jnp.concatenate([jnp.concatenate([strict, -strict], 2),
                              jnp.concatenate([incl, -incl], 2)], 1)
    m_odd = jnp.concatenate([jnp.concatenate([-strict, strict], 2),
                             jnp.concatenate([-incl, incl], 2)], 1)
    m_incl = incl.astype(_BF)

    x_all = jnp.concatenate([x, ctx], axis=1)
    for l in range(depth):
        sh1, sc1, g1, sh2, sc2, g2 = (mod_tab[l, i] for i in range(6))
        z = _in_projection(x_all, norm1_g[l][None], sh1, sc1, _permute_w_in(w_in[l]).astype(_BF))
        zr = _token_shift(z, rwkv_shift[l])
        y = _wkv_scan(zr, _direction_padded(rwkv_w_up[l]).astype(_BF), rwkv_w0[l][:, None, :],
                      _direction_padded(rwkv_a_up[l]).astype(_BF), rwkv_a0[l][:, None, :],
                      rwkv_k_k[l][None], rwkv_k_a[l][None], m_incl, m_even, m_odd)
        a_pre = _rwkv_post(y, zr, rwkv_g_up[l].astype(_BF), rwkv_gn_g[l][None], rwkv_gn_b[l][None],
                           rwkv_r_k[l][None])
        conv = _conformer_conv(z, conv_dw[l], conv_dw_b[l][None], l % 2 == 0)
        b_tile = jnp.repeat(sgu_b[l].T, SGU_GROUP_DIM, axis=1)
        c_pre = _sgu(z, sgu_ln_g[l][None], sgu_ln_b[l][None], sgu_w[l].astype(_BF), b_tile)
        x_all = _merge(x_all, a_pre, conv, c_pre, z, g1, conv_ln_g[l][None], conv_ln_b[l][None],
                       rwkv_out[l].astype(_BF), conv_out[l].astype(_BF), sgu_out[l].astype(_BF),
                       w_merge[l].astype(_BF))
        x_all = _ffn(x_all, norm2_g[l][None], sh2, sc2, g2, ffn_w_in[l].astype(_BF),
                     ffn_w_out[l].astype(_BF))
    return _final_norm(x_all, final_norm_g[None])
```

```python
import functools
import math

import jax
import jax.numpy as jnp
from jax import lax
from jax.experimental import pallas as pl
from jax.experimental.pallas import tpu as pltpu

D_MODEL = 1024
SEQ = 2048
CTX_LEN = 256
T_ALL = SEQ + CTX_LEN
GRID_W = 64
GRID_H = SEQ // GRID_W

HEAD_DIM = 64
HEADS = D_MODEL // HEAD_DIM
HEAD_PAIRS = HEADS // 2
DECAY_LORA = 64
ICLR_LORA = 64
GATE_LORA = 128
CONV_DIM = D_MODEL // 2
CONV_WIDTH = 31
CONV_HALF = CONV_WIDTH // 2
SGU_DIM = D_MODEL // 2
SGU_GROUPS = 8
SGU_GROUP_DIM = SGU_DIM // SGU_GROUPS
SGU_CHUNK = 128
D_FF = ((8 * D_MODEL // 3 + 255) // 256) * 256
NORM_EPS = 1e-6
LN_EPS = 1e-5
GN_EPS = 64e-5

RWKV_COLS = 3 * D_MODEL + 2 * DECAY_LORA + 2 * ICLR_LORA + GATE_LORA

LANES = 128
Z_GATE = 0
Z_RWKV = 3 * D_MODEL
Z_CONV = Z_RWKV + RWKV_COLS + 128
Z_SGU = Z_CONV + 2 * CONV_DIM
Z_COLS = Z_SGU + 2 * SGU_DIM
RWKV_TILES = RWKV_COLS // LANES
TILE_W = 3 * HEAD_PAIRS
TILE_A = TILE_W + 1
TILE_G = TILE_A + 1

ROW_TILE = 256
N_ROW_TILES = T_ALL // ROW_TILE
LAT_ROW_TILES = SEQ // ROW_TILE
SCAN_CHUNK = 64
N_CHUNKS = T_ALL // SCAN_CHUNK
LAT_CHUNKS = SEQ // SCAN_CHUNK
FF_CHUNK = 256

VMEM_LIMIT = 56 * 1024 * 1024

_BF = jnp.bfloat16
_F32 = jnp.float32


def _mm(a, b):
    return jnp.dot(a.astype(_BF), b.astype(_BF), preferred_element_type=_F32)


def _mm_f32(a, b):
    return jnp.dot(a, b, preferred_element_type=_F32, precision=lax.Precision.HIGHEST)


def _mm_nt(a, b):
    return lax.dot_general(a.astype(_BF), b.astype(_BF), (((1,), (1,)), ((), ())),
                           preferred_element_type=_F32)


def _mm_tn(a, b):
    return lax.dot_general(a.astype(_BF), b.astype(_BF), (((0,), (0,)), ((), ())),
                           preferred_element_type=_F32)


def _sigmoid(x):
    return jax.nn.sigmoid(x)


def _silu(x):
    return x * jax.nn.sigmoid(x)


def _gelu_tanh(x):
    return 0.5 * x * (1.0 + jnp.tanh(math.sqrt(2.0 / math.pi) * (x + 0.044715 * (x * x * x))))


def _standardize(x, eps):
    xc = x - jnp.mean(x, -1, keepdims=True)
    return xc * lax.rsqrt(jnp.mean(xc * xc, -1, keepdims=True) + eps)


def _rms_mod(x, g, shift, scale):
    y = x * lax.rsqrt(jnp.mean(x * x, -1, keepdims=True) + NORM_EPS) * g
    return y * (1.0 + scale) + shift


def _params(*sem):
    return pltpu.CompilerParams(dimension_semantics=sem, vmem_limit_bytes=VMEM_LIMIT)


def _mod_kernel(c_ref, w_ref, b_ref, o_ref):
    o_ref[...] = _mm(_silu(c_ref[...]), w_ref[...]) + b_ref[...]


def _modulation(cvec, w_mod, b_mod):
    depth = w_mod.shape[0]
    rows = cvec.shape[0]
    return pl.pallas_call(
        _mod_kernel,
        grid=(depth, 6),
        in_specs=[
            pl.BlockSpec((rows, D_MODEL), lambda l, j: (0, 0)),
            pl.BlockSpec((None, D_MODEL, D_MODEL), lambda l, j: (l, 0, j)),
            pl.BlockSpec((None, 1, D_MODEL), lambda l, j: (l, 0, j)),
        ],
        out_specs=pl.BlockSpec((None, rows, D_MODEL), lambda l, j: (l, 0, j)),
        out_shape=jax.ShapeDtypeStruct((depth, rows, 6 * D_MODEL), _F32),
        compiler_params=_params("arbitrary", "arbitrary"),
        name="modulation",
    )(cvec, w_mod, b_mod)


def _mod_spec(grid_rank):
    if grid_rank == 2:
        return pl.BlockSpec((None, 1, D_MODEL), lambda b, i: (2 * b + i // LAT_ROW_TILES, 0, 0))
    return pl.BlockSpec((None, 1, D_MODEL), lambda j, b, i: (2 * b + i // LAT_ROW_TILES, 0, 0))


def _inproj_kernel(x_ref, g_ref, sh_ref, sc_ref, w_ref, o_ref):
    h = _rms_mod(x_ref[...], g_ref[...], sh_ref[...], sc_ref[...])
    o_ref[...] = _mm(h, w_ref[...])


def _in_projection(x_all, g, shift, scale, w):
    batch = x_all.shape[0]
    n_col = 2
    tn = Z_COLS // n_col
    return pl.pallas_call(
        _inproj_kernel,
        grid=(n_col, batch, N_ROW_TILES),
        in_specs=[
            pl.BlockSpec((None, ROW_TILE, D_MODEL), lambda j, b, i: (b, i, 0)),
            pl.BlockSpec((1, D_MODEL), lambda j, b, i: (0, 0)),
            _mod_spec(3),
            _mod_spec(3),
            pl.BlockSpec((D_MODEL, tn), lambda j, b, i: (0, j)),
        ],
        out_specs=pl.BlockSpec((None, ROW_TILE, tn), lambda j, b, i: (b, i, j)),
        out_shape=jax.ShapeDtypeStruct((batch, T_ALL, Z_COLS), _F32),
        compiler_params=_params("arbitrary", "arbitrary", "arbitrary"),
        name="in_projection",
    )(x_all, g, shift, scale, w)


def _shift_kernel(z_ref, w_ref, o_ref):
    z = z_ref[...]
    t = lax.broadcasted_iota(jnp.int32, z.shape, 0)
    prev = pltpu.roll(z, 1, 0)
    nxt = pltpu.roll(z, T_ALL - 1, 0)
    prev = jnp.where((t == 0) | (t == SEQ), 0.0, prev)
    nxt = jnp.where((t == SEQ - 1) | (t == T_ALL - 1), 0.0, nxt)
    w = w_ref[...]
    o_ref[...] = prev * w[0:1] + z * w[1:2] + nxt * w[2:3]


def _token_shift(z, w_shift):
    batch = z.shape[0]
    first = Z_RWKV // LANES
    return pl.pallas_call(
        _shift_kernel,
        grid=(batch, RWKV_TILES),
        in_specs=[
            pl.BlockSpec((None, T_ALL, LANES), lambda b, j: (b, 0, first + j)),
            pl.BlockSpec((3, LANES), lambda b, j: (0, j)),
        ],
        out_specs=pl.BlockSpec((None, None, T_ALL, LANES), lambda b, j: (b, j, 0, 0)),
        out_shape=jax.ShapeDtypeStruct((batch, RWKV_TILES, T_ALL, LANES), _F32),
        compiler_params=_params("arbitrary", "arbitrary"),
        name="token_shift",
    )(z, w_shift)


def _head_sum(x):
    lane = lax.broadcasted_iota(jnp.int32, x.shape, x.ndim - 1)
    lo = jnp.sum(jnp.where(lane < HEAD_DIM, x, 0.0), -1, keepdims=True)
    tot = jnp.sum(x, -1, keepdims=True)
    return jnp.where(lane < HEAD_DIM, lo, tot - lo)


def _bmm(a, b):
    return lax.dot_general(a.astype(_BF), b.astype(_BF), (((2,), (1,)), ((0,), (0,))),
                           preferred_element_type=_F32)


def _bmm_nt(a, b):
    return lax.dot_general(a.astype(_BF), b.astype(_BF), (((2,), (2,)), ((0,), (0,))),
                           preferred_element_type=_F32)


def _bmm_tn(a, b):
    return lax.dot_general(a.astype(_BF), b.astype(_BF), (((1,), (1,)), ((0,), (0,))),
                           preferred_element_type=_F32)


def _lane_tiles(x):
    return jnp.stack([x[:, i * LANES:(i + 1) * LANES] for i in range(HEAD_PAIRS)])


N_SLOT_REFS = 5


def _scan_kernel(*refs):
    tokens = (refs[0:5], refs[5:10])
    wup_ref, w0_ref, aup_ref, a0_ref, kk_ref, ka_ref, mi_ref, mt_ref, mb_ref = refs[10:19]
    y_refs = refs[19:21]
    s_ref = refs[21]
    slots = (refs[22:22 + N_SLOT_REFS], refs[22 + N_SLOT_REFS:])
    step = pl.program_id(1)

    @pl.when(step == 0)
    def _():
        s_ref[...] = jnp.zeros_like(s_ref)
        for ref in slots[1]:
            ref[...] = jnp.zeros_like(ref)

    def run(prep, solve):
        for d in range(2):
            _scan_prepare(tokens[d], (wup_ref.at[d], w0_ref.at[d], aup_ref.at[d], a0_ref.at[d],
                                      kk_ref, ka_ref, mi_ref.at[d]), slots[prep], d)
        _scan_solve(slots[solve], mt_ref, mb_ref, y_refs, s_ref)

    @pl.when(step % 2 == 0)
    def _():
        run(0, 1)

    @pl.when(step % 2 == 1)
    def _():
        run(1, 0)


def _scan_prepare(token_refs, param_refs, slot, d):
    r_ref, k_ref, v_ref, lw_ref, la_ref = token_refs
    wup_ref, w0_ref, aup_ref, a0_ref, kk_ref, ka_ref, mi_ref = param_refs
    lhs_ref, rk_ref, vv_ref, ke_ref, dt_ref = slot
    xw = w0_ref[...] + _mm(jnp.tanh(lw_ref[...]), wup_ref[...])
    ld = -math.exp(-0.5) * _sigmoid(xw)
    a = _lane_tiles(_sigmoid(a0_ref[...] + _mm(la_ref[...], aup_ref[...])))
    hi = ld.astype(_BF)
    rem = ld - hi.astype(_F32)
    mid = rem.astype(_BF)
    lo = (rem - mid.astype(_F32)).astype(_BF)
    m_incl = mi_ref[...]
    cl = (jnp.dot(m_incl, hi, preferred_element_type=_F32)
          + jnp.dot(m_incl, mid, preferred_element_type=_F32)
          + jnp.dot(m_incl, lo, preferred_element_type=_F32))
    ld_tot = jnp.sum(ld, 0, keepdims=True)
    e_in = _lane_tiles(jnp.exp(cl))
    e_out = _lane_tiles(jnp.exp(-cl))
    e_ex = _lane_tiles(jnp.exp(cl - ld))
    e_end = _lane_tiles(jnp.exp(ld_tot - cl))
    dec_tot = _lane_tiles(jnp.exp(ld_tot))

    r = r_ref[...]
    k = k_ref[...]
    v = v_ref[...]
    kk = k * _lane_tiles(kk_ref[...])
    kk = kk * lax.rsqrt(jnp.maximum(_head_sum(kk * kk), 1e-24))
    k_d = k * (1.0 + (a - 1.0) * _lane_tiles(ka_ref[...]))
    kka = kk * a

    rows = pl.ds(d * HEAD_PAIRS, HEAD_PAIRS)
    lhs_ref[rows] = jnp.concatenate([kk * e_ex, r * e_in], axis=1).astype(_BF)
    rk_ref[rows] = jnp.concatenate([k_d * e_out, kka * e_out], axis=1).astype(_BF)
    vv_ref[rows] = v.astype(_BF)
    ke_ref[rows] = jnp.concatenate([k_d * e_end, -(kka * e_end)], axis=1).astype(_BF)
    dt_ref[rows] = dec_tot


def _by_direction(x, m_ref):
    return jnp.concatenate([x[:HEAD_PAIRS] * m_ref[0], x[HEAD_PAIRS:] * m_ref[1]], axis=0)


def _scan_solve(slot, mt_ref, mb_ref, y_refs, s_ref):
    c = SCAN_CHUNK
    lhs_ref, rk_ref, vv_ref, ke_ref, dt_ref = slot
    lhs = lhs_ref[...]
    rk = rk_ref[...]
    v = vv_ref[...]
    zero = jnp.zeros((), _BF)
    even = lax.broadcasted_iota(jnp.int32, (1, 1, LANES), 2) < HEAD_DIM
    kkt, rt = lhs[:, :c], lhs[:, c:]
    main = _bmm_nt(jnp.concatenate([jnp.where(even, kkt, zero), jnp.where(even, rt, zero),
                                    jnp.where(even, zero, rt)], axis=1), rk)
    top_odd = _bmm_nt(jnp.where(even, zero, kkt), jnp.concatenate([rk[:, c:], rk[:, :c]], axis=1))
    top = _by_direction(jnp.concatenate([main[:, :c], top_odd], axis=1), mt_ref)
    bot = _by_direction(main[:, c:], mb_ref)
    sbd = s_ref[...]
    ls = _bmm_nt(lhs, sbd)
    row_even = lax.broadcasted_iota(jnp.int32, (1, 2 * c, LANES), 1) < c
    is_x = row_even == even
    vv = jnp.concatenate([v, v], axis=1)
    x0 = jnp.concatenate([ls[:, :c], ls[:, :c]], axis=1) + _bmm(jnp.where(is_x, top, 0.0), vv)
    z = jnp.where(is_x, x0, top)
    for _ in range(int(math.log2(c))):
        swapped = jnp.concatenate([z[:, c:], z[:, :c]], axis=1)
        z = _bmm(jnp.where(is_x, 0.0, z), swapped) + jnp.where(is_x, z, 0.0)
    u = jnp.where(even, z[:, :c], z[:, c:])
    vu = jnp.concatenate([v, u.astype(_BF)], axis=1)
    yy = _bmm(bot, vu)
    y = ls[:, c:] + jnp.where(even, yy[:, :c], yy[:, c:])
    y_refs[0][...] = y[:HEAD_PAIRS]
    y_refs[1][...] = y[HEAD_PAIRS:]
    upd = _bmm_tn(vu, ke_ref[...])
    blk_row = lax.broadcasted_iota(jnp.int32, (1, LANES, LANES), 1) < HEAD_DIM
    blk_col = lax.broadcasted_iota(jnp.int32, (1, LANES, LANES), 2) < HEAD_DIM
    s_ref[...] = jnp.where(blk_row == blk_col, sbd * dt_ref[...] + upd, 0.0)


def _scan_chunk_index(d, s):
    return (s + LAT_CHUNKS) % N_CHUNKS if d == 0 else N_CHUNKS - 1 - s


def _wkv_scan(zr, wup, w0, aup, a0, k_k, k_a, m_incl, m_top, m_bot):
    batch = zr.shape[0]
    c = SCAN_CHUNK

    def prep_chunk(d, s):
        return _scan_chunk_index(d, jnp.minimum(s, N_CHUNKS - 1))

    def solve_chunk(d, s):
        return _scan_chunk_index(d, jnp.maximum(s - 1, 0))

    def tokens(d):
        def group(idx):
            return pl.BlockSpec((None, HEAD_PAIRS, c, LANES),
                                lambda b, s: (b, idx, prep_chunk(d, s), 0))

        def lora_tile(idx):
            return pl.BlockSpec((None, None, c, LANES), lambda b, s: (b, idx, prep_chunk(d, s), 0))

        return [group(0), group(1), group(2), lora_tile(TILE_W), lora_tile(TILE_A)]

    def whole(*shape):
        return pl.BlockSpec(shape, lambda b, s: (0,) * len(shape))

    def y_spec(d):
        return pl.BlockSpec((None, HEAD_PAIRS, c, LANES), lambda b, s: (b, 0, solve_chunk(d, s), 0))

    both = 2 * HEAD_PAIRS
    slot = [pltpu.VMEM((both, 2 * c, LANES), _BF), pltpu.VMEM((both, 2 * c, LANES), _BF),
            pltpu.VMEM((both, c, LANES), _BF), pltpu.VMEM((both, 2 * c, LANES), _BF),
            pltpu.VMEM((both, 1, LANES), _F32)]
    assert len(slot) == N_SLOT_REFS
    y_shape = jax.ShapeDtypeStruct((batch, HEAD_PAIRS, T_ALL, LANES), _F32)
    return pl.pallas_call(
        _scan_kernel,
        grid=(batch, N_CHUNKS + 1),
        in_specs=tokens(0) + tokens(1) + [
            whole(2, LANES, D_MODEL), whole(2, 1, D_MODEL),
            whole(2, LANES, D_MODEL), whole(2, 1, D_MODEL),
            whole(1, D_MODEL), whole(1, D_MODEL),
            whole(2, c, c), whole(2, 2 * c, 2 * c), whole(2, 2 * c, 2 * c),
        ],
        out_specs=[y_spec(0), y_spec(1)],
        out_shape=[y_shape, y_shape],
        scratch_shapes=[pltpu.VMEM((both, LANES, LANES), _F32)] + slot * 2,
        compiler_params=_params("arbitrary", "arbitrary"),
        name="wkv_scan",
    )(*([zr] * 10), wup, w0, aup, a0, k_k, k_a, m_incl, m_top, m_bot)


def _rwkv_post_kernel(yf_ref, yb_ref, r_ref, k_ref, v_ref, gz_ref, gup_ref, gng_ref, gnb_ref, rk_ref,
                      o_ref):
    gate = _mm(_sigmoid(gz_ref[...]), gup_ref[...])
    for hp in range(HEAD_PAIRS):
        cols = slice(hp * LANES, (hp + 1) * LANES)
        y = yf_ref[hp] + yb_ref[hp]
        mean = _head_sum(y) * (1.0 / HEAD_DIM)
        yc = y - mean
        var = _head_sum(yc * yc) * (1.0 / HEAD_DIM)
        yn = yc * lax.rsqrt(var + GN_EPS) * gng_ref[:, cols] + gnb_ref[:, cols]
        bonus = _head_sum(r_ref[hp] * k_ref[hp] * rk_ref[:, cols]) * v_ref[hp]
        o_ref[:, cols] = (yn + bonus) * gate[:, cols]


def _rwkv_post(y_fwd, y_bwd, zr, g_up, gn_g, gn_b, r_k):
    batch = zr.shape[0]

    def group(idx):
        return pl.BlockSpec((None, HEAD_PAIRS, ROW_TILE, LANES), lambda b, i: (b, idx, i, 0))

    vec = pl.BlockSpec((1, D_MODEL), lambda b, i: (0, 0))
    return pl.pallas_call(
        _rwkv_post_kernel,
        grid=(batch, N_ROW_TILES),
        in_specs=[
            group(0), group(0), group(0), group(1), group(2),
            pl.BlockSpec((None, None, ROW_TILE, LANES), lambda b, i: (b, TILE_G, i, 0)),
            pl.BlockSpec((GATE_LORA, D_MODEL), lambda b, i: (0, 0)),
            vec, vec, vec,
        ],
        out_specs=pl.BlockSpec((None, ROW_TILE, D_MODEL), lambda b, i: (b, i, 0)),
        out_shape=jax.ShapeDtypeStruct((batch, T_ALL, D_MODEL), _F32),
        compiler_params=_params("arbitrary", "arbitrary"),
        name="rwkv_post",
    )(y_fwd, y_bwd, zr, zr, zr, zr, g_up, gn_g, gn_b, r_k)


_H_SLOT = GRID_W + 16
_H_LEAD = 16
_V_PAD = CONV_HALF * GRID_W
_CTX_LEAD = 16


def _conv_taps(pad_ref, w, base, length, stride):
    acc = None
    for j in range(CONV_WIDTH):
        term = pad_ref[pl.ds(base + (j - CONV_HALF) * stride, length), :] * w[j:j + 1]
        acc = term if acc is None else acc + term
    return acc


def _conv_kernel(horizontal, zv_ref, zg_ref, w_ref, b_ref, o_ref, lat_ref, ctx_ref):
    w = w_ref[...]
    bias = b_ref[...]
    lat_ref[...] = jnp.zeros_like(lat_ref)
    ctx_ref[...] = jnp.zeros_like(ctx_ref)
    ctx_ref[pl.ds(_CTX_LEAD, CTX_LEN), :] = (
        zv_ref[pl.ds(SEQ, CTX_LEN), :] * _sigmoid(zg_ref[pl.ds(SEQ, CTX_LEN), :]))
    o_ref[pl.ds(SEQ, CTX_LEN), :] = _conv_taps(ctx_ref, w, _CTX_LEAD, CTX_LEN, 1) + bias
    if horizontal:
        for row in range(GRID_H):
            src = pl.ds(row * GRID_W, GRID_W)
            lat_ref[pl.ds(_H_LEAD + row * _H_SLOT, GRID_W), :] = (
                zv_ref[src, :] * _sigmoid(zg_ref[src, :]))
        for row in range(GRID_H):
            o_ref[pl.ds(row * GRID_W, GRID_W), :] = (
                _conv_taps(lat_ref, w, _H_LEAD + row * _H_SLOT, GRID_W, 1) + bias)
    else:
        lat_ref[pl.ds(_V_PAD, SEQ), :] = (
            zv_ref[pl.ds(0, SEQ), :] * _sigmoid(zg_ref[pl.ds(0, SEQ), :]))
        blk = 4 * GRID_W
        for i in range(SEQ // blk):
            o_ref[pl.ds(i * blk, blk), :] = _conv_taps(lat_ref, w, _V_PAD + i * blk, blk, GRID_W) + bias


def _conformer_conv(z, dw, dw_b, horizontal):
    batch = z.shape[0]
    first = Z_CONV // LANES
    n_tiles = CONV_DIM // LANES
    lat_rows = (_H_LEAD + GRID_H * _H_SLOT) if horizontal else (SEQ + 2 * _V_PAD)
    return pl.pallas_call(
        functools.partial(_conv_kernel, horizontal),
        grid=(batch, n_tiles),
        in_specs=[
            pl.BlockSpec((None, T_ALL, LANES), lambda b, j: (b, 0, first + j)),
            pl.BlockSpec((None, T_ALL, LANES), lambda b, j: (b, 0, first + n_tiles + j)),
            pl.BlockSpec((CONV_WIDTH, LANES), lambda b, j: (0, j)),
            pl.BlockSpec((1, LANES), lambda b, j: (0, j)),
        ],
        out_specs=pl.BlockSpec((None, T_ALL, LANES), lambda b, j: (b, 0, j)),
        out_shape=jax.ShapeDtypeStruct((batch, T_ALL, CONV_DIM), _F32),
        scratch_shapes=[pltpu.VMEM((lat_rows, LANES), _F32),
                        pltpu.VMEM((CTX_LEN + 2 * _CTX_LEAD, LANES), _F32)],
        compiler_params=_params("arbitrary", "arbitrary"),
        name="conformer_conv_h" if horizontal else "conformer_conv_v",
    )(z, z, dw, dw_b)


def _sgu_kernel(zu_ref, zv_ref, lng_ref, lnb_ref, ws_ref, bs_ref, o_ref):
    v = _standardize(_gelu_tanh(zv_ref[...]), LN_EPS) * lng_ref[...] + lnb_ref[...]
    group = lax.broadcasted_iota(jnp.int32, (SGU_CHUNK, SGU_DIM), 1) // SGU_GROUP_DIM
    for n in range(ROW_TILE // SGU_CHUNK):
        rows = slice(n * SGU_CHUNK, (n + 1) * SGU_CHUNK)
        vc = v[rows]
        mixed = bs_ref[...]
        for g in range(SGU_GROUPS):
            mixed = jnp.where(group == g, mixed + _mm(ws_ref[g], vc), mixed)
        o_ref[rows, :] = _gelu_tanh(zu_ref[rows, :]) * mixed


def _sgu(z, ln_g, ln_b, w_s, b_tile):
    batch = z.shape[0]
    first = Z_SGU // SGU_DIM
    vec = pl.BlockSpec((1, SGU_DIM), lambda b, i: (0, 0))
    return pl.pallas_call(
        _sgu_kernel,
        grid=(batch, N_ROW_TILES),
        in_specs=[
            pl.BlockSpec((None, ROW_TILE, SGU_DIM), lambda b, i: (b, i, first)),
            pl.BlockSpec((None, ROW_TILE, SGU_DIM), lambda b, i: (b, i, first + 1)),
            vec, vec,
            pl.BlockSpec((SGU_GROUPS, SGU_CHUNK, SGU_CHUNK), lambda b, i: (0, 0, 0)),
            pl.BlockSpec((SGU_CHUNK, SGU_DIM), lambda b, i: (0, 0)),
        ],
        out_specs=pl.BlockSpec((None, ROW_TILE, SGU_DIM), lambda b, i: (b, i, 0)),
        out_shape=jax.ShapeDtypeStruct((batch, T_ALL, SGU_DIM), _F32),
        compiler_params=_params("arbitrary", "arbitrary"),
        name="sgu",
    )(z, z, ln_g, ln_b, w_s, b_tile)


def _merge_kernel(x_ref, a_ref, cv_ref, c_ref, g0_ref, g1_ref, g2_ref, gate_ref, lng_ref, lnb_ref,
                  wro_ref, wco_ref, wso_ref, wm_ref, o_ref):
    a = _mm(a_ref[...], wro_ref[...])
    cb = _silu(_standardize(cv_ref[...], LN_EPS) * lng_ref[...] + lnb_ref[...])
    b = _mm(cb, wco_ref[...])
    c = _mm(c_ref[...], wso_ref[...])
    m = a * _sigmoid(g0_ref[...]) + b * _sigmoid(g1_ref[...]) + c * _sigmoid(g2_ref[...])
    o_ref[...] = x_ref[...] + gate_ref[...] * _mm(m, wm_ref[...])


def _merge(x_all, a_pre, conv, c_pre, z, gate1, ln_g, ln_b, w_ro, w_co, w_so, w_m):
    batch = x_all.shape[0]

    def rows(width, col=0):
        return pl.BlockSpec((None, ROW_TILE, width), lambda b, i: (b, i, col))

    def whole(shape):
        return pl.BlockSpec(shape, lambda b, i: (0, 0))

    return pl.pallas_call(
        _merge_kernel,
        grid=(batch, N_ROW_TILES),
        in_specs=[
            rows(D_MODEL), rows(D_MODEL), rows(CONV_DIM), rows(SGU_DIM),
            rows(D_MODEL, 0), rows(D_MODEL, 1), rows(D_MODEL, 2),
            _mod_spec(2), whole((1, CONV_DIM)), whole((1, CONV_DIM)),
            whole((D_MODEL, D_MODEL)), whole((CONV_DIM, D_MODEL)), whole((SGU_DIM, D_MODEL)),
            whole((D_MODEL, D_MODEL)),
        ],
        out_specs=rows(D_MODEL),
        out_shape=jax.ShapeDtypeStruct((batch, T_ALL, D_MODEL), _F32),
        compiler_params=_params("arbitrary", "arbitrary"),
        name="merge",
    )(x_all, a_pre, conv, c_pre, z, z, z, gate1, ln_g, ln_b, w_ro, w_co, w_so, w_m)


def _ffn_kernel(x_ref, g_ref, sh_ref, sc_ref, gate_ref, win_ref, wout_ref, o_ref):
    x = x_ref[...]
    h = _rms_mod(x, g_ref[...], sh_ref[...], sc_ref[...]).astype(_BF)
    acc = jnp.zeros((ROW_TILE, D_MODEL), _F32)
    for f in range(D_FF // FF_CHUNK):
        cols = slice(f * FF_CHUNK, (f + 1) * FF_CHUNK)
        up_cols = slice(D_FF + f * FF_CHUNK, D_FF + (f + 1) * FF_CHUNK)
        act = _silu(_mm(h, win_ref[:, cols])) * _mm(h, win_ref[:, up_cols])
        acc = acc + _mm(act, wout_ref[cols, :])
    o_ref[...] = x + gate_ref[...] * acc


def _ffn(x_all, g, shift, scale, gate2, w_in, w_out):
    batch = x_all.shape[0]
    rows = pl.BlockSpec((None, ROW_TILE, D_MODEL), lambda b, i: (b, i, 0))
    return pl.pallas_call(
        _ffn_kernel,
        grid=(batch, N_ROW_TILES),
        in_specs=[
            rows,
            pl.BlockSpec((1, D_MODEL), lambda b, i: (0, 0)),
            _mod_spec(2), _mod_spec(2), _mod_spec(2),
            pl.BlockSpec((D_MODEL, 2 * D_FF), lambda b, i: (0, 0)),
            pl.BlockSpec((D_FF, D_MODEL), lambda b, i: (0, 0)),
        ],
        out_specs=rows,
        out_shape=jax.ShapeDtypeStruct((batch, T_ALL, D_MODEL), _F32),
        compiler_params=_params("arbitrary", "arbitrary"),
        name="swiglu",
    )(x_all, g, shift, scale, gate2, w_in, w_out)


def _final_norm_kernel(x_ref, g_ref, o_ref):
    x = x_ref[...]
    o_ref[...] = x * lax.rsqrt(jnp.mean(x * x, -1, keepdims=True) + NORM_EPS) * g_ref[...]


def _final_norm(x_all, g):
    batch = x_all.shape[0]
    rows = pl.BlockSpec((None, ROW_TILE, D_MODEL), lambda b, i: (b, i, 0))
    return pl.pallas_call(
        _final_norm_kernel,
        grid=(batch, LAT_ROW_TILES),
        in_specs=[rows, pl.BlockSpec((1, D_MODEL), lambda b, i: (0, 0))],
        out_specs=rows,
        out_shape=jax.ShapeDtypeStruct((batch, SEQ, D_MODEL), _F32),
        compiler_params=_params("arbitrary", "arbitrary"),
        name="final_norm",
    )(x_all, g)


def _permute_w_in(w):
    off_conv = RWKV_COLS
    off_sgu = off_conv + 2 * CONV_DIM
    off_gate = off_sgu + 2 * SGU_DIM
    pad = jnp.zeros((D_MODEL, Z_CONV - Z_RWKV - RWKV_COLS), w.dtype)
    return jnp.concatenate([w[:, off_gate:], w[:, :off_conv], pad, w[:, off_conv:off_sgu],
                            w[:, off_sgu:off_gate]], axis=1)


def _direction_padded(w_up):
    zero = jnp.zeros_like(w_up[0])
    return jnp.stack([jnp.concatenate([w_up[0], zero], 0), jnp.concatenate([zero, w_up[1]], 0)])


def kernel(x, c, ctx, c_ctx, w_mod, b_mod, norm1_g, norm2_g, w_in, rwkv_shift, rwkv_w0, rwkv_w_up, rwkv_a0, rwkv_a_up, rwkv_g_up, rwkv_k_k, rwkv_k_a, rwkv_r_k, rwkv_gn_g, rwkv_gn_b, rwkv_out, conv_dw, conv_dw_b, conv_ln_g, conv_ln_b, conv_out, sgu_ln_g, sgu_ln_b, sgu_w, sgu_b, sgu_out, w_merge, ffn_w_in, ffn_w_out, final_norm_g):
    batch = x.shape[0]
    depth = w_mod.shape[0]
    assert x.shape[1:] == (SEQ, D_MODEL) and ctx.shape[1:] == (CTX_LEN, D_MODEL)

    rows = -(-(batch + 1) // 8) * 8
    cvec = jnp.zeros((rows, D_MODEL), _F32).at[:batch].set(c).at[batch].set(c_ctx)
    mod = _modulation(cvec, w_mod.astype(_BF), b_mod[:, None, :])
    mod_lat = mod[:, :batch].reshape(depth, batch, 6, D_MODEL)
    mod_ctx = jnp.broadcast_to(mod[:, batch].reshape(depth, 1, 6, D_MODEL), mod_lat.shape)
    mod_tab = jnp.stack([mod_lat, mod_ctx], axis=2).transpose(0, 3, 1, 2, 4)
    mod_tab = mod_tab.reshape(depth, 6, 2 * batch, 1, D_MODEL)

    idx = jnp.arange(SCAN_CHUNK)
    incl = jnp.stack([idx[:, None] >= idx[None, :], idx[:, None] <= idx[None, :]]).astype(_F32)
    strict = jnp.stack([idx[:, None] > idx[None, :], idx[:, None] < idx[None, :]]).astype(_F32)
    m_top = jnp.concatenate([jnp.concatenate([strict, -strict], 2),
                             jnp.concatenate([-strict, strict], 2)], 1)
    m_bot = jnp.concatenate([jnp.concatenate([incl, -incl], 2)] * 2, 1)
    m_incl = incl.astype(_BF)

    x_all = jnp.concatenate([x, ctx], axis=1)
    for l in range(depth):
        sh1, sc1, g1, sh2, sc2, g2 = (mod_tab[l, i] for i in range(6))
        z = _in_projection(x_all, norm1_g[l][None], sh1, sc1, _permute_w_in(w_in[l]).astype(_BF))
        zr = _token_shift(z, rwkv_shift[l])
        y_fwd, y_bwd = _wkv_scan(zr, _direction_padded(rwkv_w_up[l]).astype(_BF), rwkv_w0[l][:, None, :],
                      _direction_padded(rwkv_a_up[l]).astype(_BF), rwkv_a0[l][:, None, :],
                      rwkv_k_k[l][None], rwkv_k_a[l][None], m_incl, m_top, m_bot)
        a_pre = _rwkv_post(y_fwd, y_bwd, zr, rwkv_g_up[l].astype(_BF), rwkv_gn_g[l][None], rwkv_gn_b[l][None],
                           rwkv_r_k[l][None])
        conv = _conformer_conv(z, conv_dw[l], conv_dw_b[l][None], l % 2 == 0)
        b_tile = jnp.repeat(sgu_b[l].T, SGU_GROUP_DIM, axis=1)
        c_pre = _sgu(z, sgu_ln_g[l][None], sgu_ln_b[l][None], sgu_w[l].astype(_BF), b_tile)
        x_all = _merge(x_all, a_pre, conv, c_pre, z, g1, conv_ln_g[l][None], conv_ln_b[l][None],
                       rwkv_out[l].astype(_BF), conv_out[l].astype(_BF), sgu_out[l].astype(_BF),
                       w_merge[l].astype(_BF))
        x_all = _ffn(x_all, norm2_g[l][None], sh2, sc2, g2, ffn_w_in[l].astype(_BF),
                     ffn_w_out[l].astype(_BF))
    return _final_norm(x_all, final_norm_g[None])
```

```python
import functools
import itertools
import math

import jax
import jax.numpy as jnp
from jax import lax
from jax.experimental import pallas as pl
from jax.experimental.pallas import tpu as pltpu

D_MODEL = 1024
SEQ = 2048
CTX_LEN = 256
T_ALL = SEQ + CTX_LEN
GRID_W = 64
GRID_H = SEQ // GRID_W

HEAD_DIM = 64
HEADS = D_MODEL // HEAD_DIM
HEAD_PAIRS = HEADS // 2
DECAY_LORA = 64
ICLR_LORA = 64
GATE_LORA = 128
CONV_DIM = D_MODEL // 2
CONV_WIDTH = 31
CONV_HALF = CONV_WIDTH // 2
SGU_DIM = D_MODEL // 2
SGU_GROUPS = 8
SGU_GROUP_DIM = SGU_DIM // SGU_GROUPS
SGU_CHUNK = 128
D_FF = ((8 * D_MODEL // 3 + 255) // 256) * 256
NORM_EPS = 1e-6
LN_EPS = 1e-5
GN_EPS = 64e-5

RWKV_COLS = 3 * D_MODEL + 2 * DECAY_LORA + 2 * ICLR_LORA + GATE_LORA

LANES = 128
Z_GATE = 0
Z_RWKV = 3 * D_MODEL
Z_CONV = Z_RWKV + RWKV_COLS + 128
Z_SGU = Z_CONV + 2 * CONV_DIM
Z_COLS = Z_SGU + 2 * SGU_DIM
RWKV_TILES = RWKV_COLS // LANES
TILE_W = 3 * HEAD_PAIRS
TILE_A = TILE_W + 1
TILE_G = TILE_A + 1

ROW_TILE = 256
N_ROW_TILES = T_ALL // ROW_TILE
LAT_ROW_TILES = SEQ // ROW_TILE
SCAN_CHUNK = 64
N_CHUNKS = T_ALL // SCAN_CHUNK
LAT_CHUNKS = SEQ // SCAN_CHUNK
FF_CHUNK = 256

VMEM_LIMIT = 56 * 1024 * 1024

_BF = jnp.bfloat16
_F32 = jnp.float32


def _mm(a, b):
    return jnp.dot(a.astype(_BF), b.astype(_BF), preferred_element_type=_F32)


def _mm_f32(a, b):
    return jnp.dot(a, b, preferred_element_type=_F32, precision=lax.Precision.HIGHEST)


def _mm_nt(a, b):
    return lax.dot_general(a.astype(_BF), b.astype(_BF), (((1,), (1,)), ((), ())),
                           preferred_element_type=_F32)


def _mm_tn(a, b):
    return lax.dot_general(a.astype(_BF), b.astype(_BF), (((0,), (0,)), ((), ())),
                           preferred_element_type=_F32)


def _sigmoid(x):
    return jax.nn.sigmoid(x)


def _silu(x):
    return x * jax.nn.sigmoid(x)


def _gelu_tanh(x):
    return 0.5 * x * (1.0 + jnp.tanh(math.sqrt(2.0 / math.pi) * (x + 0.044715 * (x * x * x))))


def _standardize(x, eps):
    xc = x - jnp.mean(x, -1, keepdims=True)
    return xc * lax.rsqrt(jnp.mean(xc * xc, -1, keepdims=True) + eps)


def _rms_mod(x, g, shift, scale):
    y = x * lax.rsqrt(jnp.mean(x * x, -1, keepdims=True) + NORM_EPS) * g
    return y * (1.0 + scale) + shift


def _params(*sem):
    return pltpu.CompilerParams(dimension_semantics=sem, vmem_limit_bytes=VMEM_LIMIT)


def _mod_kernel(c_ref, w_ref, b_ref, o_ref):
    o_ref[...] = _mm(_silu(c_ref[...]), w_ref[...]) + b_ref[...]


def _modulation(cvec, w_mod, b_mod):
    depth = w_mod.shape[0]
    rows = cvec.shape[0]
    return pl.pallas_call(
        _mod_kernel,
        grid=(depth, 6),
        in_specs=[
            pl.BlockSpec((rows, D_MODEL), lambda l, j: (0, 0)),
            pl.BlockSpec((None, D_MODEL, D_MODEL), lambda l, j: (l, 0, j)),
            pl.BlockSpec((None, 1, D_MODEL), lambda l, j: (l, 0, j)),
        ],
        out_specs=pl.BlockSpec((None, rows, D_MODEL), lambda l, j: (l, 0, j)),
        out_shape=jax.ShapeDtypeStruct((depth, rows, 6 * D_MODEL), _F32),
        compiler_params=_params("arbitrary", "arbitrary"),
        name="modulation",
    )(cvec, w_mod, b_mod)


def _mod_spec(grid_rank):
    if grid_rank == 2:
        return pl.BlockSpec((None, 1, D_MODEL), lambda b, i: (2 * b + i // LAT_ROW_TILES, 0, 0))
    return pl.BlockSpec((None, 1, D_MODEL), lambda j, b, i: (2 * b + i // LAT_ROW_TILES, 0, 0))


def _inproj_kernel(x_ref, g_ref, sh_ref, sc_ref, w_ref, o_ref):
    h = _rms_mod(x_ref[...], g_ref[...], sh_ref[...], sc_ref[...])
    o_ref[...] = _mm(h, w_ref[...]).astype(o_ref.dtype)


def _in_projection(x_all, g, shift, scale, w):
    batch = x_all.shape[0]
    n_col = 2
    tn = Z_COLS // n_col
    return pl.pallas_call(
        _inproj_kernel,
        grid=(n_col, batch, N_ROW_TILES),
        in_specs=[
            pl.BlockSpec((None, ROW_TILE, D_MODEL), lambda j, b, i: (b, i, 0)),
            pl.BlockSpec((1, D_MODEL), lambda j, b, i: (0, 0)),
            _mod_spec(3),
            _mod_spec(3),
            pl.BlockSpec((D_MODEL, tn), lambda j, b, i: (0, j)),
        ],
        out_specs=pl.BlockSpec((None, ROW_TILE, tn), lambda j, b, i: (b, i, j)),
        out_shape=jax.ShapeDtypeStruct((batch, T_ALL, Z_COLS), _BF),
        compiler_params=_params("arbitrary", "arbitrary", "arbitrary"),
        name="in_projection",
    )(x_all, g, shift, scale, w)


def _shift_kernel(z_ref, w_ref, o_ref):
    z = z_ref[...].astype(_F32)
    t = lax.broadcasted_iota(jnp.int32, z.shape, 0)
    prev = pltpu.roll(z, 1, 0)
    nxt = pltpu.roll(z, T_ALL - 1, 0)
    prev = jnp.where((t == 0) | (t == SEQ), 0.0, prev)
    nxt = jnp.where((t == SEQ - 1) | (t == T_ALL - 1), 0.0, nxt)
    w = w_ref[...]
    o_ref[...] = prev * w[0:1] + z * w[1:2] + nxt * w[2:3]


def _token_shift(z, w_shift):
    batch = z.shape[0]
    first = Z_RWKV // LANES
    return pl.pallas_call(
        _shift_kernel,
        grid=(batch, RWKV_TILES),
        in_specs=[
            pl.BlockSpec((None, T_ALL, LANES), lambda b, j: (b, 0, first + j)),
            pl.BlockSpec((3, LANES), lambda b, j: (0, j)),
        ],
        out_specs=pl.BlockSpec((None, None, T_ALL, LANES), lambda b, j: (b, j, 0, 0)),
        out_shape=jax.ShapeDtypeStruct((batch, RWKV_TILES, T_ALL, LANES), _F32),
        compiler_params=_params("arbitrary", "arbitrary"),
        name="token_shift",
    )(z, w_shift)


def _head_sum(x):
    lane = lax.broadcasted_iota(jnp.int32, x.shape, x.ndim - 1)
    lo = jnp.sum(jnp.where(lane < HEAD_DIM, x, 0.0), -1, keepdims=True)
    tot = jnp.sum(x, -1, keepdims=True)
    return jnp.where(lane < HEAD_DIM, lo, tot - lo)


def _bmm(a, b):
    return lax.dot_general(a.astype(_BF), b.astype(_BF), (((2,), (1,)), ((0,), (0,))),
                           preferred_element_type=_F32)


def _bmm_nt(a, b):
    return lax.dot_general(a.astype(_BF), b.astype(_BF), (((2,), (2,)), ((0,), (0,))),
                           preferred_element_type=_F32)


def _bmm_tn(a, b):
    return lax.dot_general(a.astype(_BF), b.astype(_BF), (((1,), (1,)), ((0,), (0,))),
                           preferred_element_type=_F32)


def _lane_tiles(x):
    return jnp.stack([x[:, i * LANES:(i + 1) * LANES] for i in range(HEAD_PAIRS)])


N_SLOT_REFS = 5


def _scan_kernel(*refs):
    tokens = (refs[0:5], refs[5:10])
    wup_ref, w0_ref, aup_ref, a0_ref, kk_ref, ka_ref, mi_ref, mt_ref, mb_ref = refs[10:19]
    y_refs = refs[19:21]
    s_ref = refs[21]
    slots = (refs[22:22 + N_SLOT_REFS], refs[22 + N_SLOT_REFS:])
    step = pl.program_id(1)

    @pl.when(step == 0)
    def _():
        s_ref[...] = jnp.zeros_like(s_ref)
        for ref in slots[1]:
            ref[...] = jnp.zeros_like(ref)

    def run(prep, solve):
        pieces = itertools.chain.from_iterable(
            _scan_prepare(tokens[d], (wup_ref.at[d], w0_ref.at[d], aup_ref.at[d], a0_ref.at[d],
                                      kk_ref, ka_ref, mi_ref.at[d]), slots[prep], d)
            for d in range(2))
        next(pieces)
        for _ in _scan_solve(slots[solve], mt_ref, mb_ref, y_refs, s_ref):
            next(pieces, None)
        for _ in pieces:
            pass

    @pl.when(step % 2 == 0)
    def _():
        run(0, 1)

    @pl.when(step % 2 == 1)
    def _():
        run(1, 0)


PREP_PAIRS = 2
PREP_PIECES = HEAD_PAIRS // PREP_PAIRS


def _scan_prepare(token_refs, param_refs, slot, d):
    r_ref, k_ref, v_ref, lw_ref, la_ref = token_refs
    wup_ref, w0_ref, aup_ref, a0_ref, kk_ref, ka_ref, mi_ref = param_refs
    lhs_ref, rk_ref, vv_ref, ke_ref, dt_ref = slot
    width = PREP_PAIRS * LANES

    def tiles(x):
        return jnp.stack([x[:, i * LANES:(i + 1) * LANES] for i in range(PREP_PAIRS)])

    lw_act = jnp.tanh(lw_ref[...])
    la_in = la_ref[...]
    m_incl = mi_ref[...]
    for g in range(PREP_PIECES):
        cols = pl.ds(g * width, width)
        pairs = pl.ds(g * PREP_PAIRS, PREP_PAIRS)
        xw = w0_ref[:, cols] + _mm(lw_act, wup_ref[:, cols])
        ld = -math.exp(-0.5) * _sigmoid(xw)
        a = tiles(_sigmoid(a0_ref[:, cols] + _mm(la_in, aup_ref[:, cols])))
        hi = ld.astype(_BF)
        rem = ld - hi.astype(_F32)
        mid = rem.astype(_BF)
        lo = (rem - mid.astype(_F32)).astype(_BF)
        cl = (jnp.dot(m_incl, hi, preferred_element_type=_F32)
              + jnp.dot(m_incl, mid, preferred_element_type=_F32)
              + jnp.dot(m_incl, lo, preferred_element_type=_F32))
        ld_tot = jnp.sum(ld, 0, keepdims=True)
        e_in = tiles(jnp.exp(cl))
        e_out = tiles(jnp.exp(-cl))
        e_ex = tiles(jnp.exp(cl - ld))
        e_end = tiles(jnp.exp(ld_tot - cl))

        r = r_ref[pairs]
        k = k_ref[pairs]
        kk = k * tiles(kk_ref[:, cols])
        kk = kk * lax.rsqrt(jnp.maximum(_head_sum(kk * kk), 1e-24))
        k_d = k * (1.0 + (a - 1.0) * tiles(ka_ref[:, cols]))
        kka = kk * a

        rows = pl.ds(d * HEAD_PAIRS + g * PREP_PAIRS, PREP_PAIRS)
        lhs_ref[rows] = jnp.concatenate([kk * e_ex, r * e_in], axis=1).astype(_BF)
        rk_ref[rows] = jnp.swapaxes(jnp.concatenate([k_d * e_out, kka * e_out], axis=1),
                                    1, 2).astype(_BF)
        vv_ref[rows] = v_ref[pairs].astype(_BF)
        ke_ref[rows] = jnp.concatenate([k_d * e_end, -(kka * e_end)], axis=1).astype(_BF)
        dt_ref[rows] = tiles(jnp.exp(ld_tot))
        yield


def _by_direction(x, m_ref):
    return jnp.concatenate([x[:HEAD_PAIRS] * m_ref[0], x[HEAD_PAIRS:] * m_ref[1]], axis=0)


def _scan_solve(slot, mt_ref, mb_ref, y_refs, s_ref):
    c = SCAN_CHUNK
    lhs_ref, rk_ref, vv_ref, ke_ref, dt_ref = slot
    lhs = lhs_ref[...]
    rk = rk_ref[...]
    v = vv_ref[...]
    zero = jnp.zeros((), _BF)
    even = lax.broadcasted_iota(jnp.int32, (1, 1, LANES), 2) < HEAD_DIM
    kkt, rt = lhs[:, :c], lhs[:, c:]
    main = _bmm(jnp.concatenate([jnp.where(even, kkt, zero), jnp.where(even, rt, zero),
                                 jnp.where(even, zero, rt), jnp.where(even, zero, kkt)], axis=1), rk)
    yield
    sbd = s_ref[...]
    ls = _bmm(lhs, jnp.swapaxes(sbd, 1, 2))
    yield
    top_odd = pltpu.roll(main[:, 3 * c:], HEAD_DIM, 2)
    main = main[:, :3 * c].astype(_BF)
    top = _by_direction(jnp.concatenate([main[:, :c], top_odd.astype(_BF)], axis=1), mt_ref)
    bot = _by_direction(main[:, c:], mb_ref)
    row_even = lax.broadcasted_iota(jnp.int32, (1, 2 * c, LANES), 1) < c
    is_x = row_even == even
    vv = jnp.concatenate([v, v], axis=1)
    out = _bmm(jnp.where(is_x, top, zero), vv)
    yield
    x = jnp.concatenate([ls[:, :c], ls[:, :c]], axis=1) + out
    zb = jnp.where(is_x, x.astype(_BF), top)
    n_steps = int(math.log2(c))
    for i in range(n_steps):
        out = _bmm(jnp.where(is_x, zero, zb), jnp.concatenate([zb[:, c:], zb[:, :c]], axis=1))
        yield
        x = x + out
        if i + 1 < n_steps:
            zb = jnp.where(is_x, x.astype(_BF), out.astype(_BF))
    u = jnp.where(even, x[:, :c], x[:, c:])
    vu = jnp.concatenate([v, u.astype(_BF)], axis=1)
    yy = _bmm(bot, vu)
    upd = _bmm_tn(vu, ke_ref[...])
    yield
    y = ls[:, c:] + jnp.where(even, yy[:, :c], yy[:, c:])
    y_refs[0][...] = y[:HEAD_PAIRS]
    y_refs[1][...] = y[HEAD_PAIRS:]
    blk_row = lax.broadcasted_iota(jnp.int32, (1, LANES, LANES), 1) < HEAD_DIM
    blk_col = lax.broadcasted_iota(jnp.int32, (1, LANES, LANES), 2) < HEAD_DIM
    s_ref[...] = jnp.where(blk_row == blk_col, sbd * dt_ref[...] + upd, 0.0)


def _scan_chunk_index(d, s):
    return (s + LAT_CHUNKS) % N_CHUNKS if d == 0 else N_CHUNKS - 1 - s


def _wkv_scan(zr, wup, w0, aup, a0, k_k, k_a, m_incl, m_top, m_bot):
    batch = zr.shape[0]
    c = SCAN_CHUNK

    def prep_chunk(d, s):
        return _scan_chunk_index(d, jnp.minimum(s, N_CHUNKS - 1))

    def solve_chunk(d, s):
        return _scan_chunk_index(d, jnp.maximum(s - 1, 0))

    def tokens(d):
        def group(idx):
            return pl.BlockSpec((None, HEAD_PAIRS, c, LANES),
                                lambda b, s: (b, idx, prep_chunk(d, s), 0))

        def lora_tile(idx):
            return pl.BlockSpec((None, None, c, LANES), lambda b, s: (b, idx, prep_chunk(d, s), 0))

        return [group(0), group(1), group(2), lora_tile(TILE_W), lora_tile(TILE_A)]

    def whole(*shape):
        return pl.BlockSpec(shape, lambda b, s: (0,) * len(shape))

    def y_spec(d):
        return pl.BlockSpec((None, HEAD_PAIRS, c, LANES), lambda b, s: (b, 0, solve_chunk(d, s), 0))

    both = 2 * HEAD_PAIRS
    slot = [pltpu.VMEM((both, 2 * c, LANES), _BF), pltpu.VMEM((both, 2 * c, LANES), _BF),
            pltpu.VMEM((both, c, LANES), _BF), pltpu.VMEM((both, 2 * c, LANES), _BF),
            pltpu.VMEM((both, 1, LANES), _F32)]
    assert len(slot) == N_SLOT_REFS
    y_shape = jax.ShapeDtypeStruct((batch, HEAD_PAIRS, T_ALL, LANES), _F32)
    return pl.pallas_call(
        _scan_kernel,
        grid=(batch, N_CHUNKS + 1),
        in_specs=tokens(0) + tokens(1) + [
            whole(2, LANES, D_MODEL), whole(2, 1, D_MODEL),
            whole(2, LANES, D_MODEL), whole(2, 1, D_MODEL),
            whole(1, D_MODEL), whole(1, D_MODEL),
            whole(2, c, c), whole(2, 2 * c, 2 * c), whole(2, 2 * c, 2 * c),
        ],
        out_specs=[y_spec(0), y_spec(1)],
        out_shape=[y_shape, y_shape],
        scratch_shapes=[pltpu.VMEM((both, LANES, LANES), _F32)] + slot * 2,
        compiler_params=_params("arbitrary", "arbitrary"),
        name="wkv_scan",
    )(*([zr] * 10), wup, w0, aup, a0, k_k, k_a, m_incl, m_top, m_bot)


def _rwkv_post_kernel(yf_ref, yb_ref, r_ref, k_ref, v_ref, gz_ref, gup_ref, gng_ref, gnb_ref, rk_ref,
                      o_ref):
    gate = _mm(_sigmoid(gz_ref[...]), gup_ref[...])
    for hp in range(HEAD_PAIRS):
        cols = slice(hp * LANES, (hp + 1) * LANES)
        y = yf_ref[hp] + yb_ref[hp]
        mean = _head_sum(y) * (1.0 / HEAD_DIM)
        yc = y - mean
        var = _head_sum(yc * yc) * (1.0 / HEAD_DIM)
        yn = yc * lax.rsqrt(var + GN_EPS) * gng_ref[:, cols] + gnb_ref[:, cols]
        bonus = _head_sum(r_ref[hp] * k_ref[hp] * rk_ref[:, cols]) * v_ref[hp]
        o_ref[:, cols] = ((yn + bonus) * gate[:, cols]).astype(o_ref.dtype)


def _rwkv_post(y_fwd, y_bwd, zr, g_up, gn_g, gn_b, r_k):
    batch = zr.shape[0]

    def group(idx):
        return pl.BlockSpec((None, HEAD_PAIRS, ROW_TILE, LANES), lambda b, i: (b, idx, i, 0))

    vec = pl.BlockSpec((1, D_MODEL), lambda b, i: (0, 0))
    return pl.pallas_call(
        _rwkv_post_kernel,
        grid=(batch, N_ROW_TILES),
        in_specs=[
            group(0), group(0), group(0), group(1), group(2),
            pl.BlockSpec((None, None, ROW_TILE, LANES), lambda b, i: (b, TILE_G, i, 0)),
            pl.BlockSpec((GATE_LORA, D_MODEL), lambda b, i: (0, 0)),
            vec, vec, vec,
        ],
        out_specs=pl.BlockSpec((None, ROW_TILE, D_MODEL), lambda b, i: (b, i, 0)),
        out_shape=jax.ShapeDtypeStruct((batch, T_ALL, D_MODEL), _BF),
        compiler_params=_params("arbitrary", "arbitrary"),
        name="rwkv_post",
    )(y_fwd, y_bwd, zr, zr, zr, zr, g_up, gn_g, gn_b, r_k)


_H_SLOT = GRID_W + 16
_H_LEAD = 16
_V_PAD = CONV_HALF * GRID_W
_CTX_LEAD = 16


def _conv_taps(pad_ref, w, base, length, stride):
    acc = None
    for j in range(CONV_WIDTH):
        term = pad_ref[pl.ds(base + (j - CONV_HALF) * stride, length), :] * w[j:j + 1]
        acc = term if acc is None else acc + term
    return acc


def _glu(zv_ref, zg_ref, rows):
    return zv_ref[rows, :].astype(_F32) * _sigmoid(zg_ref[rows, :].astype(_F32))


def _conv_kernel(horizontal, zv_ref, zg_ref, w_ref, b_ref, o_ref, lat_ref, ctx_ref):
    w = w_ref[...]
    bias = b_ref[...]
    lat_ref[...] = jnp.zeros_like(lat_ref)
    ctx_ref[...] = jnp.zeros_like(ctx_ref)
    ctx_ref[pl.ds(_CTX_LEAD, CTX_LEN), :] = (
        _glu(zv_ref, zg_ref, pl.ds(SEQ, CTX_LEN)))
    o_ref[pl.ds(SEQ, CTX_LEN), :] = _conv_taps(ctx_ref, w, _CTX_LEAD, CTX_LEN, 1) + bias
    if horizontal:
        for row in range(GRID_H):
            src = pl.ds(row * GRID_W, GRID_W)
            lat_ref[pl.ds(_H_LEAD + row * _H_SLOT, GRID_W), :] = (
                _glu(zv_ref, zg_ref, src))
        for row in range(GRID_H):
            o_ref[pl.ds(row * GRID_W, GRID_W), :] = (
                _conv_taps(lat_ref, w, _H_LEAD + row * _H_SLOT, GRID_W, 1) + bias)
    else:
        lat_ref[pl.ds(_V_PAD, SEQ), :] = (
            _glu(zv_ref, zg_ref, pl.ds(0, SEQ)))
        blk = 4 * GRID_W
        for i in range(SEQ // blk):
            o_ref[pl.ds(i * blk, blk), :] = _conv_taps(lat_ref, w, _V_PAD + i * blk, blk, GRID_W) + bias


def _conformer_conv(z, dw, dw_b, horizontal):
    batch = z.shape[0]
    first = Z_CONV // LANES
    n_tiles = CONV_DIM // LANES
    lat_rows = (_H_LEAD + GRID_H * _H_SLOT) if horizontal else (SEQ + 2 * _V_PAD)
    return pl.pallas_call(
        functools.partial(_conv_kernel, horizontal),
        grid=(batch, n_tiles),
        in_specs=[
            pl.BlockSpec((None, T_ALL, LANES), lambda b, j: (b, 0, first + j)),
            pl.BlockSpec((None, T_ALL, LANES), lambda b, j: (b, 0, first + n_tiles + j)),
            pl.BlockSpec((CONV_WIDTH, LANES), lambda b, j: (0, j)),
            pl.BlockSpec((1, LANES), lambda b, j: (0, j)),
        ],
        out_specs=pl.BlockSpec((None, T_ALL, LANES), lambda b, j: (b, 0, j)),
        out_shape=jax.ShapeDtypeStruct((batch, T_ALL, CONV_DIM), _F32),
        scratch_shapes=[pltpu.VMEM((lat_rows, LANES), _F32),
                        pltpu.VMEM((CTX_LEN + 2 * _CTX_LEAD, LANES), _F32)],
        compiler_params=_params("arbitrary", "arbitrary"),
        name="conformer_conv_h" if horizontal else "conformer_conv_v",
    )(z, z, dw, dw_b)


def _sgu_kernel(zu_ref, zv_ref, lng_ref, lnb_ref, ws_ref, bs_ref, o_ref):
    v = _standardize(_gelu_tanh(zv_ref[...].astype(_F32)), LN_EPS) * lng_ref[...] + lnb_ref[...]
    group = lax.broadcasted_iota(jnp.int32, (SGU_CHUNK, SGU_DIM), 1) // SGU_GROUP_DIM
    for n in range(ROW_TILE // SGU_CHUNK):
        rows = slice(n * SGU_CHUNK, (n + 1) * SGU_CHUNK)
        vc = v[rows]
        mixed = bs_ref[...]
        for g in range(SGU_GROUPS):
            mixed = jnp.where(group == g, mixed + _mm(ws_ref[g], vc), mixed)
        o_ref[rows, :] = (_gelu_tanh(zu_ref[rows, :].astype(_F32)) * mixed).astype(o_ref.dtype)


def _sgu(z, ln_g, ln_b, w_s, b_tile):
    batch = z.shape[0]
    first = Z_SGU // SGU_DIM
    vec = pl.BlockSpec((1, SGU_DIM), lambda b, i: (0, 0))
    return pl.pallas_call(
        _sgu_kernel,
        grid=(batch, N_ROW_TILES),
        in_specs=[
            pl.BlockSpec((None, ROW_TILE, SGU_DIM), lambda b, i: (b, i, first)),
            pl.BlockSpec((None, ROW_TILE, SGU_DIM), lambda b, i: (b, i, first + 1)),
            vec, vec,
            pl.BlockSpec((SGU_GROUPS, SGU_CHUNK, SGU_CHUNK), lambda b, i: (0, 0, 0)),
            pl.BlockSpec((SGU_CHUNK, SGU_DIM), lambda b, i: (0, 0)),
        ],
        out_specs=pl.BlockSpec((None, ROW_TILE, SGU_DIM), lambda b, i: (b, i, 0)),
        out_shape=jax.ShapeDtypeStruct((batch, T_ALL, SGU_DIM), _BF),
        compiler_params=_params("arbitrary", "arbitrary"),
        name="sgu",
    )(z, z, ln_g, ln_b, w_s, b_tile)


def _merge_kernel(x_ref, a_ref, cv_ref, c_ref, g0_ref, g1_ref, g2_ref, gate_ref, lng_ref, lnb_ref,
                  wro_ref, wco_ref, wso_ref, wm_ref, o_ref):
    a = _mm(a_ref[...], wro_ref[...])
    cb = _silu(_standardize(cv_ref[...], LN_EPS) * lng_ref[...] + lnb_ref[...])
    b = _mm(cb, wco_ref[...])
    c = _mm(c_ref[...], wso_ref[...])
    m = (a * _sigmoid(g0_ref[...].astype(_F32)) + b * _sigmoid(g1_ref[...].astype(_F32))
         + c * _sigmoid(g2_ref[...].astype(_F32)))
    o_ref[...] = x_ref[...] + gate_ref[...] * _mm(m, wm_ref[...])


def _merge(x_all, a_pre, conv, c_pre, z, gate1, ln_g, ln_b, w_ro, w_co, w_so, w_m):
    batch = x_all.shape[0]

    def rows(width, col=0):
        return pl.BlockSpec((None, ROW_TILE, width), lambda b, i: (b, i, col))

    def whole(shape):
        return pl.BlockSpec(shape, lambda b, i: (0, 0))

    return pl.pallas_call(
        _merge_kernel,
        grid=(batch, N_ROW_TILES),
        in_specs=[
            rows(D_MODEL), rows(D_MODEL), rows(CONV_DIM), rows(SGU_DIM),
            rows(D_MODEL, 0), rows(D_MODEL, 1), rows(D_MODEL, 2),
            _mod_spec(2), whole((1, CONV_DIM)), whole((1, CONV_DIM)),
            whole((D_MODEL, D_MODEL)), whole((CONV_DIM, D_MODEL)), whole((SGU_DIM, D_MODEL)),
            whole((D_MODEL, D_MODEL)),
        ],
        out_specs=rows(D_MODEL),
        out_shape=jax.ShapeDtypeStruct((batch, T_ALL, D_MODEL), _F32),
        compiler_params=_params("arbitrary", "arbitrary"),
        name="merge",
    )(x_all, a_pre, conv, c_pre, z, z, z, gate1, ln_g, ln_b, w_ro, w_co, w_so, w_m)


def _ffn_kernel(x_ref, g_ref, sh_ref, sc_ref, gate_ref, win_ref, wout_ref, o_ref):
    x = x_ref[...]
    h = _rms_mod(x, g_ref[...], sh_ref[...], sc_ref[...]).astype(_BF)
    acc = jnp.zeros((ROW_TILE, D_MODEL), _F32)
    for f in range(D_FF // FF_CHUNK):
        cols = slice(f * FF_CHUNK, (f + 1) * FF_CHUNK)
        up_cols = slice(D_FF + f * FF_CHUNK, D_FF + (f + 1) * FF_CHUNK)
        act = _silu(_mm(h, win_ref[:, cols])) * _mm(h, win_ref[:, up_cols])
        acc = acc + _mm(act, wout_ref[cols, :])
    o_ref[...] = x + gate_ref[...] * acc


def _ffn(x_all, g, shift, scale, gate2, w_in, w_out):
    batch = x_all.shape[0]
    rows = pl.BlockSpec((None, ROW_TILE, D_MODEL), lambda b, i: (b, i, 0))
    return pl.pallas_call(
        _ffn_kernel,
        grid=(batch, N_ROW_TILES),
        in_specs=[
            rows,
            pl.BlockSpec((1, D_MODEL), lambda b, i: (0, 0)),
            _mod_spec(2), _mod_spec(2), _mod_spec(2),
            pl.BlockSpec((D_MODEL, 2 * D_FF), lambda b, i: (0, 0)),
            pl.BlockSpec((D_FF, D_MODEL), lambda b, i: (0, 0)),
        ],
        out_specs=rows,
        out_shape=jax.ShapeDtypeStruct((batch, T_ALL, D_MODEL), _F32),
        compiler_params=_params("arbitrary", "arbitrary"),
        name="swiglu",
    )(x_all, g, shift, scale, gate2, w_in, w_out)


def _final_norm_kernel(x_ref, g_ref, o_ref):
    x = x_ref[...]
    o_ref[...] = x * lax.rsqrt(jnp.mean(x * x, -1, keepdims=True) + NORM_EPS) * g_ref[...]


def _final_norm(x_all, g):
    batch = x_all.shape[0]
    rows = pl.BlockSpec((None, ROW_TILE, D_MODEL), lambda b, i: (b, i, 0))
    return pl.pallas_call(
        _final_norm_kernel,
        grid=(batch, LAT_ROW_TILES),
        in_specs=[rows, pl.BlockSpec((1, D_MODEL), lambda b, i: (0, 0))],
        out_specs=rows,
        out_shape=jax.ShapeDtypeStruct((batch, SEQ, D_MODEL), _F32),
        compiler_params=_params("arbitrary", "arbitrary"),
        name="final_norm",
    )(x_all, g)


def _permute_w_in(w):
    off_conv = RWKV_COLS
    off_sgu = off_conv + 2 * CONV_DIM
    off_gate = off_sgu + 2 * SGU_DIM
    pad = jnp.zeros((D_MODEL, Z_CONV - Z_RWKV - RWKV_COLS), w.dtype)
    return jnp.concatenate([w[:, off_gate:], w[:, :off_conv], pad, w[:, off_conv:off_sgu],
                            w[:, off_sgu:off_gate]], axis=1)


def _direction_padded(w_up):
    zero = jnp.zeros_like(w_up[0])
    return jnp.stack([jnp.concatenate([w_up[0], zero], 0), jnp.concatenate([zero, w_up[1]], 0)])


def kernel(x, c, ctx, c_ctx, w_mod, b_mod, norm1_g, norm2_g, w_in, rwkv_shift, rwkv_w0, rwkv_w_up, rwkv_a0, rwkv_a_up, rwkv_g_up, rwkv_k_k, rwkv_k_a, rwkv_r_k, rwkv_gn_g, rwkv_gn_b, rwkv_out, conv_dw, conv_dw_b, conv_ln_g, conv_ln_b, conv_out, sgu_ln_g, sgu_ln_b, sgu_w, sgu_b, sgu_out, w_merge, ffn_w_in, ffn_w_out, final_norm_g):
    batch = x.shape[0]
    depth = w_mod.shape[0]
    assert x.shape[1:] == (SEQ, D_MODEL) and ctx.shape[1:] == (CTX_LEN, D_MODEL)

    rows = -(-(batch + 1) // 8) * 8
    cvec = jnp.zeros((rows, D_MODEL), _F32).at[:batch].set(c).at[batch].set(c_ctx)
    mod = _modulation(cvec, w_mod.astype(_BF), b_mod[:, None, :])
    mod_lat = mod[:, :batch].reshape(depth, batch, 6, D_MODEL)
    mod_ctx = jnp.broadcast_to(mod[:, batch].reshape(depth, 1, 6, D_MODEL), mod_lat.shape)
    mod_tab = jnp.stack([mod_lat, mod_ctx], axis=2).transpose(0, 3, 1, 2, 4)
    mod_tab = mod_tab.reshape(depth, 6, 2 * batch, 1, D_MODEL)

    idx = jnp.arange(SCAN_CHUNK)
    incl = jnp.stack([idx[:, None] >= idx[None, :], idx[:, None] <= idx[None, :]]).astype(_F32)
    strict = jnp.stack([idx[:, None] > idx[None, :], idx[:, None] < idx[None, :]]).astype(_F32)
    m_top = jnp.concatenate([jnp.concatenate([strict, -strict], 2),
                             jnp.concatenate([-strict, strict], 2)], 1)
    m_bot = jnp.concatenate([jnp.concatenate([incl, -incl], 2)] * 2, 1).astype(_BF)
    m_top = m_top.astype(_BF)
    m_incl = incl.astype(_BF)

    x_all = jnp.concatenate([x, ctx], axis=1)
    for l in range(depth):
        sh1, sc1, g1, sh2, sc2, g2 = (mod_tab[l, i] for i in range(6))
        z = _in_projection(x_all, norm1_g[l][None], sh1, sc1, _permute_w_in(w_in[l]).astype(_BF))
        zr = _token_shift(z, rwkv_shift[l])
        y_fwd, y_bwd = _wkv_scan(zr, _direction_padded(rwkv_w_up[l]).astype(_BF), rwkv_w0[l][:, None, :],
                      _direction_padded(rwkv_a_up[l]).astype(_BF), rwkv_a0[l][:, None, :],
                      rwkv_k_k[l][None], rwkv_k_a[l][None], m_incl, m_top, m_bot)
        a_pre = _rwkv_post(y_fwd, y_bwd, zr, rwkv_g_up[l].astype(_BF), rwkv_gn_g[l][None], rwkv_gn_b[l][None],
                           rwkv_r_k[l][None])
        conv = _conformer_conv(z, conv_dw[l], conv_dw_b[l][None], l % 2 == 0)
        b_tile = jnp.repeat(sgu_b[l].T, SGU_GROUP_DIM, axis=1)
        c_pre = _sgu(z, sgu_ln_g[l][None], sgu_ln_b[l][None], sgu_w[l].astype(_BF), b_tile)
        x_all = _merge(x_all, a_pre, conv, c_pre, z, g1, conv_ln_g[l][None], conv_ln_b[l][None],
                       rwkv_out[l].astype(_BF), conv_out[l].astype(_BF), sgu_out[l].astype(_BF),
                       w_merge[l].astype(_BF))
        x_all = _ffn(x_all, norm2_g[l][None], sh2, sc2, g2, ffn_w_in[l].astype(_BF),
                     ffn_w_out[l].astype(_BF))
    return _final_norm(x_all, final_norm_g[None])
```

```python
import functools
import itertools
import math

import jax
import jax.numpy as jnp
from jax import lax
from jax.experimental import pallas as pl
from jax.experimental.pallas import tpu as pltpu

D_MODEL = 1024
SEQ = 2048
CTX_LEN = 256
T_ALL = SEQ + CTX_LEN
GRID_W = 64
GRID_H = SEQ // GRID_W

HEAD_DIM = 64
HEADS = D_MODEL // HEAD_DIM
HEAD_PAIRS = HEADS // 2
DECAY_LORA = 64
ICLR_LORA = 64
GATE_LORA = 128
CONV_DIM = D_MODEL // 2
CONV_WIDTH = 31
CONV_HALF = CONV_WIDTH // 2
SGU_DIM = D_MODEL // 2
SGU_GROUPS = 8
SGU_GROUP_DIM = SGU_DIM // SGU_GROUPS
SGU_CHUNK = 128
D_FF = ((8 * D_MODEL // 3 + 255) // 256) * 256
NORM_EPS = 1e-6
LN_EPS = 1e-5
GN_EPS = 64e-5

RWKV_COLS = 3 * D_MODEL + 2 * DECAY_LORA + 2 * ICLR_LORA + GATE_LORA

LANES = 128
Z_GATE = 0
Z_RWKV = 3 * D_MODEL
Z_CONV = Z_RWKV + RWKV_COLS + 128
Z_SGU = Z_CONV + 2 * CONV_DIM
Z_COLS = Z_SGU + 2 * SGU_DIM
RWKV_TILES = RWKV_COLS // LANES
TILE_W = 3 * HEAD_PAIRS
TILE_A = TILE_W + 1
TILE_G = TILE_A + 1

ROW_TILE = 256
N_ROW_TILES = T_ALL // ROW_TILE
LAT_ROW_TILES = SEQ // ROW_TILE
SCAN_CHUNK = 64
N_CHUNKS = T_ALL // SCAN_CHUNK
LAT_CHUNKS = SEQ // SCAN_CHUNK
FF_CHUNK = 256

VMEM_LIMIT = 56 * 1024 * 1024

_BF = jnp.bfloat16
_F32 = jnp.float32


def _mm(a, b):
    return jnp.dot(a.astype(_BF), b.astype(_BF), preferred_element_type=_F32)


def _mm_f32(a, b):
    return jnp.dot(a, b, preferred_element_type=_F32, precision=lax.Precision.HIGHEST)


def _mm_nt(a, b):
    return lax.dot_general(a.astype(_BF), b.astype(_BF), (((1,), (1,)), ((), ())),
                           preferred_element_type=_F32)


def _mm_tn(a, b):
    return lax.dot_general(a.astype(_BF), b.astype(_BF), (((0,), (0,)), ((), ())),
                           preferred_element_type=_F32)


def _sigmoid(x):
    return jax.nn.sigmoid(x)


def _silu(x):
    return x * jax.nn.sigmoid(x)


def _gelu_tanh(x):
    return 0.5 * x * (1.0 + jnp.tanh(math.sqrt(2.0 / math.pi) * (x + 0.044715 * (x * x * x))))


def _standardize(x, eps):
    xc = x - jnp.mean(x, -1, keepdims=True)
    return xc * lax.rsqrt(jnp.mean(xc * xc, -1, keepdims=True) + eps)


def _rms_mod(x, g, shift, scale):
    y = x * lax.rsqrt(jnp.mean(x * x, -1, keepdims=True) + NORM_EPS) * g
    return y * (1.0 + scale) + shift


def _params(*sem):
    return pltpu.CompilerParams(dimension_semantics=sem, vmem_limit_bytes=VMEM_LIMIT)


def _mod_kernel(c_ref, w_ref, b_ref, o_ref):
    o_ref[...] = _mm(_silu(c_ref[...]), w_ref[...]) + b_ref[...]


def _modulation(cvec, w_mod, b_mod):
    depth = w_mod.shape[0]
    rows = cvec.shape[0]
    return pl.pallas_call(
        _mod_kernel,
        grid=(depth, 6),
        in_specs=[
            pl.BlockSpec((rows, D_MODEL), lambda l, j: (0, 0)),
            pl.BlockSpec((None, D_MODEL, D_MODEL), lambda l, j: (l, 0, j)),
            pl.BlockSpec((None, 1, D_MODEL), lambda l, j: (l, 0, j)),
        ],
        out_specs=pl.BlockSpec((None, rows, D_MODEL), lambda l, j: (l, 0, j)),
        out_shape=jax.ShapeDtypeStruct((depth, rows, 6 * D_MODEL), _F32),
        compiler_params=_params("arbitrary", "arbitrary"),
        name="modulation",
    )(cvec, w_mod, b_mod)


def _mod_spec(grid_rank):
    if grid_rank == 2:
        return pl.BlockSpec((None, 1, D_MODEL), lambda b, i: (2 * b + i // LAT_ROW_TILES, 0, 0))
    return pl.BlockSpec((None, 1, D_MODEL), lambda j, b, i: (2 * b + i // LAT_ROW_TILES, 0, 0))


def _inproj_kernel(x_ref, g_ref, sh_ref, sc_ref, w_ref, o_ref):
    h = _rms_mod(x_ref[...], g_ref[...], sh_ref[...], sc_ref[...])
    o_ref[...] = _mm(h, w_ref[...]).astype(o_ref.dtype)


def _in_projection(x_all, g, shift, scale, w):
    batch = x_all.shape[0]
    n_col = 2
    tn = Z_COLS // n_col
    return pl.pallas_call(
        _inproj_kernel,
        grid=(n_col, batch, N_ROW_TILES),
        in_specs=[
            pl.BlockSpec((None, ROW_TILE, D_MODEL), lambda j, b, i: (b, i, 0)),
            pl.BlockSpec((1, D_MODEL), lambda j, b, i: (0, 0)),
            _mod_spec(3),
            _mod_spec(3),
            pl.BlockSpec((D_MODEL, tn), lambda j, b, i: (0, j)),
        ],
        out_specs=pl.BlockSpec((None, ROW_TILE, tn), lambda j, b, i: (b, i, j)),
        out_shape=jax.ShapeDtypeStruct((batch, T_ALL, Z_COLS), _BF),
        compiler_params=_params("arbitrary", "arbitrary", "arbitrary"),
        name="in_projection",
    )(x_all, g, shift, scale, w)


def _shift_kernel(z_ref, w_ref, o_ref):
    z = z_ref[...].astype(_F32)
    prev = pltpu.roll(z, 1, 0)
    nxt = pltpu.roll(z, T_ALL - 1, 0)
    w = w_ref[...]
    o_ref[...] = prev * w[0:1] + z * w[1:2] + nxt * w[2:3]
    sub = 8
    for g0 in (0, SEQ - sub, SEQ, T_ALL - sub):
        t = g0 + lax.broadcasted_iota(jnp.int32, (sub, LANES), 0)
        rows = slice(g0, g0 + sub)
        p = jnp.where((t == 0) | (t == SEQ), 0.0, prev[rows])
        n = jnp.where((t == SEQ - 1) | (t == T_ALL - 1), 0.0, nxt[rows])
        o_ref[rows, :] = p * w[0:1] + z[rows] * w[1:2] + n * w[2:3]


def _token_shift(z, w_shift):
    batch = z.shape[0]
    first = Z_RWKV // LANES
    return pl.pallas_call(
        _shift_kernel,
        grid=(batch, RWKV_TILES),
        in_specs=[
            pl.BlockSpec((None, T_ALL, LANES), lambda b, j: (b, 0, first + j)),
            pl.BlockSpec((3, LANES), lambda b, j: (0, j)),
        ],
        out_specs=pl.BlockSpec((None, None, T_ALL, LANES), lambda b, j: (b, j, 0, 0)),
        out_shape=jax.ShapeDtypeStruct((batch, RWKV_TILES, T_ALL, LANES), _F32),
        compiler_params=_params("arbitrary", "arbitrary"),
        name="token_shift",
    )(z, w_shift)


def _head_sum(x):
    lane = lax.broadcasted_iota(jnp.int32, x.shape, x.ndim - 1)
    lo = jnp.sum(jnp.where(lane < HEAD_DIM, x, 0.0), -1, keepdims=True)
    tot = jnp.sum(x, -1, keepdims=True)
    return jnp.where(lane < HEAD_DIM, lo, tot - lo)


def _bmm(a, b):
    return lax.dot_general(a.astype(_BF), b.astype(_BF), (((2,), (1,)), ((0,), (0,))),
                           preferred_element_type=_F32)


def _bmm_nt(a, b):
    return lax.dot_general(a.astype(_BF), b.astype(_BF), (((2,), (2,)), ((0,), (0,))),
                           preferred_element_type=_F32)


def _bmm_tn(a, b):
    return lax.dot_general(a.astype(_BF), b.astype(_BF), (((1,), (1,)), ((0,), (0,))),
                           preferred_element_type=_F32)


def _lane_tiles(x):
    return jnp.stack([x[:, i * LANES:(i + 1) * LANES] for i in range(HEAD_PAIRS)])


N_SLOT_REFS = 5


def _scan_kernel(*refs):
    tokens = (refs[0:5], refs[5:10])
    wup_ref, w0_ref, aup_ref, a0_ref, kk_ref, ka_ref, mi_ref, mt_ref, mb_ref = refs[10:19]
    y_refs = refs[19:21]
    s_ref = refs[21]
    slots = (refs[22:22 + N_SLOT_REFS], refs[22 + N_SLOT_REFS:])
    step = pl.program_id(1)

    @pl.when(step == 0)
    def _():
        s_ref[...] = jnp.zeros_like(s_ref)
        for ref in slots[1]:
            ref[...] = jnp.zeros_like(ref)

    def run(prep, solve):
        pieces = itertools.chain.from_iterable(
            _scan_prepare(tokens[d], (wup_ref.at[d], w0_ref.at[d], aup_ref.at[d], a0_ref.at[d],
                                      kk_ref, ka_ref, mi_ref.at[d]), slots[prep], d)
            for d in range(2))
        next(pieces)
        for _ in _scan_solve(slots[solve], mt_ref, mb_ref, y_refs, s_ref):
            next(pieces, None)
        for _ in pieces:
            pass

    @pl.when(step % 2 == 0)
    def _():
        run(0, 1)

    @pl.when(step % 2 == 1)
    def _():
        run(1, 0)


PREP_PAIRS = 2
PREP_PIECES = HEAD_PAIRS // PREP_PAIRS


def _scan_prepare(token_refs, param_refs, slot, d):
    r_ref, k_ref, v_ref, lw_ref, la_ref = token_refs
    wup_ref, w0_ref, aup_ref, a0_ref, kk_ref, ka_ref, mi_ref = param_refs
    lhs_ref, rk_ref, vv_ref, ke_ref, dt_ref = slot
    width = PREP_PAIRS * LANES

    def tiles(x):
        return jnp.stack([x[:, i * LANES:(i + 1) * LANES] for i in range(PREP_PAIRS)])

    lw_act = jnp.tanh(lw_ref[...])
    la_in = la_ref[...]
    m_incl = mi_ref[...]
    for g in range(PREP_PIECES):
        cols = pl.ds(g * width, width)
        pairs = pl.ds(g * PREP_PAIRS, PREP_PAIRS)
        xw = w0_ref[:, cols] + _mm(lw_act, wup_ref[:, cols])
        ld = -(math.exp(-0.5) * math.log2(math.e)) * _sigmoid(xw)
        a = tiles(_sigmoid(a0_ref[:, cols] + _mm(la_in, aup_ref[:, cols])))
        hi = ld.astype(_BF)
        lo = (ld - hi.astype(_F32)).astype(_BF)
        cl = (jnp.dot(m_incl, hi, preferred_element_type=_F32)
              + jnp.dot(m_incl, lo, preferred_element_type=_F32))
        dec_tot = jnp.exp2(jnp.sum(ld, 0, keepdims=True))
        e_in = tiles(jnp.exp2(cl))
        e_out_flat = jnp.exp2(-cl)
        e_out = tiles(e_out_flat)
        e_ex = tiles(jnp.exp2(cl - ld))
        e_end = tiles(dec_tot * e_out_flat)

        r = r_ref[pairs]
        k = k_ref[pairs]
        kk = k * tiles(kk_ref[:, cols])
        kk = kk * lax.rsqrt(jnp.maximum(_head_sum(kk * kk), 1e-24))
        k_d = k * (1.0 + (a - 1.0) * tiles(ka_ref[:, cols]))
        kka = kk * a

        rows = pl.ds(d * HEAD_PAIRS + g * PREP_PAIRS, PREP_PAIRS)
        lhs_ref[rows] = jnp.concatenate([kk * e_ex, r * e_in], axis=1).astype(_BF)
        rk_ref[rows] = jnp.swapaxes(jnp.concatenate([k_d * e_out, kka * e_out], axis=1),
                                    1, 2).astype(_BF)
        vv_ref[rows] = v_ref[pairs].astype(_BF)
        ke_ref[rows] = jnp.concatenate([k_d * e_end, -(kka * e_end)], axis=1).astype(_BF)
        dt_ref[rows] = tiles(dec_tot)
        yield


def _by_direction(x, m_ref):
    return jnp.concatenate([x[:HEAD_PAIRS] * m_ref[0], x[HEAD_PAIRS:] * m_ref[1]], axis=0)


def _scan_solve(slot, mt_ref, mb_ref, y_refs, s_ref):
    c = SCAN_CHUNK
    lhs_ref, rk_ref, vv_ref, ke_ref, dt_ref = slot
    lhs = lhs_ref[...]
    rk = rk_ref[...]
    v = vv_ref[...]
    zero = jnp.zeros((), _BF)
    even = lax.broadcasted_iota(jnp.int32, (1, 1, LANES), 2) < HEAD_DIM
    kkt, rt = lhs[:, :c], lhs[:, c:]
    main = _bmm(jnp.concatenate([jnp.where(even, kkt, zero), jnp.where(even, rt, zero),
                                 jnp.where(even, zero, rt), jnp.where(even, zero, kkt)], axis=1), rk)
    yield
    sbd = s_ref[...]
    ls = _bmm(lhs, jnp.swapaxes(sbd, 1, 2))
    yield
    top_odd = pltpu.roll(main[:, 3 * c:], HEAD_DIM, 2)
    main = main[:, :3 * c].astype(_BF)
    top = _by_direction(jnp.concatenate([main[:, :c], top_odd.astype(_BF)], axis=1), mt_ref)
    bot = _by_direction(main[:, c:], mb_ref)
    row_even = lax.broadcasted_iota(jnp.int32, (1, 2 * c, LANES), 1) < c
    is_x = row_even == even
    vv = jnp.concatenate([v, v], axis=1)
    out = _bmm(jnp.where(is_x, top, zero), vv)
    yield
    x = jnp.concatenate([ls[:, :c], ls[:, :c]], axis=1) + out
    zb = jnp.where(is_x, x.astype(_BF), top)
    n_steps = int(math.log2(c))
    for i in range(n_steps):
        out = _bmm(jnp.where(is_x, zero, zb), jnp.concatenate([zb[:, c:], zb[:, :c]], axis=1))
        yield
        x = x + out
        if i + 1 < n_steps:
            zb = jnp.where(is_x, x.astype(_BF), out.astype(_BF))
    u = jnp.where(even, x[:, :c], x[:, c:])
    vu = jnp.concatenate([v, u.astype(_BF)], axis=1)
    yy = _bmm(bot, vu)
    upd = _bmm_tn(vu, ke_ref[...])
    yield
    y = ls[:, c:] + jnp.where(even, yy[:, :c], yy[:, c:])
    y_refs[0][...] = y[:HEAD_PAIRS]
    y_refs[1][...] = y[HEAD_PAIRS:]
    blk_row = lax.broadcasted_iota(jnp.int32, (1, LANES, LANES), 1) < HEAD_DIM
    blk_col = lax.broadcasted_iota(jnp.int32, (1, LANES, LANES), 2) < HEAD_DIM
    s_ref[...] = jnp.where(blk_row == blk_col, sbd * dt_ref[...] + upd, 0.0)


def _scan_chunk_index(d, s):
    return (s + LAT_CHUNKS) % N_CHUNKS if d == 0 else N_CHUNKS - 1 - s


def _wkv_scan(zr, wup, w0, aup, a0, k_k, k_a, m_incl, m_top, m_bot):
    batch = zr.shape[0]
    c = SCAN_CHUNK

    def prep_chunk(d, s):
        return _scan_chunk_index(d, jnp.minimum(s, N_CHUNKS - 1))

    def solve_chunk(d, s):
        return _scan_chunk_index(d, jnp.maximum(s - 1, 0))

    def tokens(d):
        def group(idx):
            return pl.BlockSpec((None, HEAD_PAIRS, c, LANES),
                                lambda b, s: (b, idx, prep_chunk(d, s), 0))

        def lora_tile(idx):
            return pl.BlockSpec((None, None, c, LANES), lambda b, s: (b, idx, prep_chunk(d, s), 0))

        return [group(0), group(1), group(2), lora_tile(TILE_W), lora_tile(TILE_A)]

    def whole(*shape):
        return pl.BlockSpec(shape, lambda b, s: (0,) * len(shape))

    def y_spec(d):
        return pl.BlockSpec((None, HEAD_PAIRS, c, LANES), lambda b, s: (b, 0, solve_chunk(d, s), 0))

    both = 2 * HEAD_PAIRS
    slot = [pltpu.VMEM((both, 2 * c, LANES), _BF), pltpu.VMEM((both, 2 * c, LANES), _BF),
            pltpu.VMEM((both, c, LANES), _BF), pltpu.VMEM((both, 2 * c, LANES), _BF),
            pltpu.VMEM((both, 1, LANES), _F32)]
    assert len(slot) == N_SLOT_REFS
    y_shape = jax.ShapeDtypeStruct((batch, HEAD_PAIRS, T_ALL, LANES), _F32)
    return pl.pallas_call(
        _scan_kernel,
        grid=(batch, N_CHUNKS + 1),
        in_specs=tokens(0) + tokens(1) + [
            whole(2, LANES, D_MODEL), whole(2, 1, D_MODEL),
            whole(2, LANES, D_MODEL), whole(2, 1, D_MODEL),
            whole(1, D_MODEL), whole(1, D_MODEL),
            whole(2, c, c), whole(2, 2 * c, 2 * c), whole(2, 2 * c, 2 * c),
        ],
        out_specs=[y_spec(0), y_spec(1)],
        out_shape=[y_shape, y_shape],
        scratch_shapes=[pltpu.VMEM((both, LANES, LANES), _F32)] + slot * 2,
        compiler_params=_params("arbitrary", "arbitrary"),
        name="wkv_scan",
    )(*([zr] * 10), wup, w0, aup, a0, k_k, k_a, m_incl, m_top, m_bot)


def _rwkv_readout(yf_ref, yb_ref, r_ref, k_ref, v_ref, gz_ref, gup_ref, gng_ref, gnb_ref, rk_ref,
                  o_ref):
    gate = _mm(_sigmoid(gz_ref[...]), gup_ref[...])
    for hp in range(HEAD_PAIRS):
        cols = slice(hp * LANES, (hp + 1) * LANES)
        y = yf_ref[hp] + yb_ref[hp]
        mean = _head_sum(y) * (1.0 / HEAD_DIM)
        yc = y - mean
        var = _head_sum(yc * yc) * (1.0 / HEAD_DIM)
        yn = yc * lax.rsqrt(var + GN_EPS) * gng_ref[:, cols] + gnb_ref[:, cols]
        bonus = _head_sum(r_ref[hp] * k_ref[hp] * rk_ref[:, cols]) * v_ref[hp]
        o_ref[:, cols] = ((yn + bonus) * gate[:, cols]).astype(o_ref.dtype)


_H_SLOT = GRID_W + 16
_H_LEAD = 16
_V_PAD = CONV_HALF * GRID_W
_CTX_LEAD = 16


def _conv_taps(pad_ref, w, base, length, stride):
    acc = None
    for j in range(CONV_WIDTH):
        term = pad_ref[pl.ds(base + (j - CONV_HALF) * stride, length), :] * w[j:j + 1]
        acc = term if acc is None else acc + term
    return acc


def _glu(zv_ref, zg_ref, rows):
    return zv_ref[rows, :].astype(_F32) * _sigmoid(zg_ref[rows, :].astype(_F32))


def _conv_kernel(horizontal, zv_ref, zg_ref, w_ref, b_ref, o_ref, lat_ref, ctx_ref):
    w = w_ref[...]
    bias = b_ref[...]
    lat_ref[...] = jnp.zeros_like(lat_ref)
    ctx_ref[...] = jnp.zeros_like(ctx_ref)
    ctx_ref[pl.ds(_CTX_LEAD, CTX_LEN), :] = (
        _glu(zv_ref, zg_ref, pl.ds(SEQ, CTX_LEN)))
    o_ref[pl.ds(SEQ, CTX_LEN), :] = _conv_taps(ctx_ref, w, _CTX_LEAD, CTX_LEN, 1) + bias
    if horizontal:
        for row in range(GRID_H):
            src = pl.ds(row * GRID_W, GRID_W)
            lat_ref[pl.ds(_H_LEAD + row * _H_SLOT, GRID_W), :] = (
                _glu(zv_ref, zg_ref, src))
        for row in range(GRID_H):
            o_ref[pl.ds(row * GRID_W, GRID_W), :] = (
                _conv_taps(lat_ref, w, _H_LEAD + row * _H_SLOT, GRID_W, 1) + bias)
    else:
        lat_ref[pl.ds(_V_PAD, SEQ), :] = (
            _glu(zv_ref, zg_ref, pl.ds(0, SEQ)))
        blk = 4 * GRID_W
        for i in range(SEQ // blk):
            o_ref[pl.ds(i * blk, blk), :] = _conv_taps(lat_ref, w, _V_PAD + i * blk, blk, GRID_W) + bias


def _conformer_conv(z, dw, dw_b, horizontal):
    batch = z.shape[0]
    first = Z_CONV // LANES
    n_tiles = CONV_DIM // LANES
    lat_rows = (_H_LEAD + GRID_H * _H_SLOT) if horizontal else (SEQ + 2 * _V_PAD)
    return pl.pallas_call(
        functools.partial(_conv_kernel, horizontal),
        grid=(batch, n_tiles),
        in_specs=[
            pl.BlockSpec((None, T_ALL, LANES), lambda b, j: (b, 0, first + j)),
            pl.BlockSpec((None, T_ALL, LANES), lambda b, j: (b, 0, first + n_tiles + j)),
            pl.BlockSpec((CONV_WIDTH, LANES), lambda b, j: (0, j)),
            pl.BlockSpec((1, LANES), lambda b, j: (0, j)),
        ],
        out_specs=pl.BlockSpec((None, T_ALL, LANES), lambda b, j: (b, 0, j)),
        out_shape=jax.ShapeDtypeStruct((batch, T_ALL, CONV_DIM), _F32),
        scratch_shapes=[pltpu.VMEM((lat_rows, LANES), _F32),
                        pltpu.VMEM((CTX_LEN + 2 * _CTX_LEAD, LANES), _F32)],
        compiler_params=_params("arbitrary", "arbitrary"),
        name="conformer_conv_h" if horizontal else "conformer_conv_v",
    )(z, z, dw, dw_b)


def _sgu_kernel(zu_ref, zv_ref, lng_ref, lnb_ref, ws_ref, bs_ref, o_ref):
    v = _standardize(_gelu_tanh(zv_ref[...].astype(_F32)), LN_EPS) * lng_ref[...] + lnb_ref[...]
    group = lax.broadcasted_iota(jnp.int32, (SGU_CHUNK, SGU_DIM), 1) // SGU_GROUP_DIM
    for n in range(ROW_TILE // SGU_CHUNK):
        rows = slice(n * SGU_CHUNK, (n + 1) * SGU_CHUNK)
        vc = v[rows]
        mixed = bs_ref[...]
        for g in range(SGU_GROUPS):
            mixed = jnp.where(group == g, mixed + _mm(ws_ref[g], vc), mixed)
        o_ref[rows, :] = (_gelu_tanh(zu_ref[rows, :].astype(_F32)) * mixed).astype(o_ref.dtype)


def _sgu(z, ln_g, ln_b, w_s, b_tile):
    batch = z.shape[0]
    first = Z_SGU // SGU_DIM
    vec = pl.BlockSpec((1, SGU_DIM), lambda b, i: (0, 0))
    return pl.pallas_call(
        _sgu_kernel,
        grid=(batch, N_ROW_TILES),
        in_specs=[
            pl.BlockSpec((None, ROW_TILE, SGU_DIM), lambda b, i: (b, i, first)),
            pl.BlockSpec((None, ROW_TILE, SGU_DIM), lambda b, i: (b, i, first + 1)),
            vec, vec,
            pl.BlockSpec((SGU_GROUPS, SGU_CHUNK, SGU_CHUNK), lambda b, i: (0, 0, 0)),
            pl.BlockSpec((SGU_CHUNK, SGU_DIM), lambda b, i: (0, 0)),
        ],
        out_specs=pl.BlockSpec((None, ROW_TILE, SGU_DIM), lambda b, i: (b, i, 0)),
        out_shape=jax.ShapeDtypeStruct((batch, T_ALL, SGU_DIM), _BF),
        compiler_params=_params("arbitrary", "arbitrary"),
        name="sgu",
    )(z, z, ln_g, ln_b, w_s, b_tile)


def _merge_kernel(x_ref, yf_ref, yb_ref, r_ref, k_ref, v_ref, gz_ref, cv_ref, c_ref,
                  g0_ref, g1_ref, g2_ref, gate_ref, gup_ref, gng_ref, gnb_ref, rk_ref, lng_ref, lnb_ref,
                  wro_ref, wco_ref, wso_ref, wm_ref, o_ref, a_ref):
    _rwkv_readout(yf_ref, yb_ref, r_ref, k_ref, v_ref, gz_ref, gup_ref, gng_ref, gnb_ref, rk_ref, a_ref)
    a = _mm(a_ref[...], wro_ref[...])
    cb = _silu(_standardize(cv_ref[...], LN_EPS) * lng_ref[...] + lnb_ref[...])
    b = _mm(cb, wco_ref[...])
    c = _mm(c_ref[...], wso_ref[...])
    m = (a * _sigmoid(g0_ref[...].astype(_F32)) + b * _sigmoid(g1_ref[...].astype(_F32))
         + c * _sigmoid(g2_ref[...].astype(_F32)))
    o_ref[...] = x_ref[...] + gate_ref[...] * _mm(m, wm_ref[...])


def _merge(x_all, y_fwd, y_bwd, zr, conv, c_pre, z, gate1, g_up, gn_g, gn_b, r_k, ln_g, ln_b,
           w_ro, w_co, w_so, w_m):
    batch = x_all.shape[0]

    def rows(width, col=0):
        return pl.BlockSpec((None, ROW_TILE, width), lambda b, i: (b, i, col))

    def group(idx):
        return pl.BlockSpec((None, HEAD_PAIRS, ROW_TILE, LANES), lambda b, i: (b, idx, i, 0))

    def whole(shape):
        return pl.BlockSpec(shape, lambda b, i: (0, 0))

    vec = whole((1, D_MODEL))
    return pl.pallas_call(
        _merge_kernel,
        grid=(batch, N_ROW_TILES),
        in_specs=[
            rows(D_MODEL), group(0), group(0), group(0), group(1), group(2),
            pl.BlockSpec((None, None, ROW_TILE, LANES), lambda b, i: (b, TILE_G, i, 0)),
            rows(CONV_DIM), rows(SGU_DIM),
            rows(D_MODEL, 0), rows(D_MODEL, 1), rows(D_MODEL, 2),
            _mod_spec(2), whole((GATE_LORA, D_MODEL)), vec, vec, vec,
            whole((1, CONV_DIM)), whole((1, CONV_DIM)),
            whole((D_MODEL, D_MODEL)), whole((CONV_DIM, D_MODEL)), whole((SGU_DIM, D_MODEL)),
            whole((D_MODEL, D_MODEL)),
        ],
        out_specs=rows(D_MODEL),
        out_shape=jax.ShapeDtypeStruct((batch, T_ALL, D_MODEL), _F32),
        scratch_shapes=[pltpu.VMEM((ROW_TILE, D_MODEL), _BF)],
        compiler_params=_params("arbitrary", "arbitrary"),
        name="merge",
    )(x_all, y_fwd, y_bwd, zr, zr, zr, zr, conv, c_pre, z, z, z, gate1, g_up, gn_g, gn_b, r_k,
      ln_g, ln_b, w_ro, w_co, w_so, w_m)


def _ffn_kernel(last, x_ref, g_ref, sh_ref, sc_ref, gate_ref, win_ref, wout_ref, fg_ref, o_ref):
    x = x_ref[...]
    h = _rms_mod(x, g_ref[...], sh_ref[...], sc_ref[...]).astype(_BF)
    acc = jnp.zeros((ROW_TILE, D_MODEL), _F32)
    for f in range(D_FF // FF_CHUNK):
        cols = slice(f * FF_CHUNK, (f + 1) * FF_CHUNK)
        up_cols = slice(D_FF + f * FF_CHUNK, D_FF + (f + 1) * FF_CHUNK)
        act = _silu(_mm(h, win_ref[:, cols])) * _mm(h, win_ref[:, up_cols])
        acc = acc + _mm(act, wout_ref[cols, :])
    out = x + gate_ref[...] * acc
    if last:
        out = out * lax.rsqrt(jnp.mean(out * out, -1, keepdims=True) + NORM_EPS) * fg_ref[...]
    o_ref[...] = out


def _ffn(x_all, g, shift, scale, gate2, w_in, w_out, final_g, last):
    batch = x_all.shape[0]
    rows = pl.BlockSpec((None, ROW_TILE, D_MODEL), lambda b, i: (b, i, 0))
    vec = pl.BlockSpec((1, D_MODEL), lambda b, i: (0, 0))
    return pl.pallas_call(
        functools.partial(_ffn_kernel, last),
        grid=(batch, LAT_ROW_TILES if last else N_ROW_TILES),
        in_specs=[
            rows, vec,
            _mod_spec(2), _mod_spec(2), _mod_spec(2),
            pl.BlockSpec((D_MODEL, 2 * D_FF), lambda b, i: (0, 0)),
            pl.BlockSpec((D_FF, D_MODEL), lambda b, i: (0, 0)),
            vec,
        ],
        out_specs=rows,
        out_shape=jax.ShapeDtypeStruct((batch, SEQ if last else T_ALL, D_MODEL), _F32),
        compiler_params=_params("arbitrary", "arbitrary"),
        name="swiglu_final" if last else "swiglu",
    )(x_all, g, shift, scale, gate2, w_in, w_out, final_g)


def _permute_w_in(w):
    off_conv = RWKV_COLS
    off_sgu = off_conv + 2 * CONV_DIM
    off_gate = off_sgu + 2 * SGU_DIM
    pad = jnp.zeros((D_MODEL, Z_CONV - Z_RWKV - RWKV_COLS), w.dtype)
    return jnp.concatenate([w[:, off_gate:], w[:, :off_conv], pad, w[:, off_conv:off_sgu],
                            w[:, off_sgu:off_gate]], axis=1)


def _direction_padded(w_up):
    zero = jnp.zeros_like(w_up[0])
    return jnp.stack([jnp.concatenate([w_up[0], zero], 0), jnp.concatenate([zero, w_up[1]], 0)])


def kernel(x, c, ctx, c_ctx, w_mod, b_mod, norm1_g, norm2_g, w_in, rwkv_shift, rwkv_w0, rwkv_w_up, rwkv_a0, rwkv_a_up, rwkv_g_up, rwkv_k_k, rwkv_k_a, rwkv_r_k, rwkv_gn_g, rwkv_gn_b, rwkv_out, conv_dw, conv_dw_b, conv_ln_g, conv_ln_b, conv_out, sgu_ln_g, sgu_ln_b, sgu_w, sgu_b, sgu_out, w_merge, ffn_w_in, ffn_w_out, final_norm_g):
    batch = x.shape[0]
    depth = w_mod.shape[0]
    assert x.shape[1:] == (SEQ, D_MODEL) and ctx.shape[1:] == (CTX_LEN, D_MODEL)

    rows = -(-(batch + 1) // 8) * 8
    cvec = jnp.zeros((rows, D_MODEL), _F32).at[:batch].set(c).at[batch].set(c_ctx)
    mod = _modulation(cvec, w_mod.astype(_BF), b_mod[:, None, :])
    mod_lat = mod[:, :batch].reshape(depth, batch, 6, D_MODEL)
    mod_ctx = jnp.broadcast_to(mod[:, batch].reshape(depth, 1, 6, D_MODEL), mod_lat.shape)
    mod_tab = jnp.stack([mod_lat, mod_ctx], axis=2).transpose(0, 3, 1, 2, 4)
    mod_tab = mod_tab.reshape(depth, 6, 2 * batch, 1, D_MODEL)

    idx = jnp.arange(SCAN_CHUNK)
    incl = jnp.stack([idx[:, None] >= idx[None, :], idx[:, None] <= idx[None, :]]).astype(_F32)
    strict = jnp.stack([idx[:, None] > idx[None, :], idx[:, None] < idx[None, :]]).astype(_F32)
    m_top = jnp.concatenate([jnp.concatenate([strict, -strict], 2),
                             jnp.concatenate([-strict, strict], 2)], 1)
    m_bot = jnp.concatenate([jnp.concatenate([incl, -incl], 2)] * 2, 1).astype(_BF)
    m_top = m_top.astype(_BF)
    m_incl = incl.astype(_BF)

    x_all = jnp.concatenate([x, ctx], axis=1)
    for l in range(depth):
        sh1, sc1, g1, sh2, sc2, g2 = (mod_tab[l, i] for i in range(6))
        z = _in_projection(x_all, norm1_g[l][None], sh1, sc1, _permute_w_in(w_in[l]).astype(_BF))
        zr = _token_shift(z, rwkv_shift[l])
        y_fwd, y_bwd = _wkv_scan(zr, _direction_padded(rwkv_w_up[l]).astype(_BF), rwkv_w0[l][:, None, :],
                      _direction_padded(rwkv_a_up[l]).astype(_BF), rwkv_a0[l][:, None, :],
                      rwkv_k_k[l][None], rwkv_k_a[l][None], m_incl, m_top, m_bot)
        conv = _conformer_conv(z, conv_dw[l], conv_dw_b[l][None], l % 2 == 0)
        b_tile = jnp.repeat(sgu_b[l].T, SGU_GROUP_DIM, axis=1)
        c_pre = _sgu(z, sgu_ln_g[l][None], sgu_ln_b[l][None], sgu_w[l].astype(_BF), b_tile)
        x_all = _merge(x_all, y_fwd, y_bwd, zr, conv, c_pre, z, g1, rwkv_g_up[l].astype(_BF),
                       rwkv_gn_g[l][None], rwkv_gn_b[l][None], rwkv_r_k[l][None],
                       conv_ln_g[l][None], conv_ln_b[l][None],
                       rwkv_out[l].astype(_BF), conv_out[l].astype(_BF), sgu_out[l].astype(_BF),
                       w_merge[l].astype(_BF))
        x_all = _ffn(x_all, norm2_g[l][None], sh2, sc2, g2, ffn_w_in[l].astype(_BF),
                     ffn_w_out[l].astype(_BF), final_norm_g[None], l == depth - 1)
    return x_all
```

```python
import functools
import itertools
import math

import jax
import jax.numpy as jnp
from jax import lax
from jax.experimental import pallas as pl
from jax.experimental.pallas import tpu as pltpu

D_MODEL = 1024
SEQ = 2048
CTX_LEN = 256
T_ALL = SEQ + CTX_LEN
GRID_W = 64
GRID_H = SEQ // GRID_W

HEAD_DIM = 64
HEADS = D_MODEL // HEAD_DIM
HEAD_PAIRS = HEADS // 2
DECAY_LORA = 64
ICLR_LORA = 64
GATE_LORA = 128
CONV_DIM = D_MODEL // 2
CONV_WIDTH = 31
CONV_HALF = CONV_WIDTH // 2
SGU_DIM = D_MODEL // 2
SGU_GROUPS = 8
SGU_GROUP_DIM = SGU_DIM // SGU_GROUPS
SGU_CHUNK = 128
D_FF = ((8 * D_MODEL // 3 + 255) // 256) * 256
NORM_EPS = 1e-6
LN_EPS = 1e-5
GN_EPS = 64e-5

RWKV_COLS = 3 * D_MODEL + 2 * DECAY_LORA + 2 * ICLR_LORA + GATE_LORA

LANES = 128
Z_GATE = 0
Z_RWKV = 3 * D_MODEL
Z_CONV = Z_RWKV + RWKV_COLS + 128
Z_SGU = Z_CONV + 2 * CONV_DIM
Z_COLS = Z_SGU + 2 * SGU_DIM
RWKV_TILES = RWKV_COLS // LANES
TILE_W = 3 * HEAD_PAIRS
TILE_A = TILE_W + 1
TILE_G = TILE_A + 1

ROW_TILE = 256
N_ROW_TILES = T_ALL // ROW_TILE
LAT_ROW_TILES = SEQ // ROW_TILE
SCAN_CHUNK = 64
N_CHUNKS = T_ALL // SCAN_CHUNK
LAT_CHUNKS = SEQ // SCAN_CHUNK
FF_CHUNK = 256

VMEM_LIMIT = 56 * 1024 * 1024

_BF = jnp.bfloat16
_F32 = jnp.float32


def _mm(a, b):
    return jnp.dot(a.astype(_BF), b.astype(_BF), preferred_element_type=_F32)


def _mm_f32(a, b):
    return jnp.dot(a, b, preferred_element_type=_F32, precision=lax.Precision.HIGHEST)


def _mm_nt(a, b):
    return lax.dot_general(a.astype(_BF), b.astype(_BF), (((1,), (1,)), ((), ())),
                           preferred_element_type=_F32)


def _mm_tn(a, b):
    return lax.dot_general(a.astype(_BF), b.astype(_BF), (((0,), (0,)), ((), ())),
                           preferred_element_type=_F32)


def _sigmoid(x):
    return jax.nn.sigmoid(x)


def _silu(x):
    return x * jax.nn.sigmoid(x)


def _gelu_tanh(x):
    return 0.5 * x * (1.0 + jnp.tanh(math.sqrt(2.0 / math.pi) * (x + 0.044715 * (x * x * x))))


def _standardize(x, eps):
    xc = x - jnp.mean(x, -1, keepdims=True)
    return xc * lax.rsqrt(jnp.mean(xc * xc, -1, keepdims=True) + eps)


def _rms_mod(x, g, shift, scale):
    y = x * lax.rsqrt(jnp.mean(x * x, -1, keepdims=True) + NORM_EPS) * g
    return y * (1.0 + scale) + shift


def _params(*sem):
    return pltpu.CompilerParams(dimension_semantics=sem, vmem_limit_bytes=VMEM_LIMIT)


def _mod_kernel(c_ref, w_ref, b_ref, o_ref):
    o_ref[...] = _mm(_silu(c_ref[...]), w_ref[...]) + b_ref[...]


def _modulation(cvec, w_mod, b_mod):
    depth = w_mod.shape[0]
    rows = cvec.shape[0]
    return pl.pallas_call(
        _mod_kernel,
        grid=(depth, 6),
        in_specs=[
            pl.BlockSpec((rows, D_MODEL), lambda l, j: (0, 0)),
            pl.BlockSpec((None, D_MODEL, D_MODEL), lambda l, j: (l, 0, j)),
            pl.BlockSpec((None, 1, D_MODEL), lambda l, j: (l, 0, j)),
        ],
        out_specs=pl.BlockSpec((None, rows, D_MODEL), lambda l, j: (l, 0, j)),
        out_shape=jax.ShapeDtypeStruct((depth, rows, 6 * D_MODEL), _F32),
        compiler_params=_params("arbitrary", "arbitrary"),
        name="modulation",
    )(cvec, w_mod, b_mod)


def _mod_spec(grid_rank):
    if grid_rank == 2:
        return pl.BlockSpec((None, 1, D_MODEL), lambda b, i: (2 * b + i // LAT_ROW_TILES, 0, 0))
    return pl.BlockSpec((None, 1, D_MODEL), lambda j, b, i: (2 * b + i // LAT_ROW_TILES, 0, 0))


def _inproj_kernel(x_ref, g_ref, sh_ref, sc_ref, w_ref, o_ref):
    h = _rms_mod(x_ref[...], g_ref[...], sh_ref[...], sc_ref[...])
    o_ref[...] = _mm(h, w_ref[...]).astype(o_ref.dtype)


def _in_projection(x_all, g, shift, scale, w):
    batch = x_all.shape[0]
    n_col = 2
    tn = Z_COLS // n_col
    return pl.pallas_call(
        _inproj_kernel,
        grid=(n_col, batch, N_ROW_TILES),
        in_specs=[
            pl.BlockSpec((None, ROW_TILE, D_MODEL), lambda j, b, i: (b, i, 0)),
            pl.BlockSpec((1, D_MODEL), lambda j, b, i: (0, 0)),
            _mod_spec(3),
            _mod_spec(3),
            pl.BlockSpec((D_MODEL, tn), lambda j, b, i: (0, j)),
        ],
        out_specs=pl.BlockSpec((None, ROW_TILE, tn), lambda j, b, i: (b, i, j)),
        out_shape=jax.ShapeDtypeStruct((batch, T_ALL, Z_COLS), _BF),
        compiler_params=_params("arbitrary", "arbitrary", "arbitrary"),
        name="in_projection",
    )(x_all, g, shift, scale, w)


SHIFT_TILES = 3


def _shift_kernel(z_ref, w_ref, o_ref):
    sub = 8
    for i in range(SHIFT_TILES):
        cols = slice(i * LANES, (i + 1) * LANES)
        z = z_ref[:, cols].astype(_F32)
        prev = pltpu.roll(z, 1, 0)
        nxt = pltpu.roll(z, T_ALL - 1, 0)
        w = w_ref[:, cols]
        o_ref[i] = prev * w[0:1] + z * w[1:2] + nxt * w[2:3]
        for g0 in (0, SEQ - sub, SEQ, T_ALL - sub):
            t = g0 + lax.broadcasted_iota(jnp.int32, (sub, LANES), 0)
            rows = slice(g0, g0 + sub)
            p = jnp.where((t == 0) | (t == SEQ), 0.0, prev[rows])
            n = jnp.where((t == SEQ - 1) | (t == T_ALL - 1), 0.0, nxt[rows])
            o_ref[i, rows, :] = p * w[0:1] + z[rows] * w[1:2] + n * w[2:3]


def _token_shift(z, w_shift):
    batch = z.shape[0]
    width = SHIFT_TILES * LANES
    first = Z_RWKV // width
    return pl.pallas_call(
        _shift_kernel,
        grid=(batch, RWKV_TILES // SHIFT_TILES),
        in_specs=[
            pl.BlockSpec((None, T_ALL, width), lambda b, j: (b, 0, first + j)),
            pl.BlockSpec((3, width), lambda b, j: (0, j)),
        ],
        out_specs=pl.BlockSpec((None, SHIFT_TILES, T_ALL, LANES), lambda b, j: (b, j, 0, 0)),
        out_shape=jax.ShapeDtypeStruct((batch, RWKV_TILES, T_ALL, LANES), _F32),
        compiler_params=_params("arbitrary", "arbitrary"),
        name="token_shift",
    )(z, w_shift)


def _head_sum(x):
    lane = lax.broadcasted_iota(jnp.int32, x.shape, x.ndim - 1)
    lo = jnp.sum(jnp.where(lane < HEAD_DIM, x, 0.0), -1, keepdims=True)
    tot = jnp.sum(x, -1, keepdims=True)
    return jnp.where(lane < HEAD_DIM, lo, tot - lo)


def _bmm(a, b):
    return lax.dot_general(a.astype(_BF), b.astype(_BF), (((2,), (1,)), ((0,), (0,))),
                           preferred_element_type=_F32)


def _bmm_nt(a, b):
    return lax.dot_general(a.astype(_BF), b.astype(_BF), (((2,), (2,)), ((0,), (0,))),
                           preferred_element_type=_F32)


def _bmm_tn(a, b):
    return lax.dot_general(a.astype(_BF), b.astype(_BF), (((1,), (1,)), ((0,), (0,))),
                           preferred_element_type=_F32)


def _lane_tiles(x):
    return jnp.stack([x[:, i * LANES:(i + 1) * LANES] for i in range(HEAD_PAIRS)])


N_SLOT_REFS = 5


def _scan_kernel(*refs):
    tokens = (refs[0:5], refs[5:10])
    wup_ref, w0_ref, aup_ref, a0_ref, kk_ref, ka_ref, mi_ref, mt_ref, mb_ref = refs[10:19]
    y_refs = refs[19:21]
    s_ref = refs[21]
    slots = (refs[22:22 + N_SLOT_REFS], refs[22 + N_SLOT_REFS:])
    step = pl.program_id(1)

    @pl.when(step == 0)
    def _():
        s_ref[...] = jnp.zeros_like(s_ref)
        for ref in slots[1]:
            ref[...] = jnp.zeros_like(ref)

    def run(prep, solve):
        pieces = itertools.chain.from_iterable(
            _scan_prepare(tokens[d], (wup_ref.at[d], w0_ref.at[d], aup_ref.at[d], a0_ref.at[d],
                                      kk_ref, ka_ref, mi_ref.at[d]), slots[prep], d)
            for d in range(2))
        next(pieces)
        for _ in _scan_solve(slots[solve], mt_ref, mb_ref, y_refs, s_ref):
            next(pieces, None)
        for _ in pieces:
            pass

    @pl.when(step % 2 == 0)
    def _():
        run(0, 1)

    @pl.when(step % 2 == 1)
    def _():
        run(1, 0)


PREP_PAIRS = 8
PREP_PIECES = HEAD_PAIRS // PREP_PAIRS


def _scan_prepare(token_refs, param_refs, slot, d):
    r_ref, k_ref, v_ref, lw_ref, la_ref = token_refs
    wup_ref, w0_ref, aup_ref, a0_ref, kk_ref, ka_ref, mi_ref = param_refs
    lhs_ref, rk_ref, vv_ref, ke_ref, dt_ref = slot
    width = PREP_PAIRS * LANES

    def tiles(x):
        return jnp.stack([x[:, i * LANES:(i + 1) * LANES] for i in range(PREP_PAIRS)])

    lw_act = jnp.tanh(lw_ref[...])
    la_in = la_ref[...]
    m_incl = mi_ref[...]
    for g in range(PREP_PIECES):
        cols = pl.ds(g * width, width)
        pairs = pl.ds(g * PREP_PAIRS, PREP_PAIRS)
        xw = w0_ref[:, cols] + _mm(lw_act, wup_ref[:, cols])
        ld = -(math.exp(-0.5) * math.log2(math.e)) * _sigmoid(xw)
        a = tiles(_sigmoid(a0_ref[:, cols] + _mm(la_in, aup_ref[:, cols])))
        hi = ld.astype(_BF)
        rem = ld - hi.astype(_F32)
        mid = rem.astype(_BF)
        lo = (rem - mid.astype(_F32)).astype(_BF)
        cl = (jnp.dot(m_incl, hi, preferred_element_type=_F32)
              + jnp.dot(m_incl, mid, preferred_element_type=_F32)
              + jnp.dot(m_incl, lo, preferred_element_type=_F32))
        ld_tot = jnp.sum(ld, 0, keepdims=True)
        dec_tot = jnp.exp2(ld_tot)
        e_in = tiles(jnp.exp2(cl))
        e_out = tiles(jnp.exp2(-cl))
        e_ex = tiles(jnp.exp2(cl - ld))
        e_end = tiles(jnp.exp2(ld_tot - cl))

        r = r_ref[pairs]
        k = k_ref[pairs]
        kk = k * tiles(kk_ref[:, cols])
        kk = kk * lax.rsqrt(jnp.maximum(_head_sum(kk * kk), 1e-24))
        k_d = k * (1.0 + (a - 1.0) * tiles(ka_ref[:, cols]))
        kka = kk * a

        rows = pl.ds(d * HEAD_PAIRS + g * PREP_PAIRS, PREP_PAIRS)
        lhs_ref[rows] = jnp.concatenate([kk * e_ex, r * e_in], axis=1).astype(_BF)
        rk_ref[rows] = jnp.swapaxes(jnp.concatenate([k_d * e_out, kka * e_out], axis=1),
                                    1, 2).astype(_BF)
        vv_ref[rows] = v_ref[pairs].astype(_BF)
        ke_ref[rows] = jnp.concatenate([k_d * e_end, -(kka * e_end)], axis=1).astype(_BF)
        dt_ref[rows] = tiles(dec_tot)
        yield


def _by_direction(x, m_ref):
    return jnp.concatenate([x[:HEAD_PAIRS] * m_ref[0], x[HEAD_PAIRS:] * m_ref[1]], axis=0)


def _scan_solve(slot, mt_ref, mb_ref, y_refs, s_ref):
    c = SCAN_CHUNK
    lhs_ref, rk_ref, vv_ref, ke_ref, dt_ref = slot
    lhs = lhs_ref[...]
    rk = rk_ref[...]
    v = vv_ref[...]
    zero = jnp.zeros((), _BF)
    even = lax.broadcasted_iota(jnp.int32, (1, 1, LANES), 2) < HEAD_DIM
    kkt, rt = lhs[:, :c], lhs[:, c:]
    main = _bmm(jnp.concatenate([jnp.where(even, kkt, zero), jnp.where(even, rt, zero),
                                 jnp.where(even, zero, rt), jnp.where(even, zero, kkt)], axis=1), rk)
    yield
    sbd = s_ref[...]
    ls = _bmm(lhs, jnp.swapaxes(sbd, 1, 2))
    yield
    top_odd = pltpu.roll(main[:, 3 * c:], HEAD_DIM, 2)
    main = main[:, :3 * c].astype(_BF)
    top = _by_direction(jnp.concatenate([main[:, :c], top_odd.astype(_BF)], axis=1), mt_ref)
    bot = _by_direction(main[:, c:], mb_ref)
    row_even = lax.broadcasted_iota(jnp.int32, (1, 2 * c, LANES), 1) < c
    is_x = row_even == even
    vv = jnp.concatenate([v, v], axis=1)
    out = _bmm(jnp.where(is_x, top, zero), vv)
    yield
    x = jnp.concatenate([ls[:, :c], ls[:, :c]], axis=1) + out
    zb = jnp.where(is_x, x.astype(_BF), top)
    n_steps = int(math.log2(c))
    for i in range(n_steps):
        out = _bmm(jnp.where(is_x, zero, zb), jnp.concatenate([zb[:, c:], zb[:, :c]], axis=1))
        yield
        x = x + out
        if i + 1 < n_steps:
            zb = jnp.where(is_x, x.astype(_BF), out.astype(_BF))
    u = jnp.where(even, x[:, :c], x[:, c:])
    vu = jnp.concatenate([v, u.astype(_BF)], axis=1)
    yy = _bmm(bot, vu)
    upd = _bmm_tn(vu, ke_ref[...])
    yield
    y = ls[:, c:] + jnp.where(even, yy[:, :c], yy[:, c:])
    y_refs[0][...] = y[:HEAD_PAIRS]
    y_refs[1][...] = y[HEAD_PAIRS:]
    blk_row = lax.broadcasted_iota(jnp.int32, (1, LANES, LANES), 1) < HEAD_DIM
    blk_col = lax.broadcasted_iota(jnp.int32, (1, LANES, LANES), 2) < HEAD_DIM
    s_ref[...] = jnp.where(blk_row == blk_col, sbd * dt_ref[...] + upd, 0.0)


def _scan_chunk_index(d, s):
    return (s + LAT_CHUNKS) % N_CHUNKS if d == 0 else N_CHUNKS - 1 - s


def _wkv_scan(zr, wup, w0, aup, a0, k_k, k_a, m_incl, m_top, m_bot):
    batch = zr.shape[0]
    c = SCAN_CHUNK

    def prep_chunk(d, s):
        return _scan_chunk_index(d, jnp.minimum(s, N_CHUNKS - 1))

    def solve_chunk(d, s):
        return _scan_chunk_index(d, jnp.maximum(s - 1, 0))

    def tokens(d):
        def group(idx):
            return pl.BlockSpec((None, HEAD_PAIRS, c, LANES),
                                lambda b, s: (b, idx, prep_chunk(d, s), 0))

        def lora_tile(idx):
            return pl.BlockSpec((None, None, c, LANES), lambda b, s: (b, idx, prep_chunk(d, s), 0))

        return [group(0), group(1), group(2), lora_tile(TILE_W), lora_tile(TILE_A)]

    def whole(*shape):
        return pl.BlockSpec(shape, lambda b, s: (0,) * len(shape))

    def y_spec(d):
        return pl.BlockSpec((None, HEAD_PAIRS, c, LANES), lambda b, s: (b, 0, solve_chunk(d, s), 0))

    both = 2 * HEAD_PAIRS
    slot = [pltpu.VMEM((both, 2 * c, LANES), _BF), pltpu.VMEM((both, 2 * c, LANES), _BF),
            pltpu.VMEM((both, c, LANES), _BF), pltpu.VMEM((both, 2 * c, LANES), _BF),
            pltpu.VMEM((both, 1, LANES), _F32)]
    assert len(slot) == N_SLOT_REFS
    y_shape = jax.ShapeDtypeStruct((batch, HEAD_PAIRS, T_ALL, LANES), _F32)
    return pl.pallas_call(
        _scan_kernel,
        grid=(batch, N_CHUNKS + 1),
        in_specs=tokens(0) + tokens(1) + [
            whole(2, LANES, D_MODEL), whole(2, 1, D_MODEL),
            whole(2, LANES, D_MODEL), whole(2, 1, D_MODEL),
            whole(1, D_MODEL), whole(1, D_MODEL),
            whole(2, c, c), whole(2, 2 * c, 2 * c), whole(2, 2 * c, 2 * c),
        ],
        out_specs=[y_spec(0), y_spec(1)],
        out_shape=[y_shape, y_shape],
        scratch_shapes=[pltpu.VMEM((both, LANES, LANES), _F32)] + slot * 2,
        compiler_params=_params("arbitrary", "arbitrary"),
        name="wkv_scan",
    )(*([zr] * 10), wup, w0, aup, a0, k_k, k_a, m_incl, m_top, m_bot)


def _rwkv_readout(yf_ref, yb_ref, r_ref, k_ref, v_ref, gz_ref, gup_ref, gng_ref, gnb_ref, rk_ref,
                  o_ref):
    gate = _mm(_sigmoid(gz_ref[...]), gup_ref[...])
    for hp in range(HEAD_PAIRS):
        cols = slice(hp * LANES, (hp + 1) * LANES)
        y = yf_ref[hp] + yb_ref[hp]
        mean = _head_sum(y) * (1.0 / HEAD_DIM)
        yc = y - mean
        var = _head_sum(yc * yc) * (1.0 / HEAD_DIM)
        yn = yc * lax.rsqrt(var + GN_EPS) * gng_ref[:, cols] + gnb_ref[:, cols]
        bonus = _head_sum(r_ref[hp] * k_ref[hp] * rk_ref[:, cols]) * v_ref[hp]
        o_ref[:, cols] = ((yn + bonus) * gate[:, cols]).astype(o_ref.dtype)


_H_SLOT = GRID_W + 16
_H_LEAD = 16
_V_PAD = CONV_HALF * GRID_W
_CTX_LEAD = 16


def _conv_taps(pad_ref, w, base, length, stride):
    acc = None
    for j in range(CONV_WIDTH):
        term = pad_ref[pl.ds(base + (j - CONV_HALF) * stride, length), :] * w[j:j + 1]
        acc = term if acc is None else acc + term
    return acc


def _glu(zv_ref, zg_ref, rows):
    return zv_ref[rows, :].astype(_F32) * _sigmoid(zg_ref[rows, :].astype(_F32))


def _conv_kernel(horizontal, zv_ref, zg_ref, w_ref, b_ref, o_ref, lat_ref, ctx_ref):
    w = w_ref[...]
    bias = b_ref[...]
    lat_ref[...] = jnp.zeros_like(lat_ref)
    ctx_ref[...] = jnp.zeros_like(ctx_ref)
    ctx_ref[pl.ds(_CTX_LEAD, CTX_LEN), :] = (
        _glu(zv_ref, zg_ref, pl.ds(SEQ, CTX_LEN)))
    o_ref[pl.ds(SEQ, CTX_LEN), :] = _conv_taps(ctx_ref, w, _CTX_LEAD, CTX_LEN, 1) + bias
    if horizontal:
        for row in range(GRID_H):
            src = pl.ds(row * GRID_W, GRID_W)
            lat_ref[pl.ds(_H_LEAD + row * _H_SLOT, GRID_W), :] = (
                _glu(zv_ref, zg_ref, src))
        for row in range(GRID_H):
            o_ref[pl.ds(row * GRID_W, GRID_W), :] = (
                _conv_taps(lat_ref, w, _H_LEAD + row * _H_SLOT, GRID_W, 1) + bias)
    else:
        lat_ref[pl.ds(_V_PAD, SEQ), :] = (
            _glu(zv_ref, zg_ref, pl.ds(0, SEQ)))
        blk = 4 * GRID_W
        for i in range(SEQ // blk):
            o_ref[pl.ds(i * blk, blk), :] = _conv_taps(lat_ref, w, _V_PAD + i * blk, blk, GRID_W) + bias


def _conformer_conv(z, dw, dw_b, horizontal):
    batch = z.shape[0]
    first = Z_CONV // LANES
    n_tiles = CONV_DIM // LANES
    lat_rows = (_H_LEAD + GRID_H * _H_SLOT) if horizontal else (SEQ + 2 * _V_PAD)
    return pl.pallas_call(
        functools.partial(_conv_kernel, horizontal),
        grid=(batch, n_tiles),
        in_specs=[
            pl.BlockSpec((None, T_ALL, LANES), lambda b, j: (b, 0, first + j)),
            pl.BlockSpec((None, T_ALL, LANES), lambda b, j: (b, 0, first + n_tiles + j)),
            pl.BlockSpec((CONV_WIDTH, LANES), lambda b, j: (0, j)),
            pl.BlockSpec((1, LANES), lambda b, j: (0, j)),
        ],
        out_specs=pl.BlockSpec((None, T_ALL, LANES), lambda b, j: (b, 0, j)),
        out_shape=jax.ShapeDtypeStruct((batch, T_ALL, CONV_DIM), _F32),
        scratch_shapes=[pltpu.VMEM((lat_rows, LANES), _F32),
                        pltpu.VMEM((CTX_LEN + 2 * _CTX_LEAD, LANES), _F32)],
        compiler_params=_params("arbitrary", "arbitrary"),
        name="conformer_conv_h" if horizontal else "conformer_conv_v",
    )(z, z, dw, dw_b)


def _sgu_kernel(zu_ref, zv_ref, lng_ref, lnb_ref, ws_ref, bs_ref, o_ref):
    v = _standardize(_gelu_tanh(zv_ref[...].astype(_F32)), LN_EPS) * lng_ref[...] + lnb_ref[...]
    group = lax.broadcasted_iota(jnp.int32, (SGU_CHUNK, SGU_DIM), 1) // SGU_GROUP_DIM
    for n in range(ROW_TILE // SGU_CHUNK):
        rows = slice(n * SGU_CHUNK, (n + 1) * SGU_CHUNK)
        vc = v[rows]
        mixed = bs_ref[...]
        for g in range(SGU_GROUPS):
            mixed = jnp.where(group == g, mixed + _mm(ws_ref[g], vc), mixed)
        o_ref[rows, :] = (_gelu_tanh(zu_ref[rows, :].astype(_F32)) * mixed).astype(o_ref.dtype)


def _sgu(z, ln_g, ln_b, w_s, b_tile):
    batch = z.shape[0]
    first = Z_SGU // SGU_DIM
    vec = pl.BlockSpec((1, SGU_DIM), lambda b, i: (0, 0))
    return pl.pallas_call(
        _sgu_kernel,
        grid=(batch, N_ROW_TILES),
        in_specs=[
            pl.BlockSpec((None, ROW_TILE, SGU_DIM), lambda b, i: (b, i, first)),
            pl.BlockSpec((None, ROW_TILE, SGU_DIM), lambda b, i: (b, i, first + 1)),
            vec, vec,
            pl.BlockSpec((SGU_GROUPS, SGU_CHUNK, SGU_CHUNK), lambda b, i: (0, 0, 0)),
            pl.BlockSpec((SGU_CHUNK, SGU_DIM), lambda b, i: (0, 0)),
        ],
        out_specs=pl.BlockSpec((None, ROW_TILE, SGU_DIM), lambda b, i: (b, i, 0)),
        out_shape=jax.ShapeDtypeStruct((batch, T_ALL, SGU_DIM), _BF),
        compiler_params=_params("arbitrary", "arbitrary"),
        name="sgu",
    )(z, z, ln_g, ln_b, w_s, b_tile)


def _merge_kernel(x_ref, yf_ref, yb_ref, r_ref, k_ref, v_ref, gz_ref, cv_ref, c_ref,
                  g0_ref, g1_ref, g2_ref, gate_ref, gup_ref, gng_ref, gnb_ref, rk_ref, lng_ref, lnb_ref,
                  wro_ref, wco_ref, wso_ref, wm_ref, o_ref, a_ref):
    _rwkv_readout(yf_ref, yb_ref, r_ref, k_ref, v_ref, gz_ref, gup_ref, gng_ref, gnb_ref, rk_ref, a_ref)
    a = _mm(a_ref[...], wro_ref[...])
    cb = _silu(_standardize(cv_ref[...], LN_EPS) * lng_ref[...] + lnb_ref[...])
    b = _mm(cb, wco_ref[...])
    c = _mm(c_ref[...], wso_ref[...])
    m = (a * _sigmoid(g0_ref[...].astype(_F32)) + b * _sigmoid(g1_ref[...].astype(_F32))
         + c * _sigmoid(g2_ref[...].astype(_F32)))
    o_ref[...] = x_ref[...] + gate_ref[...] * _mm(m, wm_ref[...])


def _merge(x_all, y_fwd, y_bwd, zr, conv, c_pre, z, gate1, g_up, gn_g, gn_b, r_k, ln_g, ln_b,
           w_ro, w_co, w_so, w_m):
    batch = x_all.shape[0]

    def rows(width, col=0):
        return pl.BlockSpec((None, ROW_TILE, width), lambda b, i: (b, i, col))

    def group(idx):
        return pl.BlockSpec((None, HEAD_PAIRS, ROW_TILE, LANES), lambda b, i: (b, idx, i, 0))

    def whole(shape):
        return pl.BlockSpec(shape, lambda b, i: (0, 0))

    vec = whole((1, D_MODEL))
    return pl.pallas_call(
        _merge_kernel,
        grid=(batch, N_ROW_TILES),
        in_specs=[
            rows(D_MODEL), group(0), group(0), group(0), group(1), group(2),
            pl.BlockSpec((None, None, ROW_TILE, LANES), lambda b, i: (b, TILE_G, i, 0)),
            rows(CONV_DIM), rows(SGU_DIM),
            rows(D_MODEL, 0), rows(D_MODEL, 1), rows(D_MODEL, 2),
            _mod_spec(2), whole((GATE_LORA, D_MODEL)), vec, vec, vec,
            whole((1, CONV_DIM)), whole((1, CONV_DIM)),
            whole((D_MODEL, D_MODEL)), whole((CONV_DIM, D_MODEL)), whole((SGU_DIM, D_MODEL)),
            whole((D_MODEL, D_MODEL)),
        ],
        out_specs=rows(D_MODEL),
        out_shape=jax.ShapeDtypeStruct((batch, T_ALL, D_MODEL), _F32),
        scratch_shapes=[pltpu.VMEM((ROW_TILE, D_MODEL), _BF)],
        compiler_params=_params("arbitrary", "arbitrary"),
        name="merge",
    )(x_all, y_fwd, y_bwd, zr, zr, zr, zr, conv, c_pre, z, z, z, gate1, g_up, gn_g, gn_b, r_k,
      ln_g, ln_b, w_ro, w_co, w_so, w_m)


def _ffn_kernel(last, x_ref, g_ref, sh_ref, sc_ref, gate_ref, win_ref, wout_ref, fg_ref, o_ref):
    x = x_ref[...]
    h = _rms_mod(x, g_ref[...], sh_ref[...], sc_ref[...]).astype(_BF)
    acc = jnp.zeros((ROW_TILE, D_MODEL), _F32)
    for f in range(D_FF // FF_CHUNK):
        cols = slice(f * FF_CHUNK, (f + 1) * FF_CHUNK)
        up_cols = slice(D_FF + f * FF_CHUNK, D_FF + (f + 1) * FF_CHUNK)
        act = _silu(_mm(h, win_ref[:, cols])) * _mm(h, win_ref[:, up_cols])
        acc = acc + _mm(act, wout_ref[cols, :])
    out = x + gate_ref[...] * acc
    if last:
        out = out * lax.rsqrt(jnp.mean(out * out, -1, keepdims=True) + NORM_EPS) * fg_ref[...]
    o_ref[...] = out


def _ffn(x_all, g, shift, scale, gate2, w_in, w_out, final_g, last):
    batch = x_all.shape[0]
    rows = pl.BlockSpec((None, ROW_TILE, D_MODEL), lambda b, i: (b, i, 0))
    vec = pl.BlockSpec((1, D_MODEL), lambda b, i: (0, 0))
    return pl.pallas_call(
        functools.partial(_ffn_kernel, last),
        grid=(batch, LAT_ROW_TILES if last else N_ROW_TILES),
        in_specs=[
            rows, vec,
            _mod_spec(2), _mod_spec(2), _mod_spec(2),
            pl.BlockSpec((D_MODEL, 2 * D_FF), lambda b, i: (0, 0)),
            pl.BlockSpec((D_FF, D_MODEL), lambda b, i: (0, 0)),
            vec,
        ],
        out_specs=rows,
        out_shape=jax.ShapeDtypeStruct((batch, SEQ if last else T_ALL, D_MODEL), _F32),
        compiler_params=_params("arbitrary", "arbitrary"),
        name="swiglu_final" if last else "swiglu",
    )(x_all, g, shift, scale, gate2, w_in, w_out, final_g)


def _permute_w_in(w):
    off_conv = RWKV_COLS
    off_sgu = off_conv + 2 * CONV_DIM
    off_gate = off_sgu + 2 * SGU_DIM
    pad = jnp.zeros((D_MODEL, Z_CONV - Z_RWKV - RWKV_COLS), w.dtype)
    return jnp.concatenate([w[:, off_gate:], w[:, :off_conv], pad, w[:, off_conv:off_sgu],
                            w[:, off_sgu:off_gate]], axis=1)


def _direction_padded(w_up):
    zero = jnp.zeros_like(w_up[0])
    return jnp.stack([jnp.concatenate([w_up[0], zero], 0), jnp.concatenate([zero, w_up[1]], 0)])


def kernel(x, c, ctx, c_ctx, w_mod, b_mod, norm1_g, norm2_g, w_in, rwkv_shift, rwkv_w0, rwkv_w_up, rwkv_a0, rwkv_a_up, rwkv_g_up, rwkv_k_k, rwkv_k_a, rwkv_r_k, rwkv_gn_g, rwkv_gn_b, rwkv_out, conv_dw, conv_dw_b, conv_ln_g, conv_ln_b, conv_out, sgu_ln_g, sgu_ln_b, sgu_w, sgu_b, sgu_out, w_merge, ffn_w_in, ffn_w_out, final_norm_g):
    batch = x.shape[0]
    depth = w_mod.shape[0]
    assert x.shape[1:] == (SEQ, D_MODEL) and ctx.shape[1:] == (CTX_LEN, D_MODEL)

    rows = -(-(batch + 1) // 8) * 8
    cvec = jnp.zeros((rows, D_MODEL), _F32).at[:batch].set(c).at[batch].set(c_ctx)
    mod = _modulation(cvec, w_mod.astype(_BF), b_mod[:, None, :])
    mod_lat = mod[:, :batch].reshape(depth, batch, 6, D_MODEL)
    mod_ctx = jnp.broadcast_to(mod[:, batch].reshape(depth, 1, 6, D_MODEL), mod_lat.shape)
    mod_tab = jnp.stack([mod_lat, mod_ctx], axis=2).transpose(0, 3, 1, 2, 4)
    mod_tab = mod_tab.reshape(depth, 6, 2 * batch, 1, D_MODEL)

    idx = jnp.arange(SCAN_CHUNK)
    incl = jnp.stack([idx[:, None] >= idx[None, :], idx[:, None] <= idx[None, :]]).astype(_F32)
    strict = jnp.stack([idx[:, None] > idx[None, :], idx[:, None] < idx[None, :]]).astype(_F32)
    m_top = jnp.concatenate([jnp.concatenate([strict, -strict], 2),
                             jnp.concatenate([-strict, strict], 2)], 1)
    m_bot = jnp.concatenate([jnp.concatenate([incl, -incl], 2)] * 2, 1).astype(_BF)
    m_top = m_top.astype(_BF)
    m_incl = incl.astype(_BF)

    x_all = jnp.concatenate([x, ctx], axis=1)
    for l in range(depth):
        sh1, sc1, g1, sh2, sc2, g2 = (mod_tab[l, i] for i in range(6))
        z = _in_projection(x_all, norm1_g[l][None], sh1, sc1, _permute_w_in(w_in[l]).astype(_BF))
        zr = _token_shift(z, rwkv_shift[l])
        y_fwd, y_bwd = _wkv_scan(zr, _direction_padded(rwkv_w_up[l]).astype(_BF), rwkv_w0[l][:, None, :],
                      _direction_padded(rwkv_a_up[l]).astype(_BF), rwkv_a0[l][:, None, :],
                      rwkv_k_k[l][None], rwkv_k_a[l][None], m_incl, m_top, m_bot)
        conv = _conformer_conv(z, conv_dw[l], conv_dw_b[l][None], l % 2 == 0)
        b_tile = jnp.repeat(sgu_b[l].T, SGU_GROUP_DIM, axis=1)
        c_pre = _sgu(z, sgu_ln_g[l][None], sgu_ln_b[l][None], sgu_w[l].astype(_BF), b_tile)
        x_all = _merge(x_all, y_fwd, y_bwd, zr, conv, c_pre, z, g1, rwkv_g_up[l].astype(_BF),
                       rwkv_gn_g[l][None], rwkv_gn_b[l][None], rwkv_r_k[l][None],
                       conv_ln_g[l][None], conv_ln_b[l][None],
                       rwkv_out[l].astype(_BF), conv_out[l].astype(_BF), sgu_out[l].astype(_BF),
                       w_merge[l].astype(_BF))
        x_all = _ffn(x_all, norm2_g[l][None], sh2, sc2, g2, ffn_w_in[l].astype(_BF),
                     ffn_w_out[l].astype(_BF), final_norm_g[None], l == depth - 1)
    return x_all
```

```python
import functools
import itertools
import math

import jax
import jax.numpy as jnp
from jax import lax
from jax.experimental import pallas as pl
from jax.experimental.pallas import tpu as pltpu

D_MODEL = 1024
SEQ = 2048
CTX_LEN = 256
T_ALL = SEQ + CTX_LEN
GRID_W = 64
GRID_H = SEQ // GRID_W

HEAD_DIM = 64
HEADS = D_MODEL // HEAD_DIM
HEAD_PAIRS = HEADS // 2
DECAY_LORA = 64
ICLR_LORA = 64
GATE_LORA = 128
CONV_DIM = D_MODEL // 2
CONV_WIDTH = 31
CONV_HALF = CONV_WIDTH // 2
SGU_DIM = D_MODEL // 2
SGU_GROUPS = 8
SGU_GROUP_DIM = SGU_DIM // SGU_GROUPS
SGU_CHUNK = 128
D_FF = ((8 * D_MODEL // 3 + 255) // 256) * 256
NORM_EPS = 1e-6
LN_EPS = 1e-5
GN_EPS = 64e-5

RWKV_COLS = 3 * D_MODEL + 2 * DECAY_LORA + 2 * ICLR_LORA + GATE_LORA

LANES = 128
Z_GATE = 0
Z_RWKV = 3 * D_MODEL
Z_CONV = Z_RWKV + RWKV_COLS + 128
Z_SGU = Z_CONV + 2 * CONV_DIM
Z_COLS = Z_SGU + 2 * SGU_DIM
RWKV_TILES = RWKV_COLS // LANES
TILE_W = 3 * HEAD_PAIRS
TILE_A = TILE_W + 1
TILE_G = TILE_A + 1

ROW_TILE = 256
N_ROW_TILES = T_ALL // ROW_TILE
LAT_ROW_TILES = SEQ // ROW_TILE
SCAN_CHUNK = 64
N_CHUNKS = T_ALL // SCAN_CHUNK
LAT_CHUNKS = SEQ // SCAN_CHUNK
FF_CHUNK = 256

VMEM_LIMIT = 56 * 1024 * 1024

_BF = jnp.bfloat16
_F32 = jnp.float32


def _mm(a, b):
    return jnp.dot(a.astype(_BF), b.astype(_BF), preferred_element_type=_F32)


def _mm_f32(a, b):
    return jnp.dot(a, b, preferred_element_type=_F32, precision=lax.Precision.HIGHEST)


def _mm_nt(a, b):
    return lax.dot_general(a.astype(_BF), b.astype(_BF), (((1,), (1,)), ((), ())),
                           preferred_element_type=_F32)


def _mm_tn(a, b):
    return lax.dot_general(a.astype(_BF), b.astype(_BF), (((0,), (0,)), ((), ())),
                           preferred_element_type=_F32)


def _sigmoid(x):
    return jax.nn.sigmoid(x)


def _silu(x):
    return x * jax.nn.sigmoid(x)


def _gelu_tanh(x):
    return 0.5 * x * (1.0 + jnp.tanh(math.sqrt(2.0 / math.pi) * (x + 0.044715 * (x * x * x))))


def _standardize(x, eps):
    xc = x - jnp.mean(x, -1, keepdims=True)
    return xc * lax.rsqrt(jnp.mean(xc * xc, -1, keepdims=True) + eps)


def _rms_mod(x, g, shift, scale):
    y = x * lax.rsqrt(jnp.mean(x * x, -1, keepdims=True) + NORM_EPS) * g
    return y * (1.0 + scale) + shift


def _params(*sem):
    return pltpu.CompilerParams(dimension_semantics=sem, vmem_limit_bytes=VMEM_LIMIT)


def _mod_kernel(c_ref, w_ref, b_ref, o_ref):
    o_ref[...] = _mm(_silu(c_ref[...]), w_ref[...]) + b_ref[...]


def _modulation(cvec, w_mod, b_mod):
    depth = w_mod.shape[0]
    rows = cvec.shape[0]
    return pl.pallas_call(
        _mod_kernel,
        grid=(depth, 6),
        in_specs=[
            pl.BlockSpec((rows, D_MODEL), lambda l, j: (0, 0)),
            pl.BlockSpec((None, D_MODEL, D_MODEL), lambda l, j: (l, 0, j)),
            pl.BlockSpec((None, 1, D_MODEL), lambda l, j: (l, 0, j)),
        ],
        out_specs=pl.BlockSpec((None, rows, D_MODEL), lambda l, j: (l, 0, j)),
        out_shape=jax.ShapeDtypeStruct((depth, rows, 6 * D_MODEL), _F32),
        compiler_params=_params("arbitrary", "arbitrary"),
        name="modulation",
    )(cvec, w_mod, b_mod)


def _mod_spec(grid_rank):
    if grid_rank == 2:
        return pl.BlockSpec((None, 1, D_MODEL), lambda b, i: (2 * b + i // LAT_ROW_TILES, 0, 0))
    return pl.BlockSpec((None, 1, D_MODEL), lambda j, b, i: (2 * b + i // LAT_ROW_TILES, 0, 0))


def _inproj_kernel(x_ref, g_ref, sh_ref, sc_ref, w_ref, o_ref):
    h = _rms_mod(x_ref[...], g_ref[...], sh_ref[...], sc_ref[...])
    o_ref[...] = _mm(h, w_ref[...]).astype(o_ref.dtype)


def _in_projection(x_all, g, shift, scale, w):
    batch = x_all.shape[0]
    n_col = 2
    tn = Z_COLS // n_col
    return pl.pallas_call(
        _inproj_kernel,
        grid=(n_col, batch, N_ROW_TILES),
        in_specs=[
            pl.BlockSpec((None, ROW_TILE, D_MODEL), lambda j, b, i: (b, i, 0)),
            pl.BlockSpec((1, D_MODEL), lambda j, b, i: (0, 0)),
            _mod_spec(3),
            _mod_spec(3),
            pl.BlockSpec((D_MODEL, tn), lambda j, b, i: (0, j)),
        ],
        out_specs=pl.BlockSpec((None, ROW_TILE, tn), lambda j, b, i: (b, i, j)),
        out_shape=jax.ShapeDtypeStruct((batch, T_ALL, Z_COLS), _BF),
        compiler_params=_params("arbitrary", "arbitrary", "arbitrary"),
        name="in_projection",
    )(x_all, g, shift, scale, w)


SHIFT_TILES = 3


def _shift_kernel(z_ref, w_ref, o_ref):
    sub = 8
    for i in range(SHIFT_TILES):
        cols = slice(i * LANES, (i + 1) * LANES)
        z = z_ref[:, cols].astype(_F32)
        prev = pltpu.roll(z, 1, 0)
        nxt = pltpu.roll(z, T_ALL - 1, 0)
        w = w_ref[:, cols]
        o_ref[i] = prev * w[0:1] + z * w[1:2] + nxt * w[2:3]
        for g0 in (0, SEQ - sub, SEQ, T_ALL - sub):
            t = g0 + lax.broadcasted_iota(jnp.int32, (sub, LANES), 0)
            rows = slice(g0, g0 + sub)
            p = jnp.where((t == 0) | (t == SEQ), 0.0, prev[rows])
            n = jnp.where((t == SEQ - 1) | (t == T_ALL - 1), 0.0, nxt[rows])
            o_ref[i, rows, :] = p * w[0:1] + z[rows] * w[1:2] + n * w[2:3]


def _token_shift(z, w_shift):
    batch = z.shape[0]
    width = SHIFT_TILES * LANES
    assert Z_RWKV % width == 0 and RWKV_TILES % SHIFT_TILES == 0
    first = Z_RWKV // width
    return pl.pallas_call(
        _shift_kernel,
        grid=(batch, RWKV_TILES // SHIFT_TILES),
        in_specs=[
            pl.BlockSpec((None, T_ALL, width), lambda b, j: (b, 0, first + j)),
            pl.BlockSpec((3, width), lambda b, j: (0, j)),
        ],
        out_specs=pl.BlockSpec((None, SHIFT_TILES, T_ALL, LANES), lambda b, j: (b, j, 0, 0)),
        out_shape=jax.ShapeDtypeStruct((batch, RWKV_TILES, T_ALL, LANES), _F32),
        compiler_params=_params("arbitrary", "arbitrary"),
        name="token_shift",
    )(z, w_shift)


def _head_sum(x):
    lane = lax.broadcasted_iota(jnp.int32, x.shape, x.ndim - 1)
    lo = jnp.sum(jnp.where(lane < HEAD_DIM, x, 0.0), -1, keepdims=True)
    tot = jnp.sum(x, -1, keepdims=True)
    return jnp.where(lane < HEAD_DIM, lo, tot - lo)


def _bmm(a, b):
    return lax.dot_general(a.astype(_BF), b.astype(_BF), (((2,), (1,)), ((0,), (0,))),
                           preferred_element_type=_F32)


def _bmm_nt(a, b):
    return lax.dot_general(a.astype(_BF), b.astype(_BF), (((2,), (2,)), ((0,), (0,))),
                           preferred_element_type=_F32)


def _bmm_tn(a, b):
    return lax.dot_general(a.astype(_BF), b.astype(_BF), (((1,), (1,)), ((0,), (0,))),
                           preferred_element_type=_F32)


def _lane_tiles(x):
    return jnp.stack([x[:, i * LANES:(i + 1) * LANES] for i in range(HEAD_PAIRS)])


N_SLOT_REFS = 5


def _scan_kernel(*refs):
    tokens = (refs[0:5], refs[5:10])
    wup_ref, w0_ref, aup_ref, a0_ref, kk_ref, ka_ref, mi_ref, mt_ref, mb_ref = refs[10:19]
    y_refs = refs[19:21]
    s_ref = refs[21]
    slots = (refs[22:22 + N_SLOT_REFS], refs[22 + N_SLOT_REFS:])
    step = pl.program_id(1)

    @pl.when(step == 0)
    def _():
        s_ref[...] = jnp.zeros_like(s_ref)
        for ref in slots[1]:
            ref[...] = jnp.zeros_like(ref)

    def run(prep, solve):
        pieces = itertools.chain.from_iterable(
            _scan_prepare(tokens[d], (wup_ref.at[d], w0_ref.at[d], aup_ref.at[d], a0_ref.at[d],
                                      kk_ref, ka_ref, mi_ref.at[d]), slots[prep], d)
            for d in range(2))
        next(pieces)
        for _ in _scan_solve(slots[solve], mt_ref, mb_ref, y_refs, s_ref):
            next(pieces, None)
        for _ in pieces:
            pass

    @pl.when(step % 2 == 0)
    def _():
        run(0, 1)

    @pl.when(step % 2 == 1)
    def _():
        run(1, 0)


PREP_PAIRS = 8
PREP_PIECES = HEAD_PAIRS // PREP_PAIRS


def _scan_prepare(token_refs, param_refs, slot, d):
    r_ref, k_ref, v_ref, lw_ref, la_ref = token_refs
    wup_ref, w0_ref, aup_ref, a0_ref, kk_ref, ka_ref, mi_ref = param_refs
    lhs_ref, rk_ref, vv_ref, ke_ref, dt_ref = slot
    width = PREP_PAIRS * LANES

    def tiles(x):
        return jnp.stack([x[:, i * LANES:(i + 1) * LANES] for i in range(PREP_PAIRS)])

    lw_act = jnp.tanh(lw_ref[...])
    la_in = la_ref[...]
    m_incl = mi_ref[...]
    for g in range(PREP_PIECES):
        cols = pl.ds(g * width, width)
        pairs = pl.ds(g * PREP_PAIRS, PREP_PAIRS)
        xw = w0_ref[:, cols] + _mm(lw_act, wup_ref[:, cols])
        ld = -(math.exp(-0.5) * math.log2(math.e)) * _sigmoid(xw)
        a = tiles(_sigmoid(a0_ref[:, cols] + _mm(la_in, aup_ref[:, cols])))
        hi = ld.astype(_BF)
        rem = ld - hi.astype(_F32)
        mid = rem.astype(_BF)
        lo = (rem - mid.astype(_F32)).astype(_BF)
        cl = (jnp.dot(m_incl, hi, preferred_element_type=_F32)
              + jnp.dot(m_incl, mid, preferred_element_type=_F32)
              + jnp.dot(m_incl, lo, preferred_element_type=_F32))
        ld_tot = jnp.sum(ld, 0, keepdims=True)
        dec_tot = jnp.exp2(ld_tot)
        e_in = tiles(jnp.exp2(cl))
        e_out = tiles(jnp.exp2(-cl))
        e_ex = tiles(jnp.exp2(cl - ld))
        e_end = tiles(jnp.exp2(ld_tot - cl))

        r = r_ref[pairs]
        k = k_ref[pairs]
        kk = k * tiles(kk_ref[:, cols])
        kk = kk * lax.rsqrt(jnp.maximum(_head_sum(kk * kk), 1e-24))
        k_d = k * (1.0 + (a - 1.0) * tiles(ka_ref[:, cols]))
        kka = kk * a

        rows = pl.ds(d * HEAD_PAIRS + g * PREP_PAIRS, PREP_PAIRS)
        lhs_ref[rows] = jnp.concatenate([kk * e_ex, r * e_in], axis=1).astype(_BF)
        rk_ref[rows] = jnp.swapaxes(jnp.concatenate([k_d * e_out, kka * e_out], axis=1),
                                    1, 2).astype(_BF)
        vv_ref[rows] = v_ref[pairs].astype(_BF)
        ke_ref[rows] = jnp.concatenate([k_d * e_end, -(kka * e_end)], axis=1).astype(_BF)
        dt_ref[rows] = tiles(dec_tot)
        yield


def _by_direction(x, m_ref):
    return jnp.concatenate([x[:HEAD_PAIRS] * m_ref[0], x[HEAD_PAIRS:] * m_ref[1]], axis=0)


def _scan_solve(slot, mt_ref, mb_ref, y_refs, s_ref):
    c = SCAN_CHUNK
    lhs_ref, rk_ref, vv_ref, ke_ref, dt_ref = slot
    lhs = lhs_ref[...]
    rk = rk_ref[...]
    v = vv_ref[...]
    zero = jnp.zeros((), _BF)
    even = lax.broadcasted_iota(jnp.int32, (1, 1, LANES), 2) < HEAD_DIM
    kkt, rt = lhs[:, :c], lhs[:, c:]
    main = _bmm(jnp.concatenate([jnp.where(even, kkt, zero), jnp.where(even, rt, zero),
                                 jnp.where(even, zero, rt), jnp.where(even, zero, kkt)], axis=1), rk)
    yield
    sbd = s_ref[...]
    ls = _bmm(lhs, jnp.swapaxes(sbd, 1, 2))
    yield
    top_odd = pltpu.roll(main[:, 3 * c:], HEAD_DIM, 2)
    main = main[:, :3 * c].astype(_BF)
    top = _by_direction(jnp.concatenate([main[:, :c], top_odd.astype(_BF)], axis=1), mt_ref)
    bot = _by_direction(main[:, c:], mb_ref)
    row_even = lax.broadcasted_iota(jnp.int32, (1, 2 * c, LANES), 1) < c
    is_x = row_even == even
    vv = jnp.concatenate([v, v], axis=1)
    out = _bmm(jnp.where(is_x, top, zero), vv)
    yield
    x = jnp.concatenate([ls[:, :c], ls[:, :c]], axis=1) + out
    zb = jnp.where(is_x, x.astype(_BF), top)
    n_steps = int(math.log2(c))
    for i in range(n_steps):
        out = _bmm(jnp.where(is_x, zero, zb), jnp.concatenate([zb[:, c:], zb[:, :c]], axis=1))
        yield
        x = x + out
        if i + 1 < n_steps:
            zb = jnp.where(is_x, x.astype(_BF), out.astype(_BF))
    u = jnp.where(even, x[:, :c], x[:, c:])
    vu = jnp.concatenate([v, u.astype(_BF)], axis=1)
    yy = _bmm(bot, vu)
    upd = _bmm_tn(vu, ke_ref[...])
    yield
    y = ls[:, c:] + jnp.where(even, yy[:, :c], yy[:, c:])
    y_refs[0][...] = y[:HEAD_PAIRS]
    y_refs[1][...] = y[HEAD_PAIRS:]
    blk_row = lax.broadcasted_iota(jnp.int32, (1, LANES, LANES), 1) < HEAD_DIM
    blk_col = lax.broadcasted_iota(jnp.int32, (1, LANES, LANES), 2) < HEAD_DIM
    s_ref[...] = jnp.where(blk_row == blk_col, sbd * dt_ref[...] + upd, 0.0)


def _scan_chunk_index(d, s):
    return (s + LAT_CHUNKS) % N_CHUNKS if d == 0 else N_CHUNKS - 1 - s


def _wkv_scan(zr, wup, w0, aup, a0, k_k, k_a, m_incl, m_top, m_bot):
    batch = zr.shape[0]
    c = SCAN_CHUNK

    def prep_chunk(d, s):
        return _scan_chunk_index(d, jnp.minimum(s, N_CHUNKS - 1))

    def solve_chunk(d, s):
        return _scan_chunk_index(d, jnp.maximum(s - 1, 0))

    def tokens(d):
        def group(idx):
            return pl.BlockSpec((None, HEAD_PAIRS, c, LANES),
                                lambda b, s: (b, idx, prep_chunk(d, s), 0))

        def lora_tile(idx):
            return pl.BlockSpec((None, None, c, LANES), lambda b, s: (b, idx, prep_chunk(d, s), 0))

        return [group(0), group(1), group(2), lora_tile(TILE_W), lora_tile(TILE_A)]

    def whole(*shape):
        return pl.BlockSpec(shape, lambda b, s: (0,) * len(shape))

    def y_spec(d):
        return pl.BlockSpec((None, HEAD_PAIRS, c, LANES), lambda b, s: (b, 0, solve_chunk(d, s), 0))

    both = 2 * HEAD_PAIRS
    slot = [pltpu.VMEM((both, 2 * c, LANES), _BF), pltpu.VMEM((both, 2 * c, LANES), _BF),
            pltpu.VMEM((both, c, LANES), _BF), pltpu.VMEM((both, 2 * c, LANES), _BF),
            pltpu.VMEM((both, 1, LANES), _F32)]
    assert len(slot) == N_SLOT_REFS
    y_shape = jax.ShapeDtypeStruct((batch, HEAD_PAIRS, T_ALL, LANES), _F32)
    return pl.pallas_call(
        _scan_kernel,
        grid=(batch, N_CHUNKS + 1),
        in_specs=tokens(0) + tokens(1) + [
            whole(2, LANES, D_MODEL), whole(2, 1, D_MODEL),
            whole(2, LANES, D_MODEL), whole(2, 1, D_MODEL),
            whole(1, D_MODEL), whole(1, D_MODEL),
            whole(2, c, c), whole(2, 2 * c, 2 * c), whole(2, 2 * c, 2 * c),
        ],
        out_specs=[y_spec(0), y_spec(1)],
        out_shape=[y_shape, y_shape],
        scratch_shapes=[pltpu.VMEM((both, LANES, LANES), _F32)] + slot * 2,
        compiler_params=_params("arbitrary", "arbitrary"),
        name="wkv_scan",
    )(*([zr] * 10), wup, w0, aup, a0, k_k, k_a, m_incl, m_top, m_bot)


READOUT_PAIRS = 2


def _rwkv_readout(yf_ref, yb_ref, r_ref, k_ref, v_ref, gz_ref, gup_ref, gng_ref, gnb_ref, rk_ref,
                  wro_ref):
    gate = _mm(_sigmoid(gz_ref[...]), gup_ref[...])
    acc = jnp.zeros((ROW_TILE, D_MODEL), _F32)
    for piece in range(HEAD_PAIRS // READOUT_PAIRS):
        parts = []
        for hp in range(piece * READOUT_PAIRS, (piece + 1) * READOUT_PAIRS):
            cols = slice(hp * LANES, (hp + 1) * LANES)
            y = yf_ref[hp] + yb_ref[hp]
            mean = _head_sum(y) * (1.0 / HEAD_DIM)
            yc = y - mean
            var = _head_sum(yc * yc) * (1.0 / HEAD_DIM)
            yn = yc * lax.rsqrt(var + GN_EPS) * gng_ref[:, cols] + gnb_ref[:, cols]
            bonus = _head_sum(r_ref[hp] * k_ref[hp] * rk_ref[:, cols]) * v_ref[hp]
            parts.append(((yn + bonus) * gate[:, cols]).astype(_BF))
        rows = slice(piece * READOUT_PAIRS * LANES, (piece + 1) * READOUT_PAIRS * LANES)
        acc = acc + _mm(jnp.concatenate(parts, axis=1), wro_ref[rows, :])
    return acc


_H_SLOT = GRID_W + 16
_H_LEAD = 16
_V_PAD = CONV_HALF * GRID_W
_CTX_LEAD = 16


def _conv_taps(pad_ref, w, base, length, stride):
    acc = None
    for j in range(CONV_WIDTH):
        term = pad_ref[pl.ds(base + (j - CONV_HALF) * stride, length), :] * w[j:j + 1]
        acc = term if acc is None else acc + term
    return acc


def _glu(zv_ref, zg_ref, rows):
    return zv_ref[rows, :].astype(_F32) * _sigmoid(zg_ref[rows, :].astype(_F32))


def _conv_kernel(horizontal, zv_ref, zg_ref, w_ref, b_ref, o_ref, lat_ref, ctx_ref):
    w = w_ref[...]
    bias = b_ref[...]
    lat_ref[...] = jnp.zeros_like(lat_ref)
    ctx_ref[...] = jnp.zeros_like(ctx_ref)
    ctx_ref[pl.ds(_CTX_LEAD, CTX_LEN), :] = (
        _glu(zv_ref, zg_ref, pl.ds(SEQ, CTX_LEN)))
    o_ref[pl.ds(SEQ, CTX_LEN), :] = _conv_taps(ctx_ref, w, _CTX_LEAD, CTX_LEN, 1) + bias
    if horizontal:
        for row in range(GRID_H):
            src = pl.ds(row * GRID_W, GRID_W)
            lat_ref[pl.ds(_H_LEAD + row * _H_SLOT, GRID_W), :] = (
                _glu(zv_ref, zg_ref, src))
        for row in range(GRID_H):
            o_ref[pl.ds(row * GRID_W, GRID_W), :] = (
                _conv_taps(lat_ref, w, _H_LEAD + row * _H_SLOT, GRID_W, 1) + bias)
    else:
        lat_ref[pl.ds(_V_PAD, SEQ), :] = (
            _glu(zv_ref, zg_ref, pl.ds(0, SEQ)))
        blk = 4 * GRID_W
        for i in range(SEQ // blk):
            o_ref[pl.ds(i * blk, blk), :] = _conv_taps(lat_ref, w, _V_PAD + i * blk, blk, GRID_W) + bias


def _conformer_conv(z, dw, dw_b, horizontal):
    batch = z.shape[0]
    first = Z_CONV // LANES
    n_tiles = CONV_DIM // LANES
    lat_rows = (_H_LEAD + GRID_H * _H_SLOT) if horizontal else (SEQ + 2 * _V_PAD)
    return pl.pallas_call(
        functools.partial(_conv_kernel, horizontal),
        grid=(batch, n_tiles),
        in_specs=[
            pl.BlockSpec((None, T_ALL, LANES), lambda b, j: (b, 0, first + j)),
            pl.BlockSpec((None, T_ALL, LANES), lambda b, j: (b, 0, first + n_tiles + j)),
            pl.BlockSpec((CONV_WIDTH, LANES), lambda b, j: (0, j)),
            pl.BlockSpec((1, LANES), lambda b, j: (0, j)),
        ],
        out_specs=pl.BlockSpec((None, T_ALL, LANES), lambda b, j: (b, 0, j)),
        out_shape=jax.ShapeDtypeStruct((batch, T_ALL, CONV_DIM), _F32),
        scratch_shapes=[pltpu.VMEM((lat_rows, LANES), _F32),
                        pltpu.VMEM((CTX_LEN + 2 * _CTX_LEAD, LANES), _F32)],
        compiler_params=_params("arbitrary", "arbitrary"),
        name="conformer_conv_h" if horizontal else "conformer_conv_v",
    )(z, z, dw, dw_b)


def _sgu_kernel(zu_ref, zv_ref, lng_ref, lnb_ref, ws_ref, bs_ref, o_ref):
    v = _standardize(_gelu_tanh(zv_ref[...].astype(_F32)), LN_EPS) * lng_ref[...] + lnb_ref[...]
    group = lax.broadcasted_iota(jnp.int32, (SGU_CHUNK, SGU_DIM), 1) // SGU_GROUP_DIM
    for n in range(ROW_TILE // SGU_CHUNK):
        rows = slice(n * SGU_CHUNK, (n + 1) * SGU_CHUNK)
        vc = v[rows]
        mixed = bs_ref[...]
        for g in range(SGU_GROUPS):
            mixed = jnp.where(group == g, mixed + _mm(ws_ref[g], vc), mixed)
        o_ref[rows, :] = (_gelu_tanh(zu_ref[rows, :].astype(_F32)) * mixed).astype(o_ref.dtype)


def _sgu(z, ln_g, ln_b, w_s, b_tile):
    batch = z.shape[0]
    first = Z_SGU // SGU_DIM
    vec = pl.BlockSpec((1, SGU_DIM), lambda b, i: (0, 0))
    return pl.pallas_call(
        _sgu_kernel,
        grid=(batch, N_ROW_TILES),
        in_specs=[
            pl.BlockSpec((None, ROW_TILE, SGU_DIM), lambda b, i: (b, i, first)),
            pl.BlockSpec((None, ROW_TILE, SGU_DIM), lambda b, i: (b, i, first + 1)),
            vec, vec,
            pl.BlockSpec((SGU_GROUPS, SGU_CHUNK, SGU_CHUNK), lambda b, i: (0, 0, 0)),
            pl.BlockSpec((SGU_CHUNK, SGU_DIM), lambda b, i: (0, 0)),
        ],
        out_specs=pl.BlockSpec((None, ROW_TILE, SGU_DIM), lambda b, i: (b, i, 0)),
        out_shape=jax.ShapeDtypeStruct((batch, T_ALL, SGU_DIM), _BF),
        compiler_params=_params("arbitrary", "arbitrary"),
        name="sgu",
    )(z, z, ln_g, ln_b, w_s, b_tile)


def _merge_kernel(x_ref, yf_ref, yb_ref, r_ref, k_ref, v_ref, gz_ref, cv_ref, c_ref,
                  g0_ref, g1_ref, g2_ref, gate_ref, gup_ref, gng_ref, gnb_ref, rk_ref, lng_ref, lnb_ref,
                  wro_ref, wco_ref, wso_ref, wm_ref, o_ref):
    c = _mm(c_ref[...], wso_ref[...])
    cb = _silu(_standardize(cv_ref[...], LN_EPS) * lng_ref[...] + lnb_ref[...])
    b = _mm(cb, wco_ref[...])
    a = _rwkv_readout(yf_ref, yb_ref, r_ref, k_ref, v_ref, gz_ref, gup_ref, gng_ref, gnb_ref, rk_ref,
                      wro_ref)
    m = (a * _sigmoid(g0_ref[...].astype(_F32)) + b * _sigmoid(g1_ref[...].astype(_F32))
         + c * _sigmoid(g2_ref[...].astype(_F32)))
    o_ref[...] = x_ref[...] + gate_ref[...] * _mm(m, wm_ref[...])


def _merge(x_all, y_fwd, y_bwd, zr, conv, c_pre, z, gate1, g_up, gn_g, gn_b, r_k, ln_g, ln_b,
           w_ro, w_co, w_so, w_m):
    batch = x_all.shape[0]

    def rows(width, col=0):
        return pl.BlockSpec((None, ROW_TILE, width), lambda b, i: (b, i, col))

    def group(idx):
        return pl.BlockSpec((None, HEAD_PAIRS, ROW_TILE, LANES), lambda b, i: (b, idx, i, 0))

    def whole(shape):
        return pl.BlockSpec(shape, lambda b, i: (0, 0))

    vec = whole((1, D_MODEL))
    return pl.pallas_call(
        _merge_kernel,
        grid=(batch, N_ROW_TILES),
        in_specs=[
            rows(D_MODEL), group(0), group(0), group(0), group(1), group(2),
            pl.BlockSpec((None, None, ROW_TILE, LANES), lambda b, i: (b, TILE_G, i, 0)),
            rows(CONV_DIM), rows(SGU_DIM),
            rows(D_MODEL, 0), rows(D_MODEL, 1), rows(D_MODEL, 2),
            _mod_spec(2), whole((GATE_LORA, D_MODEL)), vec, vec, vec,
            whole((1, CONV_DIM)), whole((1, CONV_DIM)),
            whole((D_MODEL, D_MODEL)), whole((CONV_DIM, D_MODEL)), whole((SGU_DIM, D_MODEL)),
            whole((D_MODEL, D_MODEL)),
        ],
        out_specs=rows(D_MODEL),
        out_shape=jax.ShapeDtypeStruct((batch, T_ALL, D_MODEL), _F32),
        compiler_params=_params("arbitrary", "arbitrary"),
        name="merge",
    )(x_all, y_fwd, y_bwd, zr, zr, zr, zr, conv, c_pre, z, z, z, gate1, g_up, gn_g, gn_b, r_k,
      ln_g, ln_b, w_ro, w_co, w_so, w_m)


def _ffn_kernel(last, x_ref, g_ref, sh_ref, sc_ref, gate_ref, win_ref, wout_ref, fg_ref, o_ref):
    x = x_ref[...]
    h = _rms_mod(x, g_ref[...], sh_ref[...], sc_ref[...]).astype(_BF)
    def gate_up(f):
        cols = slice(f * FF_CHUNK, (f + 1) * FF_CHUNK)
        up_cols = slice(D_FF + f * FF_CHUNK, D_FF + (f + 1) * FF_CHUNK)
        return _mm(h, win_ref[:, cols]), _mm(h, win_ref[:, up_cols])

    n_chunks = D_FF // FF_CHUNK
    acc = jnp.zeros((ROW_TILE, D_MODEL), _F32)
    pending = gate_up(0)
    for f in range(n_chunks):
        g, u = pending
        if f + 1 < n_chunks:
            pending = gate_up(f + 1)
        acc = acc + _mm(_silu(g) * u, wout_ref[f * FF_CHUNK:(f + 1) * FF_CHUNK, :])
    out = x + gate_ref[...] * acc
    if last:
        out = out * lax.rsqrt(jnp.mean(out * out, -1, keepdims=True) + NORM_EPS) * fg_ref[...]
    o_ref[...] = out


def _ffn(x_all, g, shift, scale, gate2, w_in, w_out, final_g, last):
    batch = x_all.shape[0]
    rows = pl.BlockSpec((None, ROW_TILE, D_MODEL), lambda b, i: (b, i, 0))
    vec = pl.BlockSpec((1, D_MODEL), lambda b, i: (0, 0))
    return pl.pallas_call(
        functools.partial(_ffn_kernel, last),
        grid=(batch, LAT_ROW_TILES if last else N_ROW_TILES),
        in_specs=[
            rows, vec,
            _mod_spec(2), _mod_spec(2), _mod_spec(2),
            pl.BlockSpec((D_MODEL, 2 * D_FF), lambda b, i: (0, 0)),
            pl.BlockSpec((D_FF, D_MODEL), lambda b, i: (0, 0)),
            vec,
        ],
        out_specs=rows,
        out_shape=jax.ShapeDtypeStruct((batch, SEQ if last else T_ALL, D_MODEL), _F32),
        compiler_params=_params("arbitrary", "arbitrary"),
        name="swiglu_final" if last else "swiglu",
    )(x_all, g, shift, scale, gate2, w_in, w_out, final_g)


def _permute_w_in(w):
    off_conv = RWKV_COLS
    off_sgu = off_conv + 2 * CONV_DIM
    off_gate = off_sgu + 2 * SGU_DIM
    pad = jnp.zeros((D_MODEL, Z_CONV - Z_RWKV - RWKV_COLS), w.dtype)
    return jnp.concatenate([w[:, off_gate:], w[:, :off_conv], pad, w[:, off_conv:off_sgu],
                            w[:, off_sgu:off_gate]], axis=1)


def _direction_padded(w_up):
    zero = jnp.zeros_like(w_up[0])
    return jnp.stack([jnp.concatenate([w_up[0], zero], 0), jnp.concatenate([zero, w_up[1]], 0)])


def kernel(x, c, ctx, c_ctx, w_mod, b_mod, norm1_g, norm2_g, w_in, rwkv_shift, rwkv_w0, rwkv_w_up, rwkv_a0, rwkv_a_up, rwkv_g_up, rwkv_k_k, rwkv_k_a, rwkv_r_k, rwkv_gn_g, rwkv_gn_b, rwkv_out, conv_dw, conv_dw_b, conv_ln_g, conv_ln_b, conv_out, sgu_ln_g, sgu_ln_b, sgu_w, sgu_b, sgu_out, w_merge, ffn_w_in, ffn_w_out, final_norm_g):
    batch = x.shape[0]
    depth = w_mod.shape[0]
    assert x.shape[1:] == (SEQ, D_MODEL) and ctx.shape[1:] == (CTX_LEN, D_MODEL)

    rows = -(-(batch + 1) // 8) * 8
    cvec = jnp.zeros((rows, D_MODEL), _F32).at[:batch].set(c).at[batch].set(c_ctx)
    mod = _modulation(cvec, w_mod.astype(_BF), b_mod[:, None, :])
    mod_lat = mod[:, :batch].reshape(depth, batch, 6, D_MODEL)
    mod_ctx = jnp.broadcast_to(mod[:, batch].reshape(depth, 1, 6, D_MODEL), mod_lat.shape)
    mod_tab = jnp.stack([mod_lat, mod_ctx], axis=2).transpose(0, 3, 1, 2, 4)
    mod_tab = mod_tab.reshape(depth, 6, 2 * batch, 1, D_MODEL)

    idx = jnp.arange(SCAN_CHUNK)
    incl = jnp.stack([idx[:, None] >= idx[None, :], idx[:, None] <= idx[None, :]]).astype(_F32)
    strict = jnp.stack([idx[:, None] > idx[None, :], idx[:, None] < idx[None, :]]).astype(_F32)
    m_top = jnp.concatenate([jnp.concatenate([strict, -strict], 2),
                             jnp.concatenate([-strict, strict], 2)], 1)
    m_bot = jnp.concatenate([jnp.concatenate([incl, -incl], 2)] * 2, 1).astype(_BF)
    m_top = m_top.astype(_BF)
    m_incl = incl.astype(_BF)

    x_all = jnp.concatenate([x, ctx], axis=1)
    for l in range(depth):
        sh1, sc1, g1, sh2, sc2, g2 = (mod_tab[l, i] for i in range(6))
        z = _in_projection(x_all, norm1_g[l][None], sh1, sc1, _permute_w_in(w_in[l]).astype(_BF))
        zr = _token_shift(z, rwkv_shift[l])
        y_fwd, y_bwd = _wkv_scan(zr, _direction_padded(rwkv_w_up[l]).astype(_BF), rwkv_w0[l][:, None, :],
                      _direction_padded(rwkv_a_up[l]).astype(_BF), rwkv_a0[l][:, None, :],
                      rwkv_k_k[l][None], rwkv_k_a[l][None], m_incl, m_top, m_bot)
        conv = _conformer_conv(z, conv_dw[l], conv_dw_b[l][None], l % 2 == 0)
        b_tile = jnp.repeat(sgu_b[l].T, SGU_GROUP_DIM, axis=1)
        c_pre = _sgu(z, sgu_ln_g[l][None], sgu_ln_b[l][None], sgu_w[l].astype(_BF), b_tile)
        x_all = _merge(x_all, y_fwd, y_bwd, zr, conv, c_pre, z, g1, rwkv_g_up[l].astype(_BF),
                       rwkv_gn_g[l][None], rwkv_gn_b[l][None], rwkv_r_k[l][None],
                       conv_ln_g[l][None], conv_ln_b[l][None],
                       rwkv_out[l].astype(_BF), conv_out[l].astype(_BF), sgu_out[l].astype(_BF),
                       w_merge[l].astype(_BF))
        x_all = _ffn(x_all, norm2_g[l][None], sh2, sc2, g2, ffn_w_in[l].astype(_BF),
                     ffn_w_out[l].astype(_BF), final_norm_g[None], l == depth - 1)
    return x_all
```

```python
import functools
import itertools
import math

import jax
import jax.numpy as jnp
from jax import lax
from jax.experimental import pallas as pl
from jax.experimental.pallas import tpu as pltpu

D_MODEL = 1024
SEQ = 2048
CTX_LEN = 256
T_ALL = SEQ + CTX_LEN
GRID_W = 64
GRID_H = SEQ // GRID_W

HEAD_DIM = 64
HEADS = D_MODEL // HEAD_DIM
HEAD_PAIRS = HEADS // 2
DECAY_LORA = 64
ICLR_LORA = 64
GATE_LORA = 128
CONV_DIM = D_MODEL // 2
CONV_WIDTH = 31
CONV_HALF = CONV_WIDTH // 2
SGU_DIM = D_MODEL // 2
SGU_GROUPS = 8
SGU_GROUP_DIM = SGU_DIM // SGU_GROUPS
SGU_CHUNK = 128
D_FF = ((8 * D_MODEL // 3 + 255) // 256) * 256
NORM_EPS = 1e-6
LN_EPS = 1e-5
GN_EPS = 64e-5

RWKV_COLS = 3 * D_MODEL + 2 * DECAY_LORA + 2 * ICLR_LORA + GATE_LORA

LANES = 128
Z_GATE = 0
Z_RWKV = 3 * D_MODEL
Z_CONV = Z_RWKV + RWKV_COLS + 128
Z_SGU = Z_CONV + 2 * CONV_DIM
Z_COLS = Z_SGU + 2 * SGU_DIM
RWKV_TILES = RWKV_COLS // LANES
TILE_W = 3 * HEAD_PAIRS
TILE_A = TILE_W + 1
TILE_G = TILE_A + 1

ROW_TILE = 256
N_ROW_TILES = T_ALL // ROW_TILE
LAT_ROW_TILES = SEQ // ROW_TILE
SCAN_CHUNK = 64
N_CHUNKS = T_ALL // SCAN_CHUNK
LAT_CHUNKS = SEQ // SCAN_CHUNK
FF_CHUNK = 256

VMEM_LIMIT = 56 * 1024 * 1024

_BF = jnp.bfloat16
_F32 = jnp.float32


def _mm(a, b):
    return jnp.dot(a.astype(_BF), b.astype(_BF), preferred_element_type=_F32)


def _mm_f32(a, b):
    return jnp.dot(a, b, preferred_element_type=_F32, precision=lax.Precision.HIGHEST)


def _mm_nt(a, b):
    return lax.dot_general(a.astype(_BF), b.astype(_BF), (((1,), (1,)), ((), ())),
                           preferred_element_type=_F32)


def _mm_tn(a, b):
    return lax.dot_general(a.astype(_BF), b.astype(_BF), (((0,), (0,)), ((), ())),
                           preferred_element_type=_F32)


def _sigmoid(x):
    return jax.nn.sigmoid(x)


def _silu(x):
    return x * jax.nn.sigmoid(x)


def _gelu_tanh(x):
    return 0.5 * x * (1.0 + jnp.tanh(math.sqrt(2.0 / math.pi) * (x + 0.044715 * (x * x * x))))


def _standardize(x, eps):
    xc = x - jnp.mean(x, -1, keepdims=True)
    return xc * lax.rsqrt(jnp.mean(xc * xc, -1, keepdims=True) + eps)


def _rms_mod(x, g, shift, scale):
    y = x * lax.rsqrt(jnp.mean(x * x, -1, keepdims=True) + NORM_EPS) * g
    return y * (1.0 + scale) + shift


def _params(*sem):
    return pltpu.CompilerParams(dimension_semantics=sem, vmem_limit_bytes=VMEM_LIMIT)


def _mod_kernel(c_ref, w_ref, b_ref, o_ref):
    o_ref[...] = _mm(_silu(c_ref[...]), w_ref[...]) + b_ref[...]


def _modulation(cvec, w_mod, b_mod):
    depth = w_mod.shape[0]
    rows = cvec.shape[0]
    return pl.pallas_call(
        _mod_kernel,
        grid=(depth, 6),
        in_specs=[
            pl.BlockSpec((rows, D_MODEL), lambda l, j: (0, 0)),
            pl.BlockSpec((None, D_MODEL, D_MODEL), lambda l, j: (l, 0, j)),
            pl.BlockSpec((None, 1, D_MODEL), lambda l, j: (l, 0, j)),
        ],
        out_specs=pl.BlockSpec((None, rows, D_MODEL), lambda l, j: (l, 0, j)),
        out_shape=jax.ShapeDtypeStruct((depth, rows, 6 * D_MODEL), _F32),
        compiler_params=_params("arbitrary", "arbitrary"),
        name="modulation",
    )(cvec, w_mod, b_mod)


def _mod_spec(grid_rank):
    if grid_rank == 2:
        return pl.BlockSpec((None, 1, D_MODEL), lambda b, i: (2 * b + i // LAT_ROW_TILES, 0, 0))
    return pl.BlockSpec((None, 1, D_MODEL), lambda j, b, i: (2 * b + i // LAT_ROW_TILES, 0, 0))


def _inproj_kernel(x_ref, g_ref, sh_ref, sc_ref, w_ref, o_ref):
    h = _rms_mod(x_ref[...], g_ref[...], sh_ref[...], sc_ref[...])
    o_ref[...] = _mm(h, w_ref[...]).astype(o_ref.dtype)


def _in_projection(x_all, g, shift, scale, w):
    batch = x_all.shape[0]
    n_col = 2
    tn = Z_COLS // n_col
    return pl.pallas_call(
        _inproj_kernel,
        grid=(n_col, batch, N_ROW_TILES),
        in_specs=[
            pl.BlockSpec((None, ROW_TILE, D_MODEL), lambda j, b, i: (b, i, 0)),
            pl.BlockSpec((1, D_MODEL), lambda j, b, i: (0, 0)),
            _mod_spec(3),
            _mod_spec(3),
            pl.BlockSpec((D_MODEL, tn), lambda j, b, i: (0, j)),
        ],
        out_specs=pl.BlockSpec((None, ROW_TILE, tn), lambda j, b, i: (b, i, j)),
        out_shape=jax.ShapeDtypeStruct((batch, T_ALL, Z_COLS), _BF),
        compiler_params=_params("arbitrary", "arbitrary", "arbitrary"),
        name="in_projection",
    )(x_all, g, shift, scale, w)


SHIFT_TILES = 3


def _shift_kernel(z_ref, w_ref, o_ref):
    sub = 16
    for i in range(SHIFT_TILES):
        cols = slice(i * LANES, (i + 1) * LANES)
        z = z_ref[:, cols].astype(_F32)
        prev = pltpu.roll(z, 1, 0)
        nxt = pltpu.roll(z, T_ALL - 1, 0)
        w = w_ref[:, cols]
        o_ref[i] = (prev * w[0:1] + z * w[1:2] + nxt * w[2:3]).astype(o_ref.dtype)
        for g0 in (0, SEQ - sub, SEQ, T_ALL - sub):
            t = g0 + lax.broadcasted_iota(jnp.int32, (sub, LANES), 0)
            rows = slice(g0, g0 + sub)
            p = jnp.where((t == 0) | (t == SEQ), 0.0, prev[rows])
            n = jnp.where((t == SEQ - 1) | (t == T_ALL - 1), 0.0, nxt[rows])
            o_ref[i, rows, :] = (p * w[0:1] + z[rows] * w[1:2] + n * w[2:3]).astype(o_ref.dtype)


def _token_shift(z, w_shift):
    batch = z.shape[0]
    width = SHIFT_TILES * LANES
    assert Z_RWKV % width == 0 and RWKV_TILES % SHIFT_TILES == 0
    first = Z_RWKV // width
    return pl.pallas_call(
        _shift_kernel,
        grid=(batch, RWKV_TILES // SHIFT_TILES),
        in_specs=[
            pl.BlockSpec((None, T_ALL, width), lambda b, j: (b, 0, first + j)),
            pl.BlockSpec((3, width), lambda b, j: (0, j)),
        ],
        out_specs=pl.BlockSpec((None, SHIFT_TILES, T_ALL, LANES), lambda b, j: (b, j, 0, 0)),
        out_shape=jax.ShapeDtypeStruct((batch, RWKV_TILES, T_ALL, LANES), _BF),
        compiler_params=_params("arbitrary", "arbitrary"),
        name="token_shift",
    )(z, w_shift)


def _head_sum(x):
    lane = lax.broadcasted_iota(jnp.int32, x.shape, x.ndim - 1)
    lo = jnp.sum(jnp.where(lane < HEAD_DIM, x, 0.0), -1, keepdims=True)
    tot = jnp.sum(x, -1, keepdims=True)
    return jnp.where(lane < HEAD_DIM, lo, tot - lo)


def _bmm(a, b):
    return lax.dot_general(a.astype(_BF), b.astype(_BF), (((2,), (1,)), ((0,), (0,))),
                           preferred_element_type=_F32)


def _bmm_nt(a, b):
    return lax.dot_general(a.astype(_BF), b.astype(_BF), (((2,), (2,)), ((0,), (0,))),
                           preferred_element_type=_F32)


def _bmm_tn(a, b):
    return lax.dot_general(a.astype(_BF), b.astype(_BF), (((1,), (1,)), ((0,), (0,))),
                           preferred_element_type=_F32)


def _lane_tiles(x):
    return jnp.stack([x[:, i * LANES:(i + 1) * LANES] for i in range(HEAD_PAIRS)])


N_SLOT_REFS = 5


def _scan_kernel(*refs):
    tokens = (refs[0:5], refs[5:10])
    wup_ref, w0_ref, aup_ref, a0_ref, kk_ref, ka_ref, mi_ref, mt_ref, mb_ref = refs[10:19]
    y_refs = refs[19:21]
    s_ref = refs[21]
    slots = (refs[22:22 + N_SLOT_REFS], refs[22 + N_SLOT_REFS:])
    step = pl.program_id(1)

    @pl.when(step == 0)
    def _():
        s_ref[...] = jnp.zeros_like(s_ref)
        for ref in slots[1]:
            ref[...] = jnp.zeros_like(ref)

    def run(prep, solve):
        pieces = itertools.chain.from_iterable(
            _scan_prepare(tokens[d], (wup_ref.at[d], w0_ref.at[d], aup_ref.at[d], a0_ref.at[d],
                                      kk_ref, ka_ref, mi_ref.at[d]), slots[prep], d)
            for d in range(2))
        next(pieces)
        for _ in _scan_solve(slots[solve], mt_ref, mb_ref, y_refs, s_ref):
            next(pieces, None)
        for _ in pieces:
            pass

    @pl.when(step % 2 == 0)
    def _():
        run(0, 1)

    @pl.when(step % 2 == 1)
    def _():
        run(1, 0)


PREP_PAIRS = 8
PREP_PIECES = HEAD_PAIRS // PREP_PAIRS


def _scan_prepare(token_refs, param_refs, slot, d):
    r_ref, k_ref, v_ref, lw_ref, la_ref = token_refs
    wup_ref, w0_ref, aup_ref, a0_ref, kk_ref, ka_ref, mi_ref = param_refs
    lhs_ref, rk_ref, vv_ref, ke_ref, dt_ref = slot
    width = PREP_PAIRS * LANES

    def tiles(x):
        return jnp.stack([x[:, i * LANES:(i + 1) * LANES] for i in range(PREP_PAIRS)])

    lw_act = jnp.tanh(lw_ref[...].astype(_F32))
    la_in = la_ref[...]
    m_incl = mi_ref[...]
    for g in range(PREP_PIECES):
        cols = pl.ds(g * width, width)
        pairs = pl.ds(g * PREP_PAIRS, PREP_PAIRS)
        xw = w0_ref[:, cols] + _mm(lw_act, wup_ref[:, cols])
        ld = -(math.exp(-0.5) * math.log2(math.e)) * _sigmoid(xw)
        a = tiles(_sigmoid(a0_ref[:, cols] + _mm(la_in, aup_ref[:, cols])))
        hi = ld.astype(_BF)
        rem = ld - hi.astype(_F32)
        mid = rem.astype(_BF)
        lo = (rem - mid.astype(_F32)).astype(_BF)
        cl = (jnp.dot(m_incl, hi, preferred_element_type=_F32)
              + jnp.dot(m_incl, mid, preferred_element_type=_F32)
              + jnp.dot(m_incl, lo, preferred_element_type=_F32))
        ld_tot = jnp.sum(ld, 0, keepdims=True)
        dec_tot = jnp.exp2(ld_tot)
        e_in = tiles(jnp.exp2(cl))
        e_out = tiles(jnp.exp2(-cl))
        e_ex = tiles(jnp.exp2(cl - ld))
        e_end = tiles(jnp.exp2(ld_tot - cl))

        r = r_ref[pairs].astype(_F32)
        k = k_ref[pairs].astype(_F32)
        kk = k * tiles(kk_ref[:, cols])
        kk = kk * lax.rsqrt(jnp.maximum(_head_sum(kk * kk), 1e-24))
        k_d = k * (1.0 + (a - 1.0) * tiles(ka_ref[:, cols]))
        kka = kk * a

        rows = pl.ds(d * HEAD_PAIRS + g * PREP_PAIRS, PREP_PAIRS)
        lhs_ref[rows] = jnp.concatenate([kk * e_ex, r * e_in], axis=1).astype(_BF)
        rk_ref[rows] = jnp.swapaxes(jnp.concatenate([k_d * e_out, kka * e_out], axis=1),
                                    1, 2).astype(_BF)
        vv_ref[rows] = v_ref[pairs]
        ke_ref[rows] = jnp.concatenate([k_d * e_end, -(kka * e_end)], axis=1).astype(_BF)
        dt_ref[rows] = tiles(dec_tot)
        yield


def _by_direction(x, m_ref):
    return jnp.concatenate([x[:HEAD_PAIRS] * m_ref[0], x[HEAD_PAIRS:] * m_ref[1]], axis=0)


def _scan_solve(slot, mt_ref, mb_ref, y_refs, s_ref):
    c = SCAN_CHUNK
    lhs_ref, rk_ref, vv_ref, ke_ref, dt_ref = slot
    lhs = lhs_ref[...]
    rk = rk_ref[...]
    v = vv_ref[...]
    zero = jnp.zeros((), _BF)
    even = lax.broadcasted_iota(jnp.int32, (1, 1, LANES), 2) < HEAD_DIM
    kkt, rt = lhs[:, :c], lhs[:, c:]
    main = _bmm(jnp.concatenate([jnp.where(even, kkt, zero), jnp.where(even, rt, zero),
                                 jnp.where(even, zero, rt), jnp.where(even, zero, kkt)], axis=1), rk)
    yield
    sbd = s_ref[...]
    ls = _bmm(lhs, jnp.swapaxes(sbd, 1, 2))
    yield
    top_odd = pltpu.roll(main[:, 3 * c:], HEAD_DIM, 2)
    main = main[:, :3 * c].astype(_BF)
    top = _by_direction(jnp.concatenate([main[:, :c], top_odd.astype(_BF)], axis=1), mt_ref)
    bot = _by_direction(main[:, c:], mb_ref)
    row_even = lax.broadcasted_iota(jnp.int32, (1, 2 * c, LANES), 1) < c
    is_x = row_even == even
    vv = jnp.concatenate([v, v], axis=1)
    out = _bmm(jnp.where(is_x, top, zero), vv)
    yield
    x = jnp.concatenate([ls[:, :c], ls[:, :c]], axis=1) + out
    zb = jnp.where(is_x, x.astype(_BF), top)
    n_steps = int(math.log2(c))
    for i in range(n_steps):
        out = _bmm(jnp.where(is_x, zero, zb), jnp.concatenate([zb[:, c:], zb[:, :c]], axis=1))
        yield
        x = x + out
        if i + 1 < n_steps:
            zb = jnp.where(is_x, x.astype(_BF), out.astype(_BF))
    u = jnp.where(even, x[:, :c], x[:, c:])
    vu = jnp.concatenate([v, u.astype(_BF)], axis=1)
    yy = _bmm(bot, vu)
    upd = _bmm_tn(vu, ke_ref[...])
    yield
    y = ls[:, c:] + jnp.where(even, yy[:, :c], yy[:, c:])
    y_refs[0][...] = y[:HEAD_PAIRS]
    y_refs[1][...] = y[HEAD_PAIRS:]
    blk_row = lax.broadcasted_iota(jnp.int32, (1, LANES, LANES), 1) < HEAD_DIM
    blk_col = lax.broadcasted_iota(jnp.int32, (1, LANES, LANES), 2) < HEAD_DIM
    s_ref[...] = jnp.where(blk_row == blk_col, sbd * dt_ref[...] + upd, 0.0)


def _scan_chunk_index(d, s):
    return (s + LAT_CHUNKS) % N_CHUNKS if d == 0 else N_CHUNKS - 1 - s


def _wkv_scan(zr, wup, w0, aup, a0, k_k, k_a, m_incl, m_top, m_bot):
    batch = zr.shape[0]
    c = SCAN_CHUNK

    def prep_chunk(d, s):
        return _scan_chunk_index(d, jnp.minimum(s, N_CHUNKS - 1))

    def solve_chunk(d, s):
        return _scan_chunk_index(d, jnp.maximum(s - 1, 0))

    def tokens(d):
        def group(idx):
            return pl.BlockSpec((None, HEAD_PAIRS, c, LANES),
                                lambda b, s: (b, idx, prep_chunk(d, s), 0))

        def lora_tile(idx):
            return pl.BlockSpec((None, None, c, LANES), lambda b, s: (b, idx, prep_chunk(d, s), 0))

        return [group(0), group(1), group(2), lora_tile(TILE_W), lora_tile(TILE_A)]

    def whole(*shape):
        return pl.BlockSpec(shape, lambda b, s: (0,) * len(shape))

    def y_spec(d):
        return pl.BlockSpec((None, HEAD_PAIRS, c, LANES), lambda b, s: (b, 0, solve_chunk(d, s), 0))

    both = 2 * HEAD_PAIRS
    slot = [pltpu.VMEM((both, 2 * c, LANES), _BF), pltpu.VMEM((both, 2 * c, LANES), _BF),
            pltpu.VMEM((both, c, LANES), _BF), pltpu.VMEM((both, 2 * c, LANES), _BF),
            pltpu.VMEM((both, 1, LANES), _F32)]
    assert len(slot) == N_SLOT_REFS
    y_shape = jax.ShapeDtypeStruct((batch, HEAD_PAIRS, T_ALL, LANES), _F32)
    return pl.pallas_call(
        _scan_kernel,
        grid=(batch, N_CHUNKS + 1),
        in_specs=tokens(0) + tokens(1) + [
            whole(2, LANES, D_MODEL), whole(2, 1, D_MODEL),
            whole(2, LANES, D_MODEL), whole(2, 1, D_MODEL),
            whole(1, D_MODEL), whole(1, D_MODEL),
            whole(2, c, c), whole(2, 2 * c, 2 * c), whole(2, 2 * c, 2 * c),
        ],
        out_specs=[y_spec(0), y_spec(1)],
        out_shape=[y_shape, y_shape],
        scratch_shapes=[pltpu.VMEM((both, LANES, LANES), _F32)] + slot * 2,
        compiler_params=_params("arbitrary", "arbitrary"),
        name="wkv_scan",
    )(*([zr] * 10), wup, w0, aup, a0, k_k, k_a, m_incl, m_top, m_bot)


READOUT_PAIRS = 2


def _rwkv_readout(yf_ref, yb_ref, r_ref, k_ref, v_ref, gz_ref, gup_ref, gng_ref, gnb_ref, rk_ref,
                  wro_ref):
    gate = _mm(_sigmoid(gz_ref[...].astype(_F32)), gup_ref[...])
    acc = jnp.zeros((ROW_TILE, D_MODEL), _F32)
    for piece in range(HEAD_PAIRS // READOUT_PAIRS):
        parts = []
        for hp in range(piece * READOUT_PAIRS, (piece + 1) * READOUT_PAIRS):
            cols = slice(hp * LANES, (hp + 1) * LANES)
            y = yf_ref[hp] + yb_ref[hp]
            mean = _head_sum(y) * (1.0 / HEAD_DIM)
            yc = y - mean
            var = _head_sum(yc * yc) * (1.0 / HEAD_DIM)
            yn = yc * lax.rsqrt(var + GN_EPS) * gng_ref[:, cols] + gnb_ref[:, cols]
            bonus = (_head_sum(r_ref[hp].astype(_F32) * k_ref[hp].astype(_F32) * rk_ref[:, cols])
                     * v_ref[hp].astype(_F32))
            parts.append(((yn + bonus) * gate[:, cols]).astype(_BF))
        rows = slice(piece * READOUT_PAIRS * LANES, (piece + 1) * READOUT_PAIRS * LANES)
        acc = acc + _mm(jnp.concatenate(parts, axis=1), wro_ref[rows, :])
    return acc


_H_SLOT = GRID_W + 16
_H_LEAD = 16
_V_PAD = CONV_HALF * GRID_W
_CTX_LEAD = 16


def _conv_taps(pad_ref, w, base, length, stride):
    acc = None
    for j in range(CONV_WIDTH):
        term = pad_ref[pl.ds(base + (j - CONV_HALF) * stride, length), :] * w[j:j + 1]
        acc = term if acc is None else acc + term
    return acc


def _glu(zv_ref, zg_ref, rows):
    return zv_ref[rows, :].astype(_F32) * _sigmoid(zg_ref[rows, :].astype(_F32))


def _conv_kernel(horizontal, zv_ref, zg_ref, w_ref, b_ref, o_ref, lat_ref, ctx_ref):
    w = w_ref[...]
    bias = b_ref[...]
    lat_ref[...] = jnp.zeros_like(lat_ref)
    ctx_ref[...] = jnp.zeros_like(ctx_ref)
    ctx_ref[pl.ds(_CTX_LEAD, CTX_LEN), :] = (
        _glu(zv_ref, zg_ref, pl.ds(SEQ, CTX_LEN)))
    o_ref[pl.ds(SEQ, CTX_LEN), :] = _conv_taps(ctx_ref, w, _CTX_LEAD, CTX_LEN, 1) + bias
    if horizontal:
        for row in range(GRID_H):
            src = pl.ds(row * GRID_W, GRID_W)
            lat_ref[pl.ds(_H_LEAD + row * _H_SLOT, GRID_W), :] = (
                _glu(zv_ref, zg_ref, src))
        for row in range(GRID_H):
            o_ref[pl.ds(row * GRID_W, GRID_W), :] = (
                _conv_taps(lat_ref, w, _H_LEAD + row * _H_SLOT, GRID_W, 1) + bias)
    else:
        lat_ref[pl.ds(_V_PAD, SEQ), :] = (
            _glu(zv_ref, zg_ref, pl.ds(0, SEQ)))
        blk = 4 * GRID_W
        for i in range(SEQ // blk):
            o_ref[pl.ds(i * blk, blk), :] = _conv_taps(lat_ref, w, _V_PAD + i * blk, blk, GRID_W) + bias


def _conformer_conv(z, dw, dw_b, horizontal):
    batch = z.shape[0]
    first = Z_CONV // LANES
    n_tiles = CONV_DIM // LANES
    lat_rows = (_H_LEAD + GRID_H * _H_SLOT) if horizontal else (SEQ + 2 * _V_PAD)
    return pl.pallas_call(
        functools.partial(_conv_kernel, horizontal),
        grid=(batch, n_tiles),
        in_specs=[
            pl.BlockSpec((None, T_ALL, LANES), lambda b, j: (b, 0, first + j)),
            pl.BlockSpec((None, T_ALL, LANES), lambda b, j: (b, 0, first + n_tiles + j)),
            pl.BlockSpec((CONV_WIDTH, LANES), lambda b, j: (0, j)),
            pl.BlockSpec((1, LANES), lambda b, j: (0, j)),
        ],
        out_specs=pl.BlockSpec((None, T_ALL, LANES), lambda b, j: (b, 0, j)),
        out_shape=jax.ShapeDtypeStruct((batch, T_ALL, CONV_DIM), _F32),
        scratch_shapes=[pltpu.VMEM((lat_rows, LANES), _F32),
                        pltpu.VMEM((CTX_LEN + 2 * _CTX_LEAD, LANES), _F32)],
        compiler_params=_params("arbitrary", "arbitrary"),
        name="conformer_conv_h" if horizontal else "conformer_conv_v",
    )(z, z, dw, dw_b)


def _sgu_kernel(zu_ref, zv_ref, lng_ref, lnb_ref, ws_ref, bs_ref, o_ref):
    v = _standardize(_gelu_tanh(zv_ref[...].astype(_F32)), LN_EPS) * lng_ref[...] + lnb_ref[...]
    v = v.astype(_BF)
    first_half = lax.broadcasted_iota(jnp.int32, (SGU_CHUNK, LANES), 1) < SGU_GROUP_DIM
    for n in range(ROW_TILE // SGU_CHUNK):
        rows = slice(n * SGU_CHUNK, (n + 1) * SGU_CHUNK)
        tiles = []
        for t in range(SGU_DIM // LANES):
            vt = v[rows, t * LANES:(t + 1) * LANES]
            tiles.append(jnp.where(first_half, _mm(ws_ref[2 * t], vt), _mm(ws_ref[2 * t + 1], vt)))
        mixed = jnp.concatenate(tiles, axis=1) + bs_ref[...]
        o_ref[rows, :] = (_gelu_tanh(zu_ref[rows, :].astype(_F32)) * mixed).astype(o_ref.dtype)


def _sgu(z, ln_g, ln_b, w_s, b_tile):
    batch = z.shape[0]
    first = Z_SGU // SGU_DIM
    vec = pl.BlockSpec((1, SGU_DIM), lambda b, i: (0, 0))
    return pl.pallas_call(
        _sgu_kernel,
        grid=(batch, N_ROW_TILES),
        in_specs=[
            pl.BlockSpec((None, ROW_TILE, SGU_DIM), lambda b, i: (b, i, first)),
            pl.BlockSpec((None, ROW_TILE, SGU_DIM), lambda b, i: (b, i, first + 1)),
            vec, vec,
            pl.BlockSpec((SGU_GROUPS, SGU_CHUNK, SGU_CHUNK), lambda b, i: (0, 0, 0)),
            pl.BlockSpec((SGU_CHUNK, SGU_DIM), lambda b, i: (0, 0)),
        ],
        out_specs=pl.BlockSpec((None, ROW_TILE, SGU_DIM), lambda b, i: (b, i, 0)),
        out_shape=jax.ShapeDtypeStruct((batch, T_ALL, SGU_DIM), _BF),
        compiler_params=_params("arbitrary", "arbitrary"),
        name="sgu",
    )(z, z, ln_g, ln_b, w_s, b_tile)


def _merge_kernel(x_ref, yf_ref, yb_ref, r_ref, k_ref, v_ref, gz_ref, cv_ref, c_ref,
                  g0_ref, g1_ref, g2_ref, gate_ref, gup_ref, gng_ref, gnb_ref, rk_ref, lng_ref, lnb_ref,
                  wro_ref, wco_ref, wso_ref, wm_ref, o_ref):
    c = _mm(c_ref[...], wso_ref[...])
    cb = _silu(_standardize(cv_ref[...], LN_EPS) * lng_ref[...] + lnb_ref[...])
    b = _mm(cb, wco_ref[...])
    a = _rwkv_readout(yf_ref, yb_ref, r_ref, k_ref, v_ref, gz_ref, gup_ref, gng_ref, gnb_ref, rk_ref,
                      wro_ref)
    m = (a * _sigmoid(g0_ref[...].astype(_F32)) + b * _sigmoid(g1_ref[...].astype(_F32))
         + c * _sigmoid(g2_ref[...].astype(_F32)))
    o_ref[...] = x_ref[...] + gate_ref[...] * _mm(m, wm_ref[...])


def _merge(x_all, y_fwd, y_bwd, zr, conv, c_pre, z, gate1, g_up, gn_g, gn_b, r_k, ln_g, ln_b,
           w_ro, w_co, w_so, w_m):
    batch = x_all.shape[0]

    def rows(width, col=0):
        return pl.BlockSpec((None, ROW_TILE, width), lambda b, i: (b, i, col))

    def group(idx):
        return pl.BlockSpec((None, HEAD_PAIRS, ROW_TILE, LANES), lambda b, i: (b, idx, i, 0))

    def whole(shape):
        return pl.BlockSpec(shape, lambda b, i: (0, 0))

    vec = whole((1, D_MODEL))
    return pl.pallas_call(
        _merge_kernel,
        grid=(batch, N_ROW_TILES),
        in_specs=[
            rows(D_MODEL), group(0), group(0), group(0), group(1), group(2),
            pl.BlockSpec((None, None, ROW_TILE, LANES), lambda b, i: (b, TILE_G, i, 0)),
            rows(CONV_DIM), rows(SGU_DIM),
            rows(D_MODEL, 0), rows(D_MODEL, 1), rows(D_MODEL, 2),
            _mod_spec(2), whole((GATE_LORA, D_MODEL)), vec, vec, vec,
            whole((1, CONV_DIM)), whole((1, CONV_DIM)),
            whole((D_MODEL, D_MODEL)), whole((CONV_DIM, D_MODEL)), whole((SGU_DIM, D_MODEL)),
            whole((D_MODEL, D_MODEL)),
        ],
        out_specs=rows(D_MODEL),
        out_shape=jax.ShapeDtypeStruct((batch, T_ALL, D_MODEL), _F32),
        compiler_params=_params("arbitrary", "arbitrary"),
        name="merge",
    )(x_all, y_fwd, y_bwd, zr, zr, zr, zr, conv, c_pre, z, z, z, gate1, g_up, gn_g, gn_b, r_k,
      ln_g, ln_b, w_ro, w_co, w_so, w_m)


def _ffn_kernel(last, x_ref, g_ref, sh_ref, sc_ref, gate_ref, win_ref, wout_ref, fg_ref, o_ref):
    x = x_ref[...]
    h = _rms_mod(x, g_ref[...], sh_ref[...], sc_ref[...]).astype(_BF)
    def gate_up(f):
        cols = slice(f * FF_CHUNK, (f + 1) * FF_CHUNK)
        up_cols = slice(D_FF + f * FF_CHUNK, D_FF + (f + 1) * FF_CHUNK)
        return _mm(h, win_ref[:, cols]), _mm(h, win_ref[:, up_cols])

    n_chunks = D_FF // FF_CHUNK
    acc = jnp.zeros((ROW_TILE, D_MODEL), _F32)
    pending = gate_up(0)
    for f in range(n_chunks):
        g, u = pending
        if f + 1 < n_chunks:
            pending = gate_up(f + 1)
        acc = acc + _mm(_silu(g) * u, wout_ref[f * FF_CHUNK:(f + 1) * FF_CHUNK, :])
    out = x + gate_ref[...] * acc
    if last:
        out = out * lax.rsqrt(jnp.mean(out * out, -1, keepdims=True) + NORM_EPS) * fg_ref[...]
    o_ref[...] = out


def _ffn(x_all, g, shift, scale, gate2, w_in, w_out, final_g, last):
    batch = x_all.shape[0]
    rows = pl.BlockSpec((None, ROW_TILE, D_MODEL), lambda b, i: (b, i, 0))
    vec = pl.BlockSpec((1, D_MODEL), lambda b, i: (0, 0))
    return pl.pallas_call(
        functools.partial(_ffn_kernel, last),
        grid=(batch, LAT_ROW_TILES if last else N_ROW_TILES),
        in_specs=[
            rows, vec,
            _mod_spec(2), _mod_spec(2), _mod_spec(2),
            pl.BlockSpec((D_MODEL, 2 * D_FF), lambda b, i: (0, 0)),
            pl.BlockSpec((D_FF, D_MODEL), lambda b, i: (0, 0)),
            vec,
        ],
        out_specs=rows,
        out_shape=jax.ShapeDtypeStruct((batch, SEQ if last else T_ALL, D_MODEL), _F32),
        compiler_params=_params("arbitrary", "arbitrary"),
        name="swiglu_final" if last else "swiglu",
    )(x_all, g, shift, scale, gate2, w_in, w_out, final_g)


def _permute_w_in(w):
    off_conv = RWKV_COLS
    off_sgu = off_conv + 2 * CONV_DIM
    off_gate = off_sgu + 2 * SGU_DIM
    pad = jnp.zeros((D_MODEL, Z_CONV - Z_RWKV - RWKV_COLS), w.dtype)
    return jnp.concatenate([w[:, off_gate:], w[:, :off_conv], pad, w[:, off_conv:off_sgu],
                            w[:, off_sgu:off_gate]], axis=1)


def _direction_padded(w_up):
    zero = jnp.zeros_like(w_up[0])
    return jnp.stack([jnp.concatenate([w_up[0], zero], 0), jnp.concatenate([zero, w_up[1]], 0)])


def kernel(x, c, ctx, c_ctx, w_mod, b_mod, norm1_g, norm2_g, w_in, rwkv_shift, rwkv_w0, rwkv_w_up, rwkv_a0, rwkv_a_up, rwkv_g_up, rwkv_k_k, rwkv_k_a, rwkv_r_k, rwkv_gn_g, rwkv_gn_b, rwkv_out, conv_dw, conv_dw_b, conv_ln_g, conv_ln_b, conv_out, sgu_ln_g, sgu_ln_b, sgu_w, sgu_b, sgu_out, w_merge, ffn_w_in, ffn_w_out, final_norm_g):
    batch = x.shape[0]
    depth = w_mod.shape[0]
    assert x.shape[1:] == (SEQ, D_MODEL) and ctx.shape[1:] == (CTX_LEN, D_MODEL)

    rows = -(-(batch + 1) // 8) * 8
    cvec = jnp.zeros((rows, D_MODEL), _F32).at[:batch].set(c).at[batch].set(c_ctx)
    mod = _modulation(cvec, w_mod.astype(_BF), b_mod[:, None, :])
    mod_lat = mod[:, :batch].reshape(depth, batch, 6, D_MODEL)
    mod_ctx = jnp.broadcast_to(mod[:, batch].reshape(depth, 1, 6, D_MODEL), mod_lat.shape)
    mod_tab = jnp.stack([mod_lat, mod_ctx], axis=2).transpose(0, 3, 1, 2, 4)
    mod_tab = mod_tab.reshape(depth, 6, 2 * batch, 1, D_MODEL)

    idx = jnp.arange(SCAN_CHUNK)
    incl = jnp.stack([idx[:, None] >= idx[None, :], idx[:, None] <= idx[None, :]]).astype(_F32)
    strict = jnp.stack([idx[:, None] > idx[None, :], idx[:, None] < idx[None, :]]).astype(_F32)
    m_top = jnp.concatenate([jnp.concatenate([strict, -strict], 2),
                             jnp.concatenate([-strict, strict], 2)], 1)
    m_bot = jnp.concatenate([jnp.concatenate([incl, -incl], 2)] * 2, 1).astype(_BF)
    m_top = m_top.astype(_BF)
    m_incl = incl.astype(_BF)

    x_all = jnp.concatenate([x, ctx], axis=1)
    for l in range(depth):
        sh1, sc1, g1, sh2, sc2, g2 = (mod_tab[l, i] for i in range(6))
        z = _in_projection(x_all, norm1_g[l][None], sh1, sc1, _permute_w_in(w_in[l]).astype(_BF))
        zr = _token_shift(z, rwkv_shift[l])
        y_fwd, y_bwd = _wkv_scan(zr, _direction_padded(rwkv_w_up[l]).astype(_BF), rwkv_w0[l][:, None, :],
                      _direction_padded(rwkv_a_up[l]).astype(_BF), rwkv_a0[l][:, None, :],
                      rwkv_k_k[l][None], rwkv_k_a[l][None], m_incl, m_top, m_bot)
        conv = _conformer_conv(z, conv_dw[l], conv_dw_b[l][None], l % 2 == 0)
        b_tile = jnp.repeat(sgu_b[l].T, SGU_GROUP_DIM, axis=1)
        c_pre = _sgu(z, sgu_ln_g[l][None], sgu_ln_b[l][None], sgu_w[l].astype(_BF), b_tile)
        x_all = _merge(x_all, y_fwd, y_bwd, zr, conv, c_pre, z, g1, rwkv_g_up[l].astype(_BF),
                       rwkv_gn_g[l][None], rwkv_gn_b[l][None], rwkv_r_k[l][None],
                       conv_ln_g[l][None], conv_ln_b[l][None],
                       rwkv_out[l].astype(_BF), conv_out[l].astype(_BF), sgu_out[l].astype(_BF),
                       w_merge[l].astype(_BF))
        x_all = _ffn(x_all, norm2_g[l][None], sh2, sc2, g2, ffn_w_in[l].astype(_BF),
                     ffn_w_out[l].astype(_BF), final_norm_g[None], l == depth - 1)
    return x_all
```

```python
import functools
import itertools
import math

import jax
import jax.numpy as jnp
from jax import lax
from jax.experimental import pallas as pl
from jax.experimental.pallas import tpu as pltpu

D_MODEL = 1024
SEQ = 2048
CTX_LEN = 256
T_ALL = SEQ + CTX_LEN
GRID_W = 64
GRID_H = SEQ // GRID_W

HEAD_DIM = 64
HEADS = D_MODEL // HEAD_DIM
HEAD_PAIRS = HEADS // 2
DECAY_LORA = 64
ICLR_LORA = 64
GATE_LORA = 128
CONV_DIM = D_MODEL // 2
CONV_WIDTH = 31
CONV_HALF = CONV_WIDTH // 2
SGU_DIM = D_MODEL // 2
SGU_GROUPS = 8
SGU_GROUP_DIM = SGU_DIM // SGU_GROUPS
SGU_CHUNK = 128
D_FF = ((8 * D_MODEL // 3 + 255) // 256) * 256
NORM_EPS = 1e-6
LN_EPS = 1e-5
GN_EPS = 64e-5

RWKV_COLS = 3 * D_MODEL + 2 * DECAY_LORA + 2 * ICLR_LORA + GATE_LORA

LANES = 128
Z_GATE = 0
Z_RWKV = 3 * D_MODEL
Z_CONV = Z_RWKV + RWKV_COLS + 128
Z_SGU = Z_CONV + 2 * CONV_DIM
Z_COLS = Z_SGU + 2 * SGU_DIM
RWKV_TILES = RWKV_COLS // LANES
TILE_W = 3 * HEAD_PAIRS
TILE_A = TILE_W + 1
TILE_G = TILE_A + 1

ROW_TILE = 256
N_ROW_TILES = T_ALL // ROW_TILE
LAT_ROW_TILES = SEQ // ROW_TILE
SCAN_CHUNK = 64
N_CHUNKS = T_ALL // SCAN_CHUNK
LAT_CHUNKS = SEQ // SCAN_CHUNK
FF_CHUNK = 256

VMEM_LIMIT = 56 * 1024 * 1024

_BF = jnp.bfloat16
_F32 = jnp.float32


def _mm(a, b):
    return jnp.dot(a.astype(_BF), b.astype(_BF), preferred_element_type=_F32)


def _mm_f32(a, b):
    return jnp.dot(a, b, preferred_element_type=_F32, precision=lax.Precision.HIGHEST)


def _mm_nt(a, b):
    return lax.dot_general(a.astype(_BF), b.astype(_BF), (((1,), (1,)), ((), ())),
                           preferred_element_type=_F32)


def _mm_tn(a, b):
    return lax.dot_general(a.astype(_BF), b.astype(_BF), (((0,), (0,)), ((), ())),
                           preferred_element_type=_F32)


def _sigmoid(x):
    return jax.nn.sigmoid(x)


def _silu(x):
    return x * jax.nn.sigmoid(x)


def _gelu_tanh(x):
    return 0.5 * x * (1.0 + jnp.tanh(math.sqrt(2.0 / math.pi) * (x + 0.044715 * (x * x * x))))


def _standardize(x, eps):
    xc = x - jnp.mean(x, -1, keepdims=True)
    return xc * lax.rsqrt(jnp.mean(xc * xc, -1, keepdims=True) + eps)


def _rms_mod(x, g, shift, scale):
    y = x * lax.rsqrt(jnp.mean(x * x, -1, keepdims=True) + NORM_EPS) * g
    return y * (1.0 + scale) + shift


def _params(*sem):
    return pltpu.CompilerParams(dimension_semantics=sem, vmem_limit_bytes=VMEM_LIMIT)


def _mod_kernel(c_ref, w_ref, b_ref, o_ref):
    o_ref[...] = _mm(_silu(c_ref[...]), w_ref[...]) + b_ref[...]


def _modulation(cvec, w_mod, b_mod):
    depth = w_mod.shape[0]
    rows = cvec.shape[0]
    return pl.pallas_call(
        _mod_kernel,
        grid=(depth, 6),
        in_specs=[
            pl.BlockSpec((rows, D_MODEL), lambda l, j: (0, 0)),
            pl.BlockSpec((None, D_MODEL, D_MODEL), lambda l, j: (l, 0, j)),
            pl.BlockSpec((None, 1, D_MODEL), lambda l, j: (l, 0, j)),
        ],
        out_specs=pl.BlockSpec((None, rows, D_MODEL), lambda l, j: (l, 0, j)),
        out_shape=jax.ShapeDtypeStruct((depth, rows, 6 * D_MODEL), _F32),
        compiler_params=_params("arbitrary", "arbitrary"),
        name="modulation",
    )(cvec, w_mod, b_mod)


def _mod_spec(grid_rank):
    if grid_rank == 2:
        return pl.BlockSpec((None, 1, D_MODEL), lambda b, i: (2 * b + i // LAT_ROW_TILES, 0, 0))
    return pl.BlockSpec((None, 1, D_MODEL), lambda j, b, i: (2 * b + i // LAT_ROW_TILES, 0, 0))


def _inproj_kernel(x_ref, g_ref, sh_ref, sc_ref, w_ref, o_ref):
    h = _rms_mod(x_ref[...], g_ref[...], sh_ref[...], sc_ref[...])
    o_ref[...] = _mm(h, w_ref[...]).astype(o_ref.dtype)


def _in_projection(x_all, g, shift, scale, w):
    batch = x_all.shape[0]
    n_col = 2
    tn = Z_COLS // n_col
    return pl.pallas_call(
        _inproj_kernel,
        grid=(n_col, batch, N_ROW_TILES),
        in_specs=[
            pl.BlockSpec((None, ROW_TILE, D_MODEL), lambda j, b, i: (b, i, 0)),
            pl.BlockSpec((1, D_MODEL), lambda j, b, i: (0, 0)),
            _mod_spec(3),
            _mod_spec(3),
            pl.BlockSpec((D_MODEL, tn), lambda j, b, i: (0, j)),
        ],
        out_specs=pl.BlockSpec((None, ROW_TILE, tn), lambda j, b, i: (b, i, j)),
        out_shape=jax.ShapeDtypeStruct((batch, T_ALL, Z_COLS), _BF),
        compiler_params=_params("arbitrary", "arbitrary", "arbitrary"),
        name="in_projection",
    )(x_all, g, shift, scale, w)


SHIFT_TILES = 3


def _shift_kernel(z_ref, w_ref, o_ref):
    sub = 16
    for i in range(SHIFT_TILES):
        cols = slice(i * LANES, (i + 1) * LANES)
        z = z_ref[:, cols].astype(_F32)
        prev = pltpu.roll(z, 1, 0)
        nxt = pltpu.roll(z, T_ALL - 1, 0)
        w = w_ref[:, cols]
        o_ref[i] = (prev * w[0:1] + z * w[1:2] + nxt * w[2:3]).astype(o_ref.dtype)
        for g0 in (0, SEQ - sub, SEQ, T_ALL - sub):
            t = g0 + lax.broadcasted_iota(jnp.int32, (sub, LANES), 0)
            rows = slice(g0, g0 + sub)
            p = jnp.where((t == 0) | (t == SEQ), 0.0, prev[rows])
            n = jnp.where((t == SEQ - 1) | (t == T_ALL - 1), 0.0, nxt[rows])
            o_ref[i, rows, :] = (p * w[0:1] + z[rows] * w[1:2] + n * w[2:3]).astype(o_ref.dtype)


def _token_shift(z, w_shift):
    batch = z.shape[0]
    width = SHIFT_TILES * LANES
    assert Z_RWKV % width == 0 and RWKV_TILES % SHIFT_TILES == 0
    first = Z_RWKV // width
    return pl.pallas_call(
        _shift_kernel,
        grid=(batch, RWKV_TILES // SHIFT_TILES),
        in_specs=[
            pl.BlockSpec((None, T_ALL, width), lambda b, j: (b, 0, first + j)),
            pl.BlockSpec((3, width), lambda b, j: (0, j)),
        ],
        out_specs=pl.BlockSpec((None, SHIFT_TILES, T_ALL, LANES), lambda b, j: (b, j, 0, 0)),
        out_shape=jax.ShapeDtypeStruct((batch, RWKV_TILES, T_ALL, LANES), _BF),
        compiler_params=_params("arbitrary", "arbitrary"),
        name="token_shift",
    )(z, w_shift)


def _head_sum(x):
    lane = lax.broadcasted_iota(jnp.int32, x.shape, x.ndim - 1)
    lo = jnp.sum(jnp.where(lane < HEAD_DIM, x, 0.0), -1, keepdims=True)
    tot = jnp.sum(x, -1, keepdims=True)
    return jnp.where(lane < HEAD_DIM, lo, tot - lo)


def _bmm(a, b):
    return lax.dot_general(a.astype(_BF), b.astype(_BF), (((2,), (1,)), ((0,), (0,))),
                           preferred_element_type=_F32)


def _bmm_nt(a, b):
    return lax.dot_general(a.astype(_BF), b.astype(_BF), (((2,), (2,)), ((0,), (0,))),
                           preferred_element_type=_F32)


def _bmm_tn(a, b):
    return lax.dot_general(a.astype(_BF), b.astype(_BF), (((1,), (1,)), ((0,), (0,))),
                           preferred_element_type=_F32)


def _lane_tiles(x):
    return jnp.stack([x[:, i * LANES:(i + 1) * LANES] for i in range(HEAD_PAIRS)])


N_SLOT_REFS = 5


def _scan_kernel(*refs):
    tokens = (refs[0:2], refs[2:4])
    wup_ref, w0_ref, aup_ref, a0_ref, kk_ref, ka_ref, mi_ref, mt_ref, mb_ref = refs[4:13]
    y_refs = refs[13:15]
    s_ref = refs[15]
    slots = (refs[16:16 + N_SLOT_REFS], refs[16 + N_SLOT_REFS:])
    step = pl.program_id(1)

    @pl.when(step == 0)
    def _():
        s_ref[...] = jnp.zeros_like(s_ref)
        for ref in slots[1]:
            ref[...] = jnp.zeros_like(ref)

    def run(prep, solve):
        pieces = itertools.chain.from_iterable(
            _scan_prepare(tokens[d], (wup_ref.at[d], w0_ref.at[d], aup_ref.at[d], a0_ref.at[d],
                                      kk_ref, ka_ref, mi_ref.at[d]), slots[prep], d)
            for d in range(2))
        next(pieces)
        for _ in _scan_solve(slots[solve], mt_ref, mb_ref, y_refs, s_ref):
            next(pieces, None)
        for _ in pieces:
            pass

    @pl.when(step % 2 == 0)
    def _():
        run(0, 1)

    @pl.when(step % 2 == 1)
    def _():
        run(1, 0)


PREP_PAIRS = 8
PREP_PIECES = HEAD_PAIRS // PREP_PAIRS


def _scan_prepare(token_refs, param_refs, slot, d):
    rkv_ref, lora_ref = token_refs
    r_ref, k_ref, v_ref = (rkv_ref.at[pl.ds(i * HEAD_PAIRS, HEAD_PAIRS)] for i in range(3))
    lw_ref, la_ref = lora_ref.at[0], lora_ref.at[1]
    wup_ref, w0_ref, aup_ref, a0_ref, kk_ref, ka_ref, mi_ref = param_refs
    lhs_ref, rk_ref, vv_ref, ke_ref, dt_ref = slot
    width = PREP_PAIRS * LANES

    def tiles(x):
        return jnp.stack([x[:, i * LANES:(i + 1) * LANES] for i in range(PREP_PAIRS)])

    lw_act = jnp.tanh(lw_ref[...].astype(_F32))
    la_in = la_ref[...]
    m_incl = mi_ref[...]
    for g in range(PREP_PIECES):
        cols = pl.ds(g * width, width)
        pairs = pl.ds(g * PREP_PAIRS, PREP_PAIRS)
        xw = w0_ref[:, cols] + _mm(lw_act, wup_ref[:, cols])
        ld = -(math.exp(-0.5) * math.log2(math.e)) * _sigmoid(xw)
        a = tiles(_sigmoid(a0_ref[:, cols] + _mm(la_in, aup_ref[:, cols])))
        hi = ld.astype(_BF)
        rem = ld - hi.astype(_F32)
        mid = rem.astype(_BF)
        lo = (rem - mid.astype(_F32)).astype(_BF)
        cl = (jnp.dot(m_incl, hi, preferred_element_type=_F32)
              + jnp.dot(m_incl, mid, preferred_element_type=_F32)
              + jnp.dot(m_incl, lo, preferred_element_type=_F32))
        ld_tot = jnp.sum(ld, 0, keepdims=True)
        dec_tot = jnp.exp2(ld_tot)
        e_in = tiles(jnp.exp2(cl))
        e_out = tiles(jnp.exp2(-cl))
        e_ex = tiles(jnp.exp2(cl - ld))
        e_end = tiles(jnp.exp2(ld_tot - cl))

        r = r_ref[pairs].astype(_F32)
        k = k_ref[pairs].astype(_F32)
        kk = k * tiles(kk_ref[:, cols])
        kk = kk * lax.rsqrt(jnp.maximum(_head_sum(kk * kk), 1e-24))
        k_d = k * (1.0 + (a - 1.0) * tiles(ka_ref[:, cols]))
        kka = kk * a

        rows = pl.ds(d * HEAD_PAIRS + g * PREP_PAIRS, PREP_PAIRS)
        lhs_ref[rows] = jnp.concatenate([kk * e_ex, r * e_in], axis=1).astype(_BF)
        rk_ref[rows] = jnp.swapaxes(jnp.concatenate([k_d * e_out, kka * e_out], axis=1),
                                    1, 2).astype(_BF)
        vv_ref[rows] = v_ref[pairs]
        ke_ref[rows] = jnp.concatenate([k_d * e_end, -(kka * e_end)], axis=1).astype(_BF)
        dt_ref[rows] = tiles(dec_tot)
        yield


def _by_direction(x, m_ref):
    return jnp.concatenate([x[:HEAD_PAIRS] * m_ref[0], x[HEAD_PAIRS:] * m_ref[1]], axis=0)


def _scan_solve(slot, mt_ref, mb_ref, y_refs, s_ref):
    c = SCAN_CHUNK
    lhs_ref, rk_ref, vv_ref, ke_ref, dt_ref = slot
    lhs = lhs_ref[...]
    rk = rk_ref[...]
    v = vv_ref[...]
    zero = jnp.zeros((), _BF)
    even = lax.broadcasted_iota(jnp.int32, (1, 1, LANES), 2) < HEAD_DIM
    kkt, rt = lhs[:, :c], lhs[:, c:]
    main = _bmm(jnp.concatenate([jnp.where(even, kkt, zero), jnp.where(even, rt, zero),
                                 jnp.where(even, zero, rt), jnp.where(even, zero, kkt)], axis=1), rk)
    yield
    sbd = s_ref[...]
    ls = _bmm(lhs, jnp.swapaxes(sbd, 1, 2))
    yield
    top_odd = pltpu.roll(main[:, 3 * c:], HEAD_DIM, 2)
    main = main[:, :3 * c].astype(_BF)
    top = _by_direction(jnp.concatenate([main[:, :c], top_odd.astype(_BF)], axis=1), mt_ref)
    bot = _by_direction(main[:, c:], mb_ref)
    row_even = lax.broadcasted_iota(jnp.int32, (1, 2 * c, LANES), 1) < c
    is_x = row_even == even
    vv = jnp.concatenate([v, v], axis=1)
    out = _bmm(jnp.where(is_x, top, zero), vv)
    yield
    x = jnp.concatenate([ls[:, :c], ls[:, :c]], axis=1) + out
    zb = jnp.where(is_x, x.astype(_BF), top)
    n_steps = int(math.log2(c))
    for i in range(n_steps):
        out = _bmm(jnp.where(is_x, zero, zb), jnp.concatenate([zb[:, c:], zb[:, :c]], axis=1))
        yield
        x = x + out
        if i + 1 < n_steps:
            zb = jnp.where(is_x, x.astype(_BF), out.astype(_BF))
    u = jnp.where(even, x[:, :c], x[:, c:])
    vu = jnp.concatenate([v, u.astype(_BF)], axis=1)
    yy = _bmm(bot, vu)
    upd = _bmm_tn(vu, ke_ref[...])
    yield
    y = ls[:, c:] + jnp.where(even, yy[:, :c], yy[:, c:])
    y_refs[0][...] = y[:HEAD_PAIRS]
    y_refs[1][...] = y[HEAD_PAIRS:]
    blk_row = lax.broadcasted_iota(jnp.int32, (1, LANES, LANES), 1) < HEAD_DIM
    blk_col = lax.broadcasted_iota(jnp.int32, (1, LANES, LANES), 2) < HEAD_DIM
    s_ref[...] = jnp.where(blk_row == blk_col, sbd * dt_ref[...] + upd, 0.0)


def _scan_chunk_index(d, s):
    return (s + LAT_CHUNKS) % N_CHUNKS if d == 0 else N_CHUNKS - 1 - s


def _wkv_scan(zr, wup, w0, aup, a0, k_k, k_a, m_incl, m_top, m_bot):
    batch = zr.shape[0]
    c = SCAN_CHUNK

    def prep_chunk(d, s):
        return _scan_chunk_index(d, jnp.minimum(s, N_CHUNKS - 1))

    def solve_chunk(d, s):
        return _scan_chunk_index(d, jnp.maximum(s - 1, 0))

    def tokens(d):
        assert TILE_W % 2 == 0 and TILE_A == TILE_W + 1
        return [pl.BlockSpec((None, TILE_W, c, LANES), lambda b, s: (b, 0, prep_chunk(d, s), 0)),
                pl.BlockSpec((None, 2, c, LANES), lambda b, s: (b, TILE_W // 2, prep_chunk(d, s), 0))]

    def whole(*shape):
        return pl.BlockSpec(shape, lambda b, s: (0,) * len(shape))

    def y_spec(d):
        return pl.BlockSpec((None, HEAD_PAIRS, c, LANES), lambda b, s: (b, 0, solve_chunk(d, s), 0))

    both = 2 * HEAD_PAIRS
    slot = [pltpu.VMEM((both, 2 * c, LANES), _BF), pltpu.VMEM((both, 2 * c, LANES), _BF),
            pltpu.VMEM((both, c, LANES), _BF), pltpu.VMEM((both, 2 * c, LANES), _BF),
            pltpu.VMEM((both, 1, LANES), _F32)]
    assert len(slot) == N_SLOT_REFS
    y_shape = jax.ShapeDtypeStruct((batch, HEAD_PAIRS, T_ALL, LANES), _F32)
    return pl.pallas_call(
        _scan_kernel,
        grid=(batch, N_CHUNKS + 1),
        in_specs=tokens(0) + tokens(1) + [
            whole(2, LANES, D_MODEL), whole(2, 1, D_MODEL),
            whole(2, LANES, D_MODEL), whole(2, 1, D_MODEL),
            whole(1, D_MODEL), whole(1, D_MODEL),
            whole(2, c, c), whole(2, 2 * c, 2 * c), whole(2, 2 * c, 2 * c),
        ],
        out_specs=[y_spec(0), y_spec(1)],
        out_shape=[y_shape, y_shape],
        scratch_shapes=[pltpu.VMEM((both, LANES, LANES), _F32)] + slot * 2,
        compiler_params=_params("arbitrary", "arbitrary"),
        name="wkv_scan",
    )(*([zr] * 4), wup, w0, aup, a0, k_k, k_a, m_incl, m_top, m_bot)


READOUT_PAIRS = 2


def _rwkv_readout(yf_ref, yb_ref, r_ref, k_ref, v_ref, gz_ref, gup_ref, gng_ref, gnb_ref, rk_ref,
                  wro_ref):
    gate = _mm(_sigmoid(gz_ref[...].astype(_F32)), gup_ref[...])
    acc = jnp.zeros((ROW_TILE, D_MODEL), _F32)
    for piece in range(HEAD_PAIRS // READOUT_PAIRS):
        parts = []
        for hp in range(piece * READOUT_PAIRS, (piece + 1) * READOUT_PAIRS):
            cols = slice(hp * LANES, (hp + 1) * LANES)
            y = yf_ref[hp] + yb_ref[hp]
            mean = _head_sum(y) * (1.0 / HEAD_DIM)
            yc = y - mean
            var = _head_sum(yc * yc) * (1.0 / HEAD_DIM)
            yn = yc * lax.rsqrt(var + GN_EPS) * gng_ref[:, cols] + gnb_ref[:, cols]
            bonus = (_head_sum(r_ref[hp].astype(_F32) * k_ref[hp].astype(_F32) * rk_ref[:, cols])
                     * v_ref[hp].astype(_F32))
            parts.append(((yn + bonus) * gate[:, cols]).astype(_BF))
        rows = slice(piece * READOUT_PAIRS * LANES, (piece + 1) * READOUT_PAIRS * LANES)
        acc = acc + _mm(jnp.concatenate(parts, axis=1), wro_ref[rows, :])
    return acc


_H_SLOT = GRID_W + 16
_H_LEAD = 16
_V_PAD = CONV_HALF * GRID_W
_CTX_LEAD = 16


def _conv_taps(pad_ref, w, base, length, stride):
    acc = None
    for j in range(CONV_WIDTH):
        term = pad_ref[pl.ds(base + (j - CONV_HALF) * stride, length), :] * w[j:j + 1]
        acc = term if acc is None else acc + term
    return acc


def _glu(zv_ref, zg_ref, rows):
    return zv_ref[rows, :].astype(_F32) * _sigmoid(zg_ref[rows, :].astype(_F32))


def _conv_kernel(horizontal, zv_ref, zg_ref, w_ref, b_ref, o_ref, lat_ref, ctx_ref):
    w = w_ref[...]
    bias = b_ref[...]
    lat_ref[...] = jnp.zeros_like(lat_ref)
    ctx_ref[...] = jnp.zeros_like(ctx_ref)
    ctx_ref[pl.ds(_CTX_LEAD, CTX_LEN), :] = (
        _glu(zv_ref, zg_ref, pl.ds(SEQ, CTX_LEN)))
    o_ref[pl.ds(SEQ, CTX_LEN), :] = _conv_taps(ctx_ref, w, _CTX_LEAD, CTX_LEN, 1) + bias
    if horizontal:
        for row in range(GRID_H):
            src = pl.ds(row * GRID_W, GRID_W)
            lat_ref[pl.ds(_H_LEAD + row * _H_SLOT, GRID_W), :] = (
                _glu(zv_ref, zg_ref, src))
        for row in range(GRID_H):
            o_ref[pl.ds(row * GRID_W, GRID_W), :] = (
                _conv_taps(lat_ref, w, _H_LEAD + row * _H_SLOT, GRID_W, 1) + bias)
    else:
        lat_ref[pl.ds(_V_PAD, SEQ), :] = (
            _glu(zv_ref, zg_ref, pl.ds(0, SEQ)))
        blk = 4 * GRID_W
        for i in range(SEQ // blk):
            o_ref[pl.ds(i * blk, blk), :] = _conv_taps(lat_ref, w, _V_PAD + i * blk, blk, GRID_W) + bias


def _conformer_conv(z, dw, dw_b, horizontal):
    batch = z.shape[0]
    first = Z_CONV // LANES
    n_tiles = CONV_DIM // LANES
    lat_rows = (_H_LEAD + GRID_H * _H_SLOT) if horizontal else (SEQ + 2 * _V_PAD)
    return pl.pallas_call(
        functools.partial(_conv_kernel, horizontal),
        grid=(batch, n_tiles),
        in_specs=[
            pl.BlockSpec((None, T_ALL, LANES), lambda b, j: (b, 0, first + j)),
            pl.BlockSpec((None, T_ALL, LANES), lambda b, j: (b, 0, first + n_tiles + j)),
            pl.BlockSpec((CONV_WIDTH, LANES), lambda b, j: (0, j)),
            pl.BlockSpec((1, LANES), lambda b, j: (0, j)),
        ],
        out_specs=pl.BlockSpec((None, T_ALL, LANES), lambda b, j: (b, 0, j)),
        out_shape=jax.ShapeDtypeStruct((batch, T_ALL, CONV_DIM), _F32),
        scratch_shapes=[pltpu.VMEM((lat_rows, LANES), _F32),
                        pltpu.VMEM((CTX_LEN + 2 * _CTX_LEAD, LANES), _F32)],
        compiler_params=_params("arbitrary", "arbitrary"),
        name="conformer_conv_h" if horizontal else "conformer_conv_v",
    )(z, z, dw, dw_b)


def _sgu_kernel(zu_ref, zv_ref, lng_ref, lnb_ref, ws_ref, bs_ref, o_ref):
    v = _standardize(_gelu_tanh(zv_ref[...].astype(_F32)), LN_EPS) * lng_ref[...] + lnb_ref[...]
    v = v.astype(_BF)
    first_half = lax.broadcasted_iota(jnp.int32, (SGU_CHUNK, LANES), 1) < SGU_GROUP_DIM
    for n in range(ROW_TILE // SGU_CHUNK):
        rows = slice(n * SGU_CHUNK, (n + 1) * SGU_CHUNK)
        tiles = []
        for t in range(SGU_DIM // LANES):
            vt = v[rows, t * LANES:(t + 1) * LANES]
            tiles.append(jnp.where(first_half, _mm(ws_ref[2 * t], vt), _mm(ws_ref[2 * t + 1], vt)))
        mixed = jnp.concatenate(tiles, axis=1) + bs_ref[...]
        o_ref[rows, :] = (_gelu_tanh(zu_ref[rows, :].astype(_F32)) * mixed).astype(o_ref.dtype)


def _sgu(z, ln_g, ln_b, w_s, b_tile):
    batch = z.shape[0]
    first = Z_SGU // SGU_DIM
    vec = pl.BlockSpec((1, SGU_DIM), lambda b, i: (0, 0))
    return pl.pallas_call(
        _sgu_kernel,
        grid=(batch, N_ROW_TILES),
        in_specs=[
            pl.BlockSpec((None, ROW_TILE, SGU_DIM), lambda b, i: (b, i, first)),
            pl.BlockSpec((None, ROW_TILE, SGU_DIM), lambda b, i: (b, i, first + 1)),
            vec, vec,
            pl.BlockSpec((SGU_GROUPS, SGU_CHUNK, SGU_CHUNK), lambda b, i: (0, 0, 0)),
            pl.BlockSpec((SGU_CHUNK, SGU_DIM), lambda b, i: (0, 0)),
        ],
        out_specs=pl.BlockSpec((None, ROW_TILE, SGU_DIM), lambda b, i: (b, i, 0)),
        out_shape=jax.ShapeDtypeStruct((batch, T_ALL, SGU_DIM), _BF),
        compiler_params=_params("arbitrary", "arbitrary"),
        name="sgu",
    )(z, z, ln_g, ln_b, w_s, b_tile)


def _merge_kernel(x_ref, yf_ref, yb_ref, r_ref, k_ref, v_ref, gz_ref, cv_ref, c_ref,
                  g0_ref, g1_ref, g2_ref, gate_ref, gup_ref, gng_ref, gnb_ref, rk_ref, lng_ref, lnb_ref,
                  wro_ref, wco_ref, wso_ref, wm_ref, o_ref):
    c = _mm(c_ref[...], wso_ref[...])
    cb = _silu(_standardize(cv_ref[...], LN_EPS) * lng_ref[...] + lnb_ref[...])
    b = _mm(cb, wco_ref[...])
    a = _rwkv_readout(yf_ref, yb_ref, r_ref, k_ref, v_ref, gz_ref, gup_ref, gng_ref, gnb_ref, rk_ref,
                      wro_ref)
    m = (a * _sigmoid(g0_ref[...].astype(_F32)) + b * _sigmoid(g1_ref[...].astype(_F32))
         + c * _sigmoid(g2_ref[...].astype(_F32)))
    o_ref[...] = x_ref[...] + gate_ref[...] * _mm(m, wm_ref[...])


def _merge(x_all, y_fwd, y_bwd, zr, conv, c_pre, z, gate1, g_up, gn_g, gn_b, r_k, ln_g, ln_b,
           w_ro, w_co, w_so, w_m):
    batch = x_all.shape[0]

    def rows(width, col=0):
        return pl.BlockSpec((None, ROW_TILE, width), lambda b, i: (b, i, col))

    def group(idx):
        return pl.BlockSpec((None, HEAD_PAIRS, ROW_TILE, LANES), lambda b, i: (b, idx, i, 0))

    def whole(shape):
        return pl.BlockSpec(shape, lambda b, i: (0, 0))

    vec = whole((1, D_MODEL))
    return pl.pallas_call(
        _merge_kernel,
        grid=(batch, N_ROW_TILES),
        in_specs=[
            rows(D_MODEL), group(0), group(0), group(0), group(1), group(2),
            pl.BlockSpec((None, None, ROW_TILE, LANES), lambda b, i: (b, TILE_G, i, 0)),
            rows(CONV_DIM), rows(SGU_DIM),
            rows(D_MODEL, 0), rows(D_MODEL, 1), rows(D_MODEL, 2),
            _mod_spec(2), whole((GATE_LORA, D_MODEL)), vec, vec, vec,
            whole((1, CONV_DIM)), whole((1, CONV_DIM)),
            whole((D_MODEL, D_MODEL)), whole((CONV_DIM, D_MODEL)), whole((SGU_DIM, D_MODEL)),
            whole((D_MODEL, D_MODEL)),
        ],
        out_specs=rows(D_MODEL),
        out_shape=jax.ShapeDtypeStruct((batch, T_ALL, D_MODEL), _F32),
        compiler_params=_params("arbitrary", "arbitrary"),
        name="merge",
    )(x_all, y_fwd, y_bwd, zr, zr, zr, zr, conv, c_pre, z, z, z, gate1, g_up, gn_g, gn_b, r_k,
      ln_g, ln_b, w_ro, w_co, w_so, w_m)


def _ffn_kernel(last, x_ref, g_ref, sh_ref, sc_ref, gate_ref, win_ref, wout_ref, fg_ref, o_ref):
    x = x_ref[...]
    h = _rms_mod(x, g_ref[...], sh_ref[...], sc_ref[...]).astype(_BF)
    def gate_up(f):
        cols = slice(f * FF_CHUNK, (f + 1) * FF_CHUNK)
        up_cols = slice(D_FF + f * FF_CHUNK, D_FF + (f + 1) * FF_CHUNK)
        return _mm(h, win_ref[:, cols]), _mm(h, win_ref[:, up_cols])

    n_chunks = D_FF // FF_CHUNK
    acc = jnp.zeros((ROW_TILE, D_MODEL), _F32)
    pending = gate_up(0)
    for f in range(n_chunks):
        g, u = pending
        if f + 1 < n_chunks:
            pending = gate_up(f + 1)
        acc = acc + _mm(_silu(g) * u, wout_ref[f * FF_CHUNK:(f + 1) * FF_CHUNK, :])
    out = x + gate_ref[...] * acc
    if last:
        out = out * lax.rsqrt(jnp.mean(out * out, -1, keepdims=True) + NORM_EPS) * fg_ref[...]
    o_ref[...] = out


def _ffn(x_all, g, shift, scale, gate2, w_in, w_out, final_g, last):
    batch = x_all.shape[0]
    rows = pl.BlockSpec((None, ROW_TILE, D_MODEL), lambda b, i: (b, i, 0))
    vec = pl.BlockSpec((1, D_MODEL), lambda b, i: (0, 0))
    return pl.pallas_call(
        functools.partial(_ffn_kernel, last),
        grid=(batch, LAT_ROW_TILES if last else N_ROW_TILES),
        in_specs=[
            rows, vec,
            _mod_spec(2), _mod_spec(2), _mod_spec(2),
            pl.BlockSpec((D_MODEL, 2 * D_FF), lambda b, i: (0, 0)),
            pl.BlockSpec((D_FF, D_MODEL), lambda b, i: (0, 0)),
            vec,
        ],
        out_specs=rows,
        out_shape=jax.ShapeDtypeStruct((batch, SEQ if last else T_ALL, D_MODEL), _F32),
        compiler_params=_params("arbitrary", "arbitrary"),
        name="swiglu_final" if last else "swiglu",
    )(x_all, g, shift, scale, gate2, w_in, w_out, final_g)


def _permute_w_in(w):
    off_conv = RWKV_COLS
    off_sgu = off_conv + 2 * CONV_DIM
    off_gate = off_sgu + 2 * SGU_DIM
    pad = jnp.zeros((D_MODEL, Z_CONV - Z_RWKV - RWKV_COLS), w.dtype)
    return jnp.concatenate([w[:, off_gate:], w[:, :off_conv], pad, w[:, off_conv:off_sgu],
                            w[:, off_sgu:off_gate]], axis=1)


def _direction_padded(w_up):
    zero = jnp.zeros_like(w_up[0])
    return jnp.stack([jnp.concatenate([w_up[0], zero], 0), jnp.concatenate([zero, w_up[1]], 0)])


def kernel(x, c, ctx, c_ctx, w_mod, b_mod, norm1_g, norm2_g, w_in, rwkv_shift, rwkv_w0, rwkv_w_up, rwkv_a0, rwkv_a_up, rwkv_g_up, rwkv_k_k, rwkv_k_a, rwkv_r_k, rwkv_gn_g, rwkv_gn_b, rwkv_out, conv_dw, conv_dw_b, conv_ln_g, conv_ln_b, conv_out, sgu_ln_g, sgu_ln_b, sgu_w, sgu_b, sgu_out, w_merge, ffn_w_in, ffn_w_out, final_norm_g):
    batch = x.shape[0]
    depth = w_mod.shape[0]
    assert x.shape[1:] == (SEQ, D_MODEL) and ctx.shape[1:] == (CTX_LEN, D_MODEL)

    rows = -(-(batch + 1) // 8) * 8
    cvec = jnp.zeros((rows, D_MODEL), _F32).at[:batch].set(c).at[batch].set(c_ctx)
    mod = _modulation(cvec, w_mod.astype(_BF), b_mod[:, None, :])
    mod_lat = mod[:, :batch].reshape(depth, batch, 6, D_MODEL)
    mod_ctx = jnp.broadcast_to(mod[:, batch].reshape(depth, 1, 6, D_MODEL), mod_lat.shape)
    mod_tab = jnp.stack([mod_lat, mod_ctx], axis=2).transpose(0, 3, 1, 2, 4)
    mod_tab = mod_tab.reshape(depth, 6, 2 * batch, 1, D_MODEL)

    idx = jnp.arange(SCAN_CHUNK)
    incl = jnp.stack([idx[:, None] >= idx[None, :], idx[:, None] <= idx[None, :]]).astype(_F32)
    strict = jnp.stack([idx[:, None] > idx[None, :], idx[:, None] < idx[None, :]]).astype(_F32)
    m_top = jnp.concatenate([jnp.concatenate([strict, -strict], 2),
                             jnp.concatenate([-strict, strict], 2)], 1)
    m_bot = jnp.concatenate([jnp.concatenate([incl, -incl], 2)] * 2, 1).astype(_BF)
    m_top = m_top.astype(_BF)
    m_incl = incl.astype(_BF)

    x_all = jnp.concatenate([x, ctx], axis=1)
    for l in range(depth):
        sh1, sc1, g1, sh2, sc2, g2 = (mod_tab[l, i] for i in range(6))
        z = _in_projection(x_all, norm1_g[l][None], sh1, sc1, _permute_w_in(w_in[l]).astype(_BF))
        zr = _token_shift(z, rwkv_shift[l])
        y_fwd, y_bwd = _wkv_scan(zr, _direction_padded(rwkv_w_up[l]).astype(_BF), rwkv_w0[l][:, None, :],
                      _direction_padded(rwkv_a_up[l]).astype(_BF), rwkv_a0[l][:, None, :],
                      rwkv_k_k[l][None], rwkv_k_a[l][None], m_incl, m_top, m_bot)
        conv = _conformer_conv(z, conv_dw[l], conv_dw_b[l][None], l % 2 == 0)
        b_tile = jnp.repeat(sgu_b[l].T, SGU_GROUP_DIM, axis=1)
        c_pre = _sgu(z, sgu_ln_g[l][None], sgu_ln_b[l][None], sgu_w[l].astype(_BF), b_tile)
        x_all = _merge(x_all, y_fwd, y_bwd, zr, conv, c_pre, z, g1, rwkv_g_up[l].astype(_BF),
                       rwkv_gn_g[l][None], rwkv_gn_b[l][None], rwkv_r_k[l][None],
                       conv_ln_g[l][None], conv_ln_b[l][None],
                       rwkv_out[l].astype(_BF), conv_out[l].astype(_BF), sgu_out[l].astype(_BF),
                       w_merge[l].astype(_BF))
        x_all = _ffn(x_all, norm2_g[l][None], sh2, sc2, g2, ffn_w_in[l].astype(_BF),
                     ffn_w_out[l].astype(_BF), final_norm_g[None], l == depth - 1)
    return x_all
```

```python
import functools
import itertools
import math

import jax
import jax.numpy as jnp
from jax import lax
from jax.experimental import pallas as pl
from jax.experimental.pallas import tpu as pltpu

D_MODEL = 1024
SEQ = 2048
CTX_LEN = 256
T_ALL = SEQ + CTX_LEN
GRID_W = 64
GRID_H = SEQ // GRID_W

HEAD_DIM = 64
HEADS = D_MODEL // HEAD_DIM
HEAD_PAIRS = HEADS // 2
DECAY_LORA = 64
ICLR_LORA = 64
GATE_LORA = 128
CONV_DIM = D_MODEL // 2
CONV_WIDTH = 31
CONV_HALF = CONV_WIDTH // 2
SGU_DIM = D_MODEL // 2
SGU_GROUPS = 8
SGU_GROUP_DIM = SGU_DIM // SGU_GROUPS
SGU_CHUNK = 128
D_FF = ((8 * D_MODEL // 3 + 255) // 256) * 256
NORM_EPS = 1e-6
LN_EPS = 1e-5
GN_EPS = 64e-5

RWKV_COLS = 3 * D_MODEL + 2 * DECAY_LORA + 2 * ICLR_LORA + GATE_LORA

LANES = 128
Z_GATE = 0
Z_CONV = 3 * D_MODEL
Z_SGU = Z_CONV + 2 * CONV_DIM
Z_COLS = Z_SGU + 2 * SGU_DIM
RWKV_TILES = RWKV_COLS // LANES
TILE_W = 3 * HEAD_PAIRS
TILE_A = TILE_W + 1
TILE_G = TILE_A + 1

ROW_TILE = 256
N_ROW_TILES = T_ALL // ROW_TILE
LAT_ROW_TILES = SEQ // ROW_TILE
SCAN_CHUNK = 64
N_CHUNKS = T_ALL // SCAN_CHUNK
LAT_CHUNKS = SEQ // SCAN_CHUNK
FF_CHUNK = 256

VMEM_LIMIT = 56 * 1024 * 1024

_BF = jnp.bfloat16
_F32 = jnp.float32


def _mm(a, b):
    return jnp.dot(a.astype(_BF), b.astype(_BF), preferred_element_type=_F32)


def _mm_f32(a, b):
    return jnp.dot(a, b, preferred_element_type=_F32, precision=lax.Precision.HIGHEST)


def _mm_nt(a, b):
    return lax.dot_general(a.astype(_BF), b.astype(_BF), (((1,), (1,)), ((), ())),
                           preferred_element_type=_F32)


def _mm_tn(a, b):
    return lax.dot_general(a.astype(_BF), b.astype(_BF), (((0,), (0,)), ((), ())),
                           preferred_element_type=_F32)


def _sigmoid(x):
    return jax.nn.sigmoid(x)


def _silu(x):
    return x * jax.nn.sigmoid(x)


def _gelu_tanh(x):
    return 0.5 * x * (1.0 + jnp.tanh(math.sqrt(2.0 / math.pi) * (x + 0.044715 * (x * x * x))))


def _standardize(x, eps):
    xc = x - jnp.mean(x, -1, keepdims=True)
    return xc * lax.rsqrt(jnp.mean(xc * xc, -1, keepdims=True) + eps)


def _rms_mod(x, g, shift, scale):
    y = x * lax.rsqrt(jnp.mean(x * x, -1, keepdims=True) + NORM_EPS) * g
    return y * (1.0 + scale) + shift


def _params(*sem):
    return pltpu.CompilerParams(dimension_semantics=sem, vmem_limit_bytes=VMEM_LIMIT)


def _mod_kernel(c_ref, w_ref, b_ref, o_ref):
    o_ref[...] = _mm(_silu(c_ref[...]), w_ref[...]) + b_ref[...]


def _modulation(cvec, w_mod, b_mod):
    depth = w_mod.shape[0]
    rows = cvec.shape[0]
    return pl.pallas_call(
        _mod_kernel,
        grid=(depth, 6),
        in_specs=[
            pl.BlockSpec((rows, D_MODEL), lambda l, j: (0, 0)),
            pl.BlockSpec((None, D_MODEL, D_MODEL), lambda l, j: (l, 0, j)),
            pl.BlockSpec((None, 1, D_MODEL), lambda l, j: (l, 0, j)),
        ],
        out_specs=pl.BlockSpec((None, rows, D_MODEL), lambda l, j: (l, 0, j)),
        out_shape=jax.ShapeDtypeStruct((depth, rows, 6 * D_MODEL), _F32),
        compiler_params=_params("arbitrary", "arbitrary"),
        name="modulation",
    )(cvec, w_mod, b_mod)


def _mod_spec(grid_rank):
    if grid_rank == 2:
        return pl.BlockSpec((None, 1, D_MODEL), lambda b, i: (2 * b + i // LAT_ROW_TILES, 0, 0))
    return pl.BlockSpec((None, 1, D_MODEL), lambda j, b, i: (2 * b + i // LAT_ROW_TILES, 0, 0))


def _inproj_kernel(x_ref, g_ref, sh_ref, sc_ref, w_ref, o_ref):
    h = _rms_mod(x_ref[...], g_ref[...], sh_ref[...], sc_ref[...])
    o_ref[...] = _mm(h, w_ref[...]).astype(o_ref.dtype)


def _in_projection(x_all, g, shift, scale, w):
    batch = x_all.shape[0]
    n_col = 2
    tn = Z_COLS // n_col
    return pl.pallas_call(
        _inproj_kernel,
        grid=(n_col, batch, N_ROW_TILES),
        in_specs=[
            pl.BlockSpec((None, ROW_TILE, D_MODEL), lambda j, b, i: (b, i, 0)),
            pl.BlockSpec((1, D_MODEL), lambda j, b, i: (0, 0)),
            _mod_spec(3),
            _mod_spec(3),
            pl.BlockSpec((D_MODEL, tn), lambda j, b, i: (0, j)),
        ],
        out_specs=pl.BlockSpec((None, ROW_TILE, tn), lambda j, b, i: (b, i, j)),
        out_shape=jax.ShapeDtypeStruct((batch, T_ALL, Z_COLS), _BF),
        compiler_params=_params("arbitrary", "arbitrary", "arbitrary"),
        name="in_projection",
    )(x_all, g, shift, scale, w)


HALO = 8
HALO_BLOCKS = ROW_TILE // HALO
SHIFT_TILES = 3


def _inproj_shift_kernel(x_ref, xp_ref, xn_ref, g_ref, sh_ref, sc_ref, w_ref, ws_ref, o_ref):
    i = pl.program_id(1)
    starts = (i == 0) | (i == LAT_ROW_TILES)
    ends = (i == LAT_ROW_TILES - 1) | (i == N_ROW_TILES - 1)
    g, sh, sc = g_ref[...], sh_ref[...], sc_ref[...]
    h_prev = _rms_mod(xp_ref[...], g, sh, sc) * jnp.where(starts, 0.0, 1.0)
    h_next = _rms_mod(xn_ref[...], g, sh, sc) * jnp.where(ends, 0.0, 1.0)
    h = jnp.concatenate([h_prev, _rms_mod(x_ref[...], g, sh, sc), h_next], axis=0).astype(_BF)
    n_rows = ROW_TILE + 2 * HALO
    inner = slice(HALO, HALO + ROW_TILE)
    for j0 in range(0, RWKV_TILES, SHIFT_TILES):
        cols = slice(j0 * LANES, (j0 + SHIFT_TILES) * LANES)
        z = jnp.dot(h, w_ref[:, cols], preferred_element_type=_F32)
        w = ws_ref[:, cols]
        zr = (pltpu.roll(z, 1, 0)[inner] * w[0:1] + z[inner] * w[1:2]
              + pltpu.roll(z, n_rows - 1, 0)[inner] * w[2:3])
        for t in range(SHIFT_TILES):
            o_ref[j0 + t] = zr[:, t * LANES:(t + 1) * LANES].astype(o_ref.dtype)


def _in_projection_rwkv(x_all, g, shift, scale, w, w_shift):
    batch = x_all.shape[0]
    assert RWKV_TILES % SHIFT_TILES == 0
    last_block = T_ALL // HALO - 1
    vec = pl.BlockSpec((1, D_MODEL), lambda b, i: (0, 0))
    return pl.pallas_call(
        _inproj_shift_kernel,
        grid=(batch, N_ROW_TILES),
        in_specs=[
            pl.BlockSpec((None, ROW_TILE, D_MODEL), lambda b, i: (b, i, 0)),
            pl.BlockSpec((None, HALO, D_MODEL),
                         lambda b, i: (b, jnp.maximum(i * HALO_BLOCKS - 1, 0), 0)),
            pl.BlockSpec((None, HALO, D_MODEL),
                         lambda b, i: (b, jnp.minimum((i + 1) * HALO_BLOCKS, last_block), 0)),
            vec, _mod_spec(2), _mod_spec(2),
            pl.BlockSpec((D_MODEL, RWKV_COLS), lambda b, i: (0, 0)),
            pl.BlockSpec((3, RWKV_COLS), lambda b, i: (0, 0)),
        ],
        out_specs=pl.BlockSpec((None, RWKV_TILES, ROW_TILE, LANES), lambda b, i: (b, 0, i, 0)),
        out_shape=jax.ShapeDtypeStruct((batch, RWKV_TILES, T_ALL, LANES), _BF),
        compiler_params=_params("arbitrary", "arbitrary"),
        name="in_projection_rwkv",
    )(x_all, x_all, x_all, g, shift, scale, w, w_shift)


def _head_sum(x):
    lane = lax.broadcasted_iota(jnp.int32, x.shape, x.ndim - 1)
    lo = jnp.sum(jnp.where(lane < HEAD_DIM, x, 0.0), -1, keepdims=True)
    tot = jnp.sum(x, -1, keepdims=True)
    return jnp.where(lane < HEAD_DIM, lo, tot - lo)


def _bmm(a, b):
    return lax.dot_general(a.astype(_BF), b.astype(_BF), (((2,), (1,)), ((0,), (0,))),
                           preferred_element_type=_F32)


def _bmm_nt(a, b):
    return lax.dot_general(a.astype(_BF), b.astype(_BF), (((2,), (2,)), ((0,), (0,))),
                           preferred_element_type=_F32)


def _bmm_tn(a, b):
    return lax.dot_general(a.astype(_BF), b.astype(_BF), (((1,), (1,)), ((0,), (0,))),
                           preferred_element_type=_F32)


def _lane_tiles(x):
    return jnp.stack([x[:, i * LANES:(i + 1) * LANES] for i in range(HEAD_PAIRS)])


N_SLOT_REFS = 5


def _scan_kernel(*refs):
    tokens = (refs[0:2], refs[2:4])
    wup_ref, w0_ref, aup_ref, a0_ref, kk_ref, ka_ref, mi_ref, mt_ref, mb_ref = refs[4:13]
    y_refs = refs[13:15]
    s_ref = refs[15]
    slots = (refs[16:16 + N_SLOT_REFS], refs[16 + N_SLOT_REFS:])
    step = pl.program_id(1)

    @pl.when(step == 0)
    def _():
        s_ref[...] = jnp.zeros_like(s_ref)
        for ref in slots[1]:
            ref[...] = jnp.zeros_like(ref)

    def run(prep, solve):
        pieces = itertools.chain.from_iterable(
            _scan_prepare(tokens[d], (wup_ref.at[d], w0_ref.at[d], aup_ref.at[d], a0_ref.at[d],
                                      kk_ref, ka_ref, mi_ref.at[d]), slots[prep], d)
            for d in range(2))
        next(pieces)
        for _ in _scan_solve(slots[solve], mt_ref, mb_ref, y_refs, s_ref):
            next(pieces, None)
        for _ in pieces:
            pass

    @pl.when(step % 2 == 0)
    def _():
        run(0, 1)

    @pl.when(step % 2 == 1)
    def _():
        run(1, 0)


PREP_PAIRS = 8
PREP_PIECES = HEAD_PAIRS // PREP_PAIRS


def _scan_prepare(token_refs, param_refs, slot, d):
    rkv_ref, lora_ref = token_refs
    r_ref, k_ref, v_ref = (rkv_ref.at[pl.ds(i * HEAD_PAIRS, HEAD_PAIRS)] for i in range(3))
    lw_ref, la_ref = lora_ref.at[0], lora_ref.at[1]
    wup_ref, w0_ref, aup_ref, a0_ref, kk_ref, ka_ref, mi_ref = param_refs
    lhs_ref, rk_ref, vv_ref, ke_ref, dt_ref = slot
    width = PREP_PAIRS * LANES

    def tiles(x):
        return jnp.stack([x[:, i * LANES:(i + 1) * LANES] for i in range(PREP_PAIRS)])

    lw_act = jnp.tanh(lw_ref[...].astype(_F32))
    la_in = la_ref[...]
    m_incl = mi_ref[...]
    for g in range(PREP_PIECES):
        cols = pl.ds(g * width, width)
        pairs = pl.ds(g * PREP_PAIRS, PREP_PAIRS)
        xw = w0_ref[:, cols] + _mm(lw_act, wup_ref[:, cols])
        ld = -(math.exp(-0.5) * math.log2(math.e)) * _sigmoid(xw)
        a = tiles(_sigmoid(a0_ref[:, cols] + _mm(la_in, aup_ref[:, cols])))
        hi = ld.astype(_BF)
        rem = ld - hi.astype(_F32)
        mid = rem.astype(_BF)
        lo = (rem - mid.astype(_F32)).astype(_BF)
        cl = (jnp.dot(m_incl, hi, preferred_element_type=_F32)
              + jnp.dot(m_incl, mid, preferred_element_type=_F32)
              + jnp.dot(m_incl, lo, preferred_element_type=_F32))
        ld_tot = jnp.sum(ld, 0, keepdims=True)
        dec_tot = jnp.exp2(ld_tot)
        e_in = tiles(jnp.exp2(cl))
        e_out = tiles(jnp.exp2(-cl))
        e_ex = tiles(jnp.exp2(cl - ld))
        e_end = tiles(jnp.exp2(ld_tot - cl))

        r = r_ref[pairs].astype(_F32)
        k = k_ref[pairs].astype(_F32)
        kk = k * tiles(kk_ref[:, cols])
        kk = kk * lax.rsqrt(jnp.maximum(_head_sum(kk * kk), 1e-24))
        k_d = k * (1.0 + (a - 1.0) * tiles(ka_ref[:, cols]))
        kka = kk * a

        rows = pl.ds(d * HEAD_PAIRS + g * PREP_PAIRS, PREP_PAIRS)
        lhs_ref[rows] = jnp.concatenate([kk * e_ex, r * e_in], axis=1).astype(_BF)
        rk_ref[rows] = jnp.swapaxes(jnp.concatenate([k_d * e_out, kka * e_out], axis=1),
                                    1, 2).astype(_BF)
        vv_ref[rows] = v_ref[pairs]
        ke_ref[rows] = jnp.concatenate([k_d * e_end, -(kka * e_end)], axis=1).astype(_BF)
        dt_ref[rows] = tiles(dec_tot)
        yield


def _by_direction(x, m_ref):
    return jnp.concatenate([x[:HEAD_PAIRS] * m_ref[0], x[HEAD_PAIRS:] * m_ref[1]], axis=0)


def _scan_solve(slot, mt_ref, mb_ref, y_refs, s_ref):
    c = SCAN_CHUNK
    lhs_ref, rk_ref, vv_ref, ke_ref, dt_ref = slot
    lhs = lhs_ref[...]
    rk = rk_ref[...]
    v = vv_ref[...]
    zero = jnp.zeros((), _BF)
    even = lax.broadcasted_iota(jnp.int32, (1, 1, LANES), 2) < HEAD_DIM
    kkt, rt = lhs[:, :c], lhs[:, c:]
    main = _bmm(jnp.concatenate([jnp.where(even, kkt, zero), jnp.where(even, rt, zero),
                                 jnp.where(even, zero, rt), jnp.where(even, zero, kkt)], axis=1), rk)
    yield
    sbd = s_ref[...]
    ls = _bmm(lhs, jnp.swapaxes(sbd, 1, 2))
    yield
    top_odd = pltpu.roll(main[:, 3 * c:], HEAD_DIM, 2)
    main = main[:, :3 * c].astype(_BF)
    top = _by_direction(jnp.concatenate([main[:, :c], top_odd.astype(_BF)], axis=1), mt_ref)
    bot = _by_direction(main[:, c:], mb_ref)
    row_even = lax.broadcasted_iota(jnp.int32, (1, 2 * c, LANES), 1) < c
    is_x = row_even == even
    vv = jnp.concatenate([v, v], axis=1)
    out = _bmm(jnp.where(is_x, top, zero), vv)
    yield
    def both_heads(o):
        return jnp.where(even, o[:, :c], o[:, c:])

    def operand(x, n_part):
        xb = x.astype(_BF)
        return jnp.where(is_x, jnp.concatenate([xb, xb], axis=1), n_part)

    x = ls[:, :c] + both_heads(out)
    zb = operand(x, top)
    n_steps = int(math.log2(c))
    for i in range(n_steps):
        out = _bmm(jnp.where(is_x, zero, zb), jnp.concatenate([zb[:, c:], zb[:, :c]], axis=1))
        yield
        x = x + both_heads(out)
        if i + 1 < n_steps:
            zb = operand(x, out.astype(_BF))
    vu = jnp.concatenate([v, x.astype(_BF)], axis=1)
    yy = _bmm(bot, vu)
    upd = _bmm_tn(vu, ke_ref[...])
    yield
    y = ls[:, c:] + jnp.where(even, yy[:, :c], yy[:, c:])
    y_refs[0][...] = y[:HEAD_PAIRS]
    y_refs[1][...] = y[HEAD_PAIRS:]
    blk_row = lax.broadcasted_iota(jnp.int32, (1, LANES, LANES), 1) < HEAD_DIM
    blk_col = lax.broadcasted_iota(jnp.int32, (1, LANES, LANES), 2) < HEAD_DIM
    s_ref[...] = jnp.where(blk_row == blk_col, sbd * dt_ref[...] + upd, 0.0)


def _scan_chunk_index(d, s):
    return (s + LAT_CHUNKS) % N_CHUNKS if d == 0 else N_CHUNKS - 1 - s


def _wkv_scan(zr, wup, w0, aup, a0, k_k, k_a, m_incl, m_top, m_bot):
    batch = zr.shape[0]
    c = SCAN_CHUNK

    def prep_chunk(d, s):
        return _scan_chunk_index(d, jnp.minimum(s, N_CHUNKS - 1))

    def solve_chunk(d, s):
        return _scan_chunk_index(d, jnp.maximum(s - 1, 0))

    def tokens(d):
        assert TILE_W % 2 == 0 and TILE_A == TILE_W + 1
        return [pl.BlockSpec((None, TILE_W, c, LANES), lambda b, s: (b, 0, prep_chunk(d, s), 0)),
                pl.BlockSpec((None, 2, c, LANES), lambda b, s: (b, TILE_W // 2, prep_chunk(d, s), 0))]

    def whole(*shape):
        return pl.BlockSpec(shape, lambda b, s: (0,) * len(shape))

    def y_spec(d):
        return pl.BlockSpec((None, HEAD_PAIRS, c, LANES), lambda b, s: (b, 0, solve_chunk(d, s), 0))

    both = 2 * HEAD_PAIRS
    slot = [pltpu.VMEM((both, 2 * c, LANES), _BF), pltpu.VMEM((both, 2 * c, LANES), _BF),
            pltpu.VMEM((both, c, LANES), _BF), pltpu.VMEM((both, 2 * c, LANES), _BF),
            pltpu.VMEM((both, 1, LANES), _F32)]
    assert len(slot) == N_SLOT_REFS
    y_shape = jax.ShapeDtypeStruct((batch, HEAD_PAIRS, T_ALL, LANES), _F32)
    return pl.pallas_call(
        _scan_kernel,
        grid=(batch, N_CHUNKS + 1),
        in_specs=tokens(0) + tokens(1) + [
            whole(2, LANES, D_MODEL), whole(2, 1, D_MODEL),
            whole(2, LANES, D_MODEL), whole(2, 1, D_MODEL),
            whole(1, D_MODEL), whole(1, D_MODEL),
            whole(2, c, c), whole(2, 2 * c, 2 * c), whole(2, 2 * c, 2 * c),
        ],
        out_specs=[y_spec(0), y_spec(1)],
        out_shape=[y_shape, y_shape],
        scratch_shapes=[pltpu.VMEM((both, LANES, LANES), _F32)] + slot * 2,
        compiler_params=_params("arbitrary", "arbitrary"),
        name="wkv_scan",
    )(*([zr] * 4), wup, w0, aup, a0, k_k, k_a, m_incl, m_top, m_bot)


READOUT_PAIRS = 2


def _rwkv_readout(yf_ref, yb_ref, r_ref, k_ref, v_ref, gz_ref, gup_ref, gng_ref, gnb_ref, rk_ref,
                  wro_ref):
    gate = _mm(_sigmoid(gz_ref[...].astype(_F32)), gup_ref[...])
    acc = jnp.zeros((ROW_TILE, D_MODEL), _F32)
    for piece in range(HEAD_PAIRS // READOUT_PAIRS):
        parts = []
        for hp in range(piece * READOUT_PAIRS, (piece + 1) * READOUT_PAIRS):
            cols = slice(hp * LANES, (hp + 1) * LANES)
            y = yf_ref[hp] + yb_ref[hp]
            mean = _head_sum(y) * (1.0 / HEAD_DIM)
            yc = y - mean
            var = _head_sum(yc * yc) * (1.0 / HEAD_DIM)
            yn = yc * lax.rsqrt(var + GN_EPS) * gng_ref[:, cols] + gnb_ref[:, cols]
            bonus = (_head_sum(r_ref[hp].astype(_F32) * k_ref[hp].astype(_F32) * rk_ref[:, cols])
                     * v_ref[hp].astype(_F32))
            parts.append(((yn + bonus) * gate[:, cols]).astype(_BF))
        rows = slice(piece * READOUT_PAIRS * LANES, (piece + 1) * READOUT_PAIRS * LANES)
        acc = acc + _mm(jnp.concatenate(parts, axis=1), wro_ref[rows, :])
    return acc


_H_SLOT = GRID_W + 16
_H_LEAD = 16
_V_PAD = CONV_HALF * GRID_W
_CTX_LEAD = 16


def _conv_taps(pad_ref, w, base, length, stride):
    acc = None
    for j in range(CONV_WIDTH):
        term = pad_ref[pl.ds(base + (j - CONV_HALF) * stride, length), :] * w[j:j + 1]
        acc = term if acc is None else acc + term
    return acc


def _glu(zv_ref, zg_ref, rows):
    return zv_ref[rows, :].astype(_F32) * _sigmoid(zg_ref[rows, :].astype(_F32))


def _conv_kernel(horizontal, zv_ref, zg_ref, w_ref, b_ref, o_ref, lat_ref, ctx_ref):
    w = w_ref[...]
    bias = b_ref[...]
    lat_ref[...] = jnp.zeros_like(lat_ref)
    ctx_ref[...] = jnp.zeros_like(ctx_ref)
    ctx_ref[pl.ds(_CTX_LEAD, CTX_LEN), :] = (
        _glu(zv_ref, zg_ref, pl.ds(SEQ, CTX_LEN)))
    o_ref[pl.ds(SEQ, CTX_LEN), :] = _conv_taps(ctx_ref, w, _CTX_LEAD, CTX_LEN, 1) + bias
    if horizontal:
        for row in range(GRID_H):
            src = pl.ds(row * GRID_W, GRID_W)
            lat_ref[pl.ds(_H_LEAD + row * _H_SLOT, GRID_W), :] = (
                _glu(zv_ref, zg_ref, src))
        for row in range(GRID_H):
            o_ref[pl.ds(row * GRID_W, GRID_W), :] = (
                _conv_taps(lat_ref, w, _H_LEAD + row * _H_SLOT, GRID_W, 1) + bias)
    else:
        lat_ref[pl.ds(_V_PAD, SEQ), :] = (
            _glu(zv_ref, zg_ref, pl.ds(0, SEQ)))
        blk = 4 * GRID_W
        for i in range(SEQ // blk):
            o_ref[pl.ds(i * blk, blk), :] = _conv_taps(lat_ref, w, _V_PAD + i * blk, blk, GRID_W) + bias


def _conformer_conv(z, dw, dw_b, horizontal):
    batch = z.shape[0]
    first = Z_CONV // LANES
    n_tiles = CONV_DIM // LANES
    lat_rows = (_H_LEAD + GRID_H * _H_SLOT) if horizontal else (SEQ + 2 * _V_PAD)
    return pl.pallas_call(
        functools.partial(_conv_kernel, horizontal),
        grid=(batch, n_tiles),
        in_specs=[
            pl.BlockSpec((None, T_ALL, LANES), lambda b, j: (b, 0, first + j)),
            pl.BlockSpec((None, T_ALL, LANES), lambda b, j: (b, 0, first + n_tiles + j)),
            pl.BlockSpec((CONV_WIDTH, LANES), lambda b, j: (0, j)),
            pl.BlockSpec((1, LANES), lambda b, j: (0, j)),
        ],
        out_specs=pl.BlockSpec((None, T_ALL, LANES), lambda b, j: (b, 0, j)),
        out_shape=jax.ShapeDtypeStruct((batch, T_ALL, CONV_DIM), _F32),
        scratch_shapes=[pltpu.VMEM((lat_rows, LANES), _F32),
                        pltpu.VMEM((CTX_LEN + 2 * _CTX_LEAD, LANES), _F32)],
        compiler_params=_params("arbitrary", "arbitrary"),
        name="conformer_conv_h" if horizontal else "conformer_conv_v",
    )(z, z, dw, dw_b)


def _sgu_kernel(zu_ref, zv_ref, lng_ref, lnb_ref, ws_ref, bs_ref, o_ref):
    v = _standardize(_gelu_tanh(zv_ref[...].astype(_F32)), LN_EPS) * lng_ref[...] + lnb_ref[...]
    v = v.astype(_BF)
    first_half = lax.broadcasted_iota(jnp.int32, (SGU_CHUNK, LANES), 1) < SGU_GROUP_DIM
    for n in range(ROW_TILE // SGU_CHUNK):
        rows = slice(n * SGU_CHUNK, (n + 1) * SGU_CHUNK)
        tiles = []
        for t in range(SGU_DIM // LANES):
            vt = v[rows, t * LANES:(t + 1) * LANES]
            tiles.append(jnp.where(first_half, _mm(ws_ref[2 * t], vt), _mm(ws_ref[2 * t + 1], vt)))
        mixed = jnp.concatenate(tiles, axis=1) + bs_ref[...]
        o_ref[rows, :] = (_gelu_tanh(zu_ref[rows, :].astype(_F32)) * mixed).astype(o_ref.dtype)


def _sgu(z, ln_g, ln_b, w_s, b_tile):
    batch = z.shape[0]
    first = Z_SGU // SGU_DIM
    vec = pl.BlockSpec((1, SGU_DIM), lambda b, i: (0, 0))
    return pl.pallas_call(
        _sgu_kernel,
        grid=(batch, N_ROW_TILES),
        in_specs=[
            pl.BlockSpec((None, ROW_TILE, SGU_DIM), lambda b, i: (b, i, first)),
            pl.BlockSpec((None, ROW_TILE, SGU_DIM), lambda b, i: (b, i, first + 1)),
            vec, vec,
            pl.BlockSpec((SGU_GROUPS, SGU_CHUNK, SGU_CHUNK), lambda b, i: (0, 0, 0)),
            pl.BlockSpec((SGU_CHUNK, SGU_DIM), lambda b, i: (0, 0)),
        ],
        out_specs=pl.BlockSpec((None, ROW_TILE, SGU_DIM), lambda b, i: (b, i, 0)),
        out_shape=jax.ShapeDtypeStruct((batch, T_ALL, SGU_DIM), _BF),
        compiler_params=_params("arbitrary", "arbitrary"),
        name="sgu",
    )(z, z, ln_g, ln_b, w_s, b_tile)


def _merge_kernel(x_ref, yf_ref, yb_ref, r_ref, k_ref, v_ref, gz_ref, cv_ref, c_ref,
                  g0_ref, g1_ref, g2_ref, gate_ref, gup_ref, gng_ref, gnb_ref, rk_ref, lng_ref, lnb_ref,
                  wro_ref, wco_ref, wso_ref, wm_ref, o_ref):
    c = _mm(c_ref[...], wso_ref[...])
    cb = _silu(_standardize(cv_ref[...], LN_EPS) * lng_ref[...] + lnb_ref[...])
    b = _mm(cb, wco_ref[...])
    a = _rwkv_readout(yf_ref, yb_ref, r_ref, k_ref, v_ref, gz_ref, gup_ref, gng_ref, gnb_ref, rk_ref,
                      wro_ref)
    m = (a * _sigmoid(g0_ref[...].astype(_F32)) + b * _sigmoid(g1_ref[...].astype(_F32))
         + c * _sigmoid(g2_ref[...].astype(_F32)))
    o_ref[...] = x_ref[...] + gate_ref[...] * _mm(m, wm_ref[...])


def _merge(x_all, y_fwd, y_bwd, zr, conv, c_pre, z, gate1, g_up, gn_g, gn_b, r_k, ln_g, ln_b,
           w_ro, w_co, w_so, w_m):
    batch = x_all.shape[0]

    def rows(width, col=0):
        return pl.BlockSpec((None, ROW_TILE, width), lambda b, i: (b, i, col))

    def group(idx):
        return pl.BlockSpec((None, HEAD_PAIRS, ROW_TILE, LANES), lambda b, i: (b, idx, i, 0))

    def whole(shape):
        return pl.BlockSpec(shape, lambda b, i: (0, 0))

    vec = whole((1, D_MODEL))
    return pl.pallas_call(
        _merge_kernel,
        grid=(batch, N_ROW_TILES),
        in_specs=[
            rows(D_MODEL), group(0), group(0), group(0), group(1), group(2),
            pl.BlockSpec((None, None, ROW_TILE, LANES), lambda b, i: (b, TILE_G, i, 0)),
            rows(CONV_DIM), rows(SGU_DIM),
            rows(D_MODEL, 0), rows(D_MODEL, 1), rows(D_MODEL, 2),
            _mod_spec(2), whole((GATE_LORA, D_MODEL)), vec, vec, vec,
            whole((1, CONV_DIM)), whole((1, CONV_DIM)),
            whole((D_MODEL, D_MODEL)), whole((CONV_DIM, D_MODEL)), whole((SGU_DIM, D_MODEL)),
            whole((D_MODEL, D_MODEL)),
        ],
        out_specs=rows(D_MODEL),
        out_shape=jax.ShapeDtypeStruct((batch, T_ALL, D_MODEL), _F32),
        compiler_params=_params("arbitrary", "arbitrary"),
        name="merge",
    )(x_all, y_fwd, y_bwd, zr, zr, zr, zr, conv, c_pre, z, z, z, gate1, g_up, gn_g, gn_b, r_k,
      ln_g, ln_b, w_ro, w_co, w_so, w_m)


def _ffn_kernel(last, x_ref, g_ref, sh_ref, sc_ref, gate_ref, win_ref, wout_ref, fg_ref, o_ref):
    x = x_ref[...]
    h = _rms_mod(x, g_ref[...], sh_ref[...], sc_ref[...]).astype(_BF)
    def gate_up(f):
        cols = slice(f * FF_CHUNK, (f + 1) * FF_CHUNK)
        up_cols = slice(D_FF + f * FF_CHUNK, D_FF + (f + 1) * FF_CHUNK)
        return _mm(h, win_ref[:, cols]), _mm(h, win_ref[:, up_cols])

    n_chunks = D_FF // FF_CHUNK
    acc = jnp.zeros((ROW_TILE, D_MODEL), _F32)
    pending = gate_up(0)
    for f in range(n_chunks):
        g, u = pending
        if f + 1 < n_chunks:
            pending = gate_up(f + 1)
        acc = acc + _mm(_silu(g) * u, wout_ref[f * FF_CHUNK:(f + 1) * FF_CHUNK, :])
    out = x + gate_ref[...] * acc
    if last:
        out = out * lax.rsqrt(jnp.mean(out * out, -1, keepdims=True) + NORM_EPS) * fg_ref[...]
    o_ref[...] = out


def _ffn(x_all, g, shift, scale, gate2, w_in, w_out, final_g, last):
    batch = x_all.shape[0]
    rows = pl.BlockSpec((None, ROW_TILE, D_MODEL), lambda b, i: (b, i, 0))
    vec = pl.BlockSpec((1, D_MODEL), lambda b, i: (0, 0))
    return pl.pallas_call(
        functools.partial(_ffn_kernel, last),
        grid=(batch, LAT_ROW_TILES if last else N_ROW_TILES),
        in_specs=[
            rows, vec,
            _mod_spec(2), _mod_spec(2), _mod_spec(2),
            pl.BlockSpec((D_MODEL, 2 * D_FF), lambda b, i: (0, 0)),
            pl.BlockSpec((D_FF, D_MODEL), lambda b, i: (0, 0)),
            vec,
        ],
        out_specs=rows,
        out_shape=jax.ShapeDtypeStruct((batch, SEQ if last else T_ALL, D_MODEL), _F32),
        compiler_params=_params("arbitrary", "arbitrary"),
        name="swiglu_final" if last else "swiglu",
    )(x_all, g, shift, scale, gate2, w_in, w_out, final_g)


def _split_w_in(w):
    off_conv = RWKV_COLS
    off_gate = off_conv + 2 * CONV_DIM + 2 * SGU_DIM
    return jnp.concatenate([w[:, off_gate:], w[:, off_conv:off_gate]], axis=1), w[:, :off_conv]


def _direction_padded(w_up):
    zero = jnp.zeros_like(w_up[0])
    return jnp.stack([jnp.concatenate([w_up[0], zero], 0), jnp.concatenate([zero, w_up[1]], 0)])


def kernel(x, c, ctx, c_ctx, w_mod, b_mod, norm1_g, norm2_g, w_in, rwkv_shift, rwkv_w0, rwkv_w_up, rwkv_a0, rwkv_a_up, rwkv_g_up, rwkv_k_k, rwkv_k_a, rwkv_r_k, rwkv_gn_g, rwkv_gn_b, rwkv_out, conv_dw, conv_dw_b, conv_ln_g, conv_ln_b, conv_out, sgu_ln_g, sgu_ln_b, sgu_w, sgu_b, sgu_out, w_merge, ffn_w_in, ffn_w_out, final_norm_g):
    batch = x.shape[0]
    depth = w_mod.shape[0]
    assert x.shape[1:] == (SEQ, D_MODEL) and ctx.shape[1:] == (CTX_LEN, D_MODEL)

    rows = -(-(batch + 1) // 8) * 8
    cvec = jnp.zeros((rows, D_MODEL), _F32).at[:batch].set(c).at[batch].set(c_ctx)
    mod = _modulation(cvec, w_mod.astype(_BF), b_mod[:, None, :])
    mod_lat = mod[:, :batch].reshape(depth, batch, 6, D_MODEL)
    mod_ctx = jnp.broadcast_to(mod[:, batch].reshape(depth, 1, 6, D_MODEL), mod_lat.shape)
    mod_tab = jnp.stack([mod_lat, mod_ctx], axis=2).transpose(0, 3, 1, 2, 4)
    mod_tab = mod_tab.reshape(depth, 6, 2 * batch, 1, D_MODEL)

    idx = jnp.arange(SCAN_CHUNK)
    incl = jnp.stack([idx[:, None] >= idx[None, :], idx[:, None] <= idx[None, :]]).astype(_F32)
    strict = jnp.stack([idx[:, None] > idx[None, :], idx[:, None] < idx[None, :]]).astype(_F32)
    m_top = jnp.concatenate([jnp.concatenate([strict, -strict], 2),
                             jnp.concatenate([-strict, strict], 2)], 1)
    m_bot = jnp.concatenate([jnp.concatenate([incl, -incl], 2)] * 2, 1).astype(_BF)
    m_top = m_top.astype(_BF)
    m_incl = incl.astype(_BF)

    x_all = jnp.concatenate([x, ctx], axis=1)
    for l in range(depth):
        sh1, sc1, g1, sh2, sc2, g2 = (mod_tab[l, i] for i in range(6))
        w_z, w_r = _split_w_in(w_in[l])
        z = _in_projection(x_all, norm1_g[l][None], sh1, sc1, w_z.astype(_BF))
        zr = _in_projection_rwkv(x_all, norm1_g[l][None], sh1, sc1, w_r.astype(_BF), rwkv_shift[l])
        y_fwd, y_bwd = _wkv_scan(zr, _direction_padded(rwkv_w_up[l]).astype(_BF), rwkv_w0[l][:, None, :],
                      _direction_padded(rwkv_a_up[l]).astype(_BF), rwkv_a0[l][:, None, :],
                      rwkv_k_k[l][None], rwkv_k_a[l][None], m_incl, m_top, m_bot)
        conv = _conformer_conv(z, conv_dw[l], conv_dw_b[l][None], l % 2 == 0)
        b_tile = jnp.repeat(sgu_b[l].T, SGU_GROUP_DIM, axis=1)
        c_pre = _sgu(z, sgu_ln_g[l][None], sgu_ln_b[l][None], sgu_w[l].astype(_BF), b_tile)
        x_all = _merge(x_all, y_fwd, y_bwd, zr, conv, c_pre, z, g1, rwkv_g_up[l].astype(_BF),
                       rwkv_gn_g[l][None], rwkv_gn_b[l][None], rwkv_r_k[l][None],
                       conv_ln_g[l][None], conv_ln_b[l][None],
                       rwkv_out[l].astype(_BF), conv_out[l].astype(_BF), sgu_out[l].astype(_BF),
                       w_merge[l].astype(_BF))
        x_all = _ffn(x_all, norm2_g[l][None], sh2, sc2, g2, ffn_w_in[l].astype(_BF),
                     ffn_w_out[l].astype(_BF), final_norm_g[None], l == depth - 1)
    return x_all
```

```python
import functools
import itertools
import math

import jax
import jax.numpy as jnp
from jax import lax
from jax.experimental import pallas as pl
from jax.experimental.pallas import tpu as pltpu

D_MODEL = 1024
SEQ = 2048
CTX_LEN = 256
T_ALL = SEQ + CTX_LEN
GRID_W = 64
GRID_H = SEQ // GRID_W

HEAD_DIM = 64
HEADS = D_MODEL // HEAD_DIM
HEAD_PAIRS = HEADS // 2
DECAY_LORA = 64
ICLR_LORA = 64
GATE_LORA = 128
CONV_DIM = D_MODEL // 2
CONV_WIDTH = 31
CONV_HALF = CONV_WIDTH // 2
SGU_DIM = D_MODEL // 2
SGU_GROUPS = 8
SGU_GROUP_DIM = SGU_DIM // SGU_GROUPS
SGU_CHUNK = 128
D_FF = ((8 * D_MODEL // 3 + 255) // 256) * 256
NORM_EPS = 1e-6
LN_EPS = 1e-5
GN_EPS = 64e-5

RWKV_COLS = 3 * D_MODEL + 2 * DECAY_LORA + 2 * ICLR_LORA + GATE_LORA

LANES = 128
Z_GATE = 0
Z_CONV = 3 * D_MODEL
Z_SGU = Z_CONV + 2 * CONV_DIM
Z_COLS = Z_SGU + 2 * SGU_DIM
RWKV_TILES = RWKV_COLS // LANES
TILE_W = 3 * HEAD_PAIRS
TILE_A = TILE_W + 1
TILE_G = TILE_A + 1

ROW_TILE = 256
N_ROW_TILES = T_ALL // ROW_TILE
LAT_ROW_TILES = SEQ // ROW_TILE
SCAN_CHUNK = 64
N_CHUNKS = T_ALL // SCAN_CHUNK
LAT_CHUNKS = SEQ // SCAN_CHUNK
FF_CHUNK = 256

VMEM_LIMIT = 56 * 1024 * 1024

_BF = jnp.bfloat16
_F32 = jnp.float32


def _mm(a, b):
    return jnp.dot(a.astype(_BF), b.astype(_BF), preferred_element_type=_F32)


def _mm_f32(a, b):
    return jnp.dot(a, b, preferred_element_type=_F32, precision=lax.Precision.HIGHEST)


def _mm_nt(a, b):
    return lax.dot_general(a.astype(_BF), b.astype(_BF), (((1,), (1,)), ((), ())),
                           preferred_element_type=_F32)


def _mm_tn(a, b):
    return lax.dot_general(a.astype(_BF), b.astype(_BF), (((0,), (0,)), ((), ())),
                           preferred_element_type=_F32)


def _sigmoid(x):
    return jax.nn.sigmoid(x)


def _silu(x):
    return x * jax.nn.sigmoid(x)


def _gelu_tanh(x):
    return 0.5 * x * (1.0 + jnp.tanh(math.sqrt(2.0 / math.pi) * (x + 0.044715 * (x * x * x))))


def _standardize(x, eps):
    xc = x - jnp.mean(x, -1, keepdims=True)
    return xc * lax.rsqrt(jnp.mean(xc * xc, -1, keepdims=True) + eps)


def _rms_mod(x, g, shift, scale):
    y = x * lax.rsqrt(jnp.mean(x * x, -1, keepdims=True) + NORM_EPS) * g
    return y * (1.0 + scale) + shift


def _params(*sem):
    return pltpu.CompilerParams(dimension_semantics=sem, vmem_limit_bytes=VMEM_LIMIT)


def _mod_kernel(c_ref, w_ref, b_ref, o_ref):
    o_ref[...] = _mm(_silu(c_ref[...]), w_ref[...]) + b_ref[...]


def _modulation(cvec, w_mod, b_mod):
    depth = w_mod.shape[0]
    rows = cvec.shape[0]
    return pl.pallas_call(
        _mod_kernel,
        grid=(depth, 6),
        in_specs=[
            pl.BlockSpec((rows, D_MODEL), lambda l, j: (0, 0)),
            pl.BlockSpec((None, D_MODEL, D_MODEL), lambda l, j: (l, 0, j)),
            pl.BlockSpec((None, 1, D_MODEL), lambda l, j: (l, 0, j)),
        ],
        out_specs=pl.BlockSpec((None, rows, D_MODEL), lambda l, j: (l, 0, j)),
        out_shape=jax.ShapeDtypeStruct((depth, rows, 6 * D_MODEL), _F32),
        compiler_params=_params("arbitrary", "arbitrary"),
        name="modulation",
    )(cvec, w_mod, b_mod)


def _mod_spec(grid_rank):
    if grid_rank == 2:
        return pl.BlockSpec((None, 1, D_MODEL), lambda b, i: (2 * b + i // LAT_ROW_TILES, 0, 0))
    return pl.BlockSpec((None, 1, D_MODEL), lambda j, b, i: (2 * b + i // LAT_ROW_TILES, 0, 0))


def _inproj_kernel(x_ref, g_ref, sh_ref, sc_ref, w_ref, o_ref):
    h = _rms_mod(x_ref[...], g_ref[...], sh_ref[...], sc_ref[...])
    o_ref[...] = _mm(h, w_ref[...]).astype(o_ref.dtype)


def _in_projection(x_all, g, shift, scale, w):
    batch = x_all.shape[0]
    n_col = 1
    tn = Z_COLS // n_col
    return pl.pallas_call(
        _inproj_kernel,
        grid=(n_col, batch, N_ROW_TILES),
        in_specs=[
            pl.BlockSpec((None, ROW_TILE, D_MODEL), lambda j, b, i: (b, i, 0)),
            pl.BlockSpec((1, D_MODEL), lambda j, b, i: (0, 0)),
            _mod_spec(3),
            _mod_spec(3),
            pl.BlockSpec((D_MODEL, tn), lambda j, b, i: (0, j)),
        ],
        out_specs=pl.BlockSpec((None, ROW_TILE, tn), lambda j, b, i: (b, i, j)),
        out_shape=jax.ShapeDtypeStruct((batch, T_ALL, Z_COLS), _BF),
        compiler_params=_params("arbitrary", "arbitrary", "arbitrary"),
        name="in_projection",
    )(x_all, g, shift, scale, w)


HALO = 8
HALO_BLOCKS = ROW_TILE // HALO
SHIFT_TILES = 4


def _inproj_shift_kernel(x_ref, xp_ref, xn_ref, g_ref, sh_ref, sc_ref, w_ref, ws_ref, o_ref):
    i = pl.program_id(1)
    starts = (i == 0) | (i == LAT_ROW_TILES)
    ends = (i == LAT_ROW_TILES - 1) | (i == N_ROW_TILES - 1)
    g, sh, sc = g_ref[...], sh_ref[...], sc_ref[...]
    h_prev = _rms_mod(xp_ref[...], g, sh, sc) * jnp.where(starts, 0.0, 1.0)
    h_next = _rms_mod(xn_ref[...], g, sh, sc) * jnp.where(ends, 0.0, 1.0)
    h = jnp.concatenate([h_prev, _rms_mod(x_ref[...], g, sh, sc), h_next], axis=0).astype(_BF)
    n_rows = ROW_TILE + 2 * HALO
    inner = slice(HALO, HALO + ROW_TILE)
    for j0 in range(0, RWKV_TILES, SHIFT_TILES):
        n_tiles = min(SHIFT_TILES, RWKV_TILES - j0)
        cols = slice(j0 * LANES, (j0 + n_tiles) * LANES)
        z = jnp.dot(h, w_ref[:, cols], preferred_element_type=_F32)
        w = ws_ref[:, cols]
        zr = (pltpu.roll(z, 1, 0)[inner] * w[0:1] + z[inner] * w[1:2]
              + pltpu.roll(z, n_rows - 1, 0)[inner] * w[2:3])
        for t in range(n_tiles):
            o_ref[j0 + t] = zr[:, t * LANES:(t + 1) * LANES].astype(o_ref.dtype)


def _in_projection_rwkv(x_all, g, shift, scale, w, w_shift):
    batch = x_all.shape[0]
    last_block = T_ALL // HALO - 1
    vec = pl.BlockSpec((1, D_MODEL), lambda b, i: (0, 0))
    return pl.pallas_call(
        _inproj_shift_kernel,
        grid=(batch, N_ROW_TILES),
        in_specs=[
            pl.BlockSpec((None, ROW_TILE, D_MODEL), lambda b, i: (b, i, 0)),
            pl.BlockSpec((None, HALO, D_MODEL),
                         lambda b, i: (b, jnp.maximum(i * HALO_BLOCKS - 1, 0), 0)),
            pl.BlockSpec((None, HALO, D_MODEL),
                         lambda b, i: (b, jnp.minimum((i + 1) * HALO_BLOCKS, last_block), 0)),
            vec, _mod_spec(2), _mod_spec(2),
            pl.BlockSpec((D_MODEL, RWKV_COLS), lambda b, i: (0, 0)),
            pl.BlockSpec((3, RWKV_COLS), lambda b, i: (0, 0)),
        ],
        out_specs=pl.BlockSpec((None, RWKV_TILES, ROW_TILE, LANES), lambda b, i: (b, 0, i, 0)),
        out_shape=jax.ShapeDtypeStruct((batch, RWKV_TILES, T_ALL, LANES), _BF),
        compiler_params=_params("arbitrary", "arbitrary"),
        name="in_projection_rwkv",
    )(x_all, x_all, x_all, g, shift, scale, w, w_shift)


def _head_sum(x):
    lane = lax.broadcasted_iota(jnp.int32, x.shape, x.ndim - 1)
    lo = jnp.sum(jnp.where(lane < HEAD_DIM, x, 0.0), -1, keepdims=True)
    tot = jnp.sum(x, -1, keepdims=True)
    return jnp.where(lane < HEAD_DIM, lo, tot - lo)


def _bmm(a, b):
    return lax.dot_general(a.astype(_BF), b.astype(_BF), (((2,), (1,)), ((0,), (0,))),
                           preferred_element_type=_F32)


def _bmm_nt(a, b):
    return lax.dot_general(a.astype(_BF), b.astype(_BF), (((2,), (2,)), ((0,), (0,))),
                           preferred_element_type=_F32)


def _bmm_tn(a, b):
    return lax.dot_general(a.astype(_BF), b.astype(_BF), (((1,), (1,)), ((0,), (0,))),
                           preferred_element_type=_F32)


def _lane_tiles(x):
    return jnp.stack([x[:, i * LANES:(i + 1) * LANES] for i in range(HEAD_PAIRS)])


N_SLOT_REFS = 5


def _scan_kernel(*refs):
    tokens = (refs[0:2], refs[2:4])
    wup_ref, w0_ref, aup_ref, a0_ref, kk_ref, ka_ref, mi_ref, mt_ref, mb_ref = refs[4:13]
    y_refs = refs[13:15]
    s_ref = refs[15]
    slots = (refs[16:16 + N_SLOT_REFS], refs[16 + N_SLOT_REFS:])
    step = pl.program_id(1)

    @pl.when(step == 0)
    def _():
        s_ref[...] = jnp.zeros_like(s_ref)
        for ref in slots[1]:
            ref[...] = jnp.zeros_like(ref)

    def run(prep, solve):
        pieces = itertools.chain.from_iterable(
            _scan_prepare(tokens[d], (wup_ref.at[d], w0_ref.at[d], aup_ref.at[d], a0_ref.at[d],
                                      kk_ref, ka_ref, mi_ref.at[d]), slots[prep], d)
            for d in range(2))
        next(pieces)
        for _ in _scan_solve(slots[solve], mt_ref, mb_ref, y_refs, s_ref):
            next(pieces, None)
        for _ in pieces:
            pass

    @pl.when(step % 2 == 0)
    def _():
        run(0, 1)

    @pl.when(step % 2 == 1)
    def _():
        run(1, 0)


PREP_PAIRS = 8
PREP_PIECES = HEAD_PAIRS // PREP_PAIRS


def _scan_prepare(token_refs, param_refs, slot, d):
    rkv_ref, lora_ref = token_refs
    r_ref, k_ref, v_ref = (rkv_ref.at[pl.ds(i * HEAD_PAIRS, HEAD_PAIRS)] for i in range(3))
    lw_ref, la_ref = lora_ref.at[0], lora_ref.at[1]
    wup_ref, w0_ref, aup_ref, a0_ref, kk_ref, ka_ref, mi_ref = param_refs
    lhs_ref, rk_ref, vv_ref, ke_ref, dt_ref = slot
    width = PREP_PAIRS * LANES

    def tiles(x):
        return jnp.stack([x[:, i * LANES:(i + 1) * LANES] for i in range(PREP_PAIRS)])

    lw_act = jnp.tanh(lw_ref[...].astype(_F32))
    la_in = la_ref[...]
    m_incl = mi_ref[...]
    for g in range(PREP_PIECES):
        cols = pl.ds(g * width, width)
        pairs = pl.ds(g * PREP_PAIRS, PREP_PAIRS)
        xw = w0_ref[:, cols] + _mm(lw_act, wup_ref[:, cols])
        ld = -(math.exp(-0.5) * math.log2(math.e)) * _sigmoid(xw)
        a = tiles(_sigmoid(a0_ref[:, cols] + _mm(la_in, aup_ref[:, cols])))
        hi = ld.astype(_BF)
        rem = ld - hi.astype(_F32)
        mid = rem.astype(_BF)
        lo = (rem - mid.astype(_F32)).astype(_BF)
        cl = (jnp.dot(m_incl, hi, preferred_element_type=_F32)
              + jnp.dot(m_incl, mid, preferred_element_type=_F32)
              + jnp.dot(m_incl, lo, preferred_element_type=_F32))
        ld_tot = jnp.sum(ld, 0, keepdims=True)
        dec_tot = jnp.exp2(ld_tot)
        e_in = tiles(jnp.exp2(cl))
        e_out = tiles(jnp.exp2(-cl))
        e_ex = tiles(jnp.exp2(cl - ld))
        e_end = tiles(jnp.exp2(ld_tot - cl))

        r = r_ref[pairs].astype(_F32)
        k = k_ref[pairs].astype(_F32)
        kk = k * tiles(kk_ref[:, cols])
        kk = kk * lax.rsqrt(jnp.maximum(_head_sum(kk * kk), 1e-24))
        k_d = k * (1.0 + (a - 1.0) * tiles(ka_ref[:, cols]))
        kka = kk * a

        rows = pl.ds(d * HEAD_PAIRS + g * PREP_PAIRS, PREP_PAIRS)
        lhs_ref[rows] = jnp.concatenate([kk * e_ex, r * e_in], axis=1).astype(_BF)
        rk_ref[rows] = jnp.swapaxes(jnp.concatenate([k_d * e_out, kka * e_out], axis=1),
                                    1, 2).astype(_BF)
        vv_ref[rows] = v_ref[pairs]
        ke_ref[rows] = jnp.concatenate([k_d * e_end, -(kka * e_end)], axis=1).astype(_BF)
        dt_ref[rows] = tiles(dec_tot)
        yield


def _by_direction(x, m_ref):
    return jnp.concatenate([x[:HEAD_PAIRS] * m_ref[0], x[HEAD_PAIRS:] * m_ref[1]], axis=0)


def _scan_solve(slot, mt_ref, mb_ref, y_refs, s_ref):
    c = SCAN_CHUNK
    lhs_ref, rk_ref, vv_ref, ke_ref, dt_ref = slot
    lhs = lhs_ref[...]
    rk = rk_ref[...]
    v = vv_ref[...]
    zero = jnp.zeros((), _BF)
    even = lax.broadcasted_iota(jnp.int32, (1, 1, LANES), 2) < HEAD_DIM
    kkt, rt = lhs[:, :c], lhs[:, c:]
    main = _bmm(jnp.concatenate([jnp.where(even, kkt, zero), jnp.where(even, rt, zero),
                                 jnp.where(even, zero, rt), jnp.where(even, zero, kkt)], axis=1), rk)
    yield
    sbd = s_ref[...]
    ls = _bmm(lhs, jnp.swapaxes(sbd, 1, 2))
    yield
    top_odd = pltpu.roll(main[:, 3 * c:], HEAD_DIM, 2)
    main = main[:, :3 * c].astype(_BF)
    top = _by_direction(jnp.concatenate([main[:, :c], top_odd.astype(_BF)], axis=1), mt_ref)
    bot = _by_direction(main[:, c:], mb_ref)
    row_even = lax.broadcasted_iota(jnp.int32, (1, 2 * c, LANES), 1) < c
    is_x = row_even == even
    vv = jnp.concatenate([v, v], axis=1)
    out = _bmm(jnp.where(is_x, top, zero), vv)
    yield
    def both_heads(o):
        return jnp.where(even, o[:, :c], o[:, c:])

    def operand(x, n_part):
        xb = x.astype(_BF)
        return jnp.where(is_x, jnp.concatenate([xb, xb], axis=1), n_part)

    x = ls[:, :c] + both_heads(out)
    zb = operand(x, top)
    n_steps = int(math.log2(c))
    for i in range(n_steps):
        out = _bmm(jnp.where(is_x, zero, zb), jnp.concatenate([zb[:, c:], zb[:, :c]], axis=1))
        yield
        x = x + both_heads(out)
        if i + 1 < n_steps:
            zb = operand(x, out.astype(_BF))
    vu = jnp.concatenate([v, x.astype(_BF)], axis=1)
    yy = _bmm(bot, vu)
    upd = _bmm_tn(vu, ke_ref[...])
    yield
    y = ls[:, c:] + jnp.where(even, yy[:, :c], yy[:, c:])
    y_refs[0][...] = y[:HEAD_PAIRS]
    y_refs[1][...] = y[HEAD_PAIRS:]
    blk_row = lax.broadcasted_iota(jnp.int32, (1, LANES, LANES), 1) < HEAD_DIM
    blk_col = lax.broadcasted_iota(jnp.int32, (1, LANES, LANES), 2) < HEAD_DIM
    s_ref[...] = jnp.where(blk_row == blk_col, sbd * dt_ref[...] + upd, 0.0)


def _scan_chunk_index(d, s):
    return (s + LAT_CHUNKS) % N_CHUNKS if d == 0 else N_CHUNKS - 1 - s


def _wkv_scan(zr, wup, w0, aup, a0, k_k, k_a, m_incl, m_top, m_bot):
    batch = zr.shape[0]
    c = SCAN_CHUNK

    def prep_chunk(d, s):
        return _scan_chunk_index(d, jnp.minimum(s, N_CHUNKS - 1))

    def solve_chunk(d, s):
        return _scan_chunk_index(d, jnp.maximum(s - 1, 0))

    def tokens(d):
        assert TILE_W % 2 == 0 and TILE_A == TILE_W + 1
        return [pl.BlockSpec((None, TILE_W, c, LANES), lambda b, s: (b, 0, prep_chunk(d, s), 0)),
                pl.BlockSpec((None, 2, c, LANES), lambda b, s: (b, TILE_W // 2, prep_chunk(d, s), 0))]

    def whole(*shape):
        return pl.BlockSpec(shape, lambda b, s: (0,) * len(shape))

    def y_spec(d):
        return pl.BlockSpec((None, HEAD_PAIRS, c, LANES), lambda b, s: (b, 0, solve_chunk(d, s), 0))

    both = 2 * HEAD_PAIRS
    slot = [pltpu.VMEM((both, 2 * c, LANES), _BF), pltpu.VMEM((both, 2 * c, LANES), _BF),
            pltpu.VMEM((both, c, LANES), _BF), pltpu.VMEM((both, 2 * c, LANES), _BF),
            pltpu.VMEM((both, 1, LANES), _F32)]
    assert len(slot) == N_SLOT_REFS
    y_shape = jax.ShapeDtypeStruct((batch, HEAD_PAIRS, T_ALL, LANES), _F32)
    return pl.pallas_call(
        _scan_kernel,
        grid=(batch, N_CHUNKS + 1),
        in_specs=tokens(0) + tokens(1) + [
            whole(2, LANES, D_MODEL), whole(2, 1, D_MODEL),
            whole(2, LANES, D_MODEL), whole(2, 1, D_MODEL),
            whole(1, D_MODEL), whole(1, D_MODEL),
            whole(2, c, c), whole(2, 2 * c, 2 * c), whole(2, 2 * c, 2 * c),
        ],
        out_specs=[y_spec(0), y_spec(1)],
        out_shape=[y_shape, y_shape],
        scratch_shapes=[pltpu.VMEM((both, LANES, LANES), _F32)] + slot * 2,
        compiler_params=_params("arbitrary", "arbitrary"),
        name="wkv_scan",
    )(*([zr] * 4), wup, w0, aup, a0, k_k, k_a, m_incl, m_top, m_bot)


READOUT_PAIRS = 2


def _rwkv_readout(yf_ref, yb_ref, r_ref, k_ref, v_ref, gz_ref, gup_ref, gng_ref, gnb_ref, rk_ref,
                  wro_ref):
    gate = _mm(_sigmoid(gz_ref[...].astype(_F32)), gup_ref[...])
    acc = jnp.zeros((ROW_TILE, D_MODEL), _F32)
    for piece in range(HEAD_PAIRS // READOUT_PAIRS):
        parts = []
        for hp in range(piece * READOUT_PAIRS, (piece + 1) * READOUT_PAIRS):
            cols = slice(hp * LANES, (hp + 1) * LANES)
            y = yf_ref[hp] + yb_ref[hp]
            mean = _head_sum(y) * (1.0 / HEAD_DIM)
            yc = y - mean
            var = _head_sum(yc * yc) * (1.0 / HEAD_DIM)
            yn = yc * lax.rsqrt(var + GN_EPS) * gng_ref[:, cols] + gnb_ref[:, cols]
            bonus = (_head_sum(r_ref[hp].astype(_F32) * k_ref[hp].astype(_F32) * rk_ref[:, cols])
                     * v_ref[hp].astype(_F32))
            parts.append(((yn + bonus) * gate[:, cols]).astype(_BF))
        rows = slice(piece * READOUT_PAIRS * LANES, (piece + 1) * READOUT_PAIRS * LANES)
        acc = acc + _mm(jnp.concatenate(parts, axis=1), wro_ref[rows, :])
    return acc


_H_SLOT = GRID_W + 16
_H_LEAD = 16
_V_PAD = CONV_HALF * GRID_W
_CTX_LEAD = 16


def _conv_taps(pad_ref, w, base, length, stride):
    acc = None
    for j in range(CONV_WIDTH):
        term = pad_ref[pl.ds(base + (j - CONV_HALF) * stride, length), :] * w[j:j + 1]
        acc = term if acc is None else acc + term
    return acc


def _glu(zv_ref, zg_ref, rows):
    return zv_ref[rows, :].astype(_F32) * _sigmoid(zg_ref[rows, :].astype(_F32))


def _conv_kernel(horizontal, zv_ref, zg_ref, w_ref, b_ref, o_ref, lat_ref, ctx_ref):
    w = w_ref[...]
    bias = b_ref[...]
    lat_ref[...] = jnp.zeros_like(lat_ref)
    ctx_ref[...] = jnp.zeros_like(ctx_ref)
    ctx_ref[pl.ds(_CTX_LEAD, CTX_LEN), :] = (
        _glu(zv_ref, zg_ref, pl.ds(SEQ, CTX_LEN)))
    o_ref[pl.ds(SEQ, CTX_LEN), :] = _conv_taps(ctx_ref, w, _CTX_LEAD, CTX_LEN, 1) + bias
    if horizontal:
        for row in range(GRID_H):
            src = pl.ds(row * GRID_W, GRID_W)
            lat_ref[pl.ds(_H_LEAD + row * _H_SLOT, GRID_W), :] = (
                _glu(zv_ref, zg_ref, src))
        for row in range(GRID_H):
            o_ref[pl.ds(row * GRID_W, GRID_W), :] = (
                _conv_taps(lat_ref, w, _H_LEAD + row * _H_SLOT, GRID_W, 1) + bias)
    else:
        lat_ref[pl.ds(_V_PAD, SEQ), :] = (
            _glu(zv_ref, zg_ref, pl.ds(0, SEQ)))
        blk = 4 * GRID_W
        for i in range(SEQ // blk):
            o_ref[pl.ds(i * blk, blk), :] = _conv_taps(lat_ref, w, _V_PAD + i * blk, blk, GRID_W) + bias


def _conformer_conv(z, dw, dw_b, horizontal):
    batch = z.shape[0]
    first = Z_CONV // LANES
    n_tiles = CONV_DIM // LANES
    lat_rows = (_H_LEAD + GRID_H * _H_SLOT) if horizontal else (SEQ + 2 * _V_PAD)
    return pl.pallas_call(
        functools.partial(_conv_kernel, horizontal),
        grid=(batch, n_tiles),
        in_specs=[
            pl.BlockSpec((None, T_ALL, LANES), lambda b, j: (b, 0, first + j)),
            pl.BlockSpec((None, T_ALL, LANES), lambda b, j: (b, 0, first + n_tiles + j)),
            pl.BlockSpec((CONV_WIDTH, LANES), lambda b, j: (0, j)),
            pl.BlockSpec((1, LANES), lambda b, j: (0, j)),
        ],
        out_specs=pl.BlockSpec((None, T_ALL, LANES), lambda b, j: (b, 0, j)),
        out_shape=jax.ShapeDtypeStruct((batch, T_ALL, CONV_DIM), _F32),
        scratch_shapes=[pltpu.VMEM((lat_rows, LANES), _F32),
                        pltpu.VMEM((CTX_LEN + 2 * _CTX_LEAD, LANES), _F32)],
        compiler_params=_params("arbitrary", "arbitrary"),
        name="conformer_conv_h" if horizontal else "conformer_conv_v",
    )(z, z, dw, dw_b)


def _sgu_kernel(zu_ref, zv_ref, lng_ref, lnb_ref, ws_ref, bs_ref, o_ref):
    v = _standardize(_gelu_tanh(zv_ref[...].astype(_F32)), LN_EPS) * lng_ref[...] + lnb_ref[...]
    v = v.astype(_BF)
    first_half = lax.broadcasted_iota(jnp.int32, (SGU_CHUNK, LANES), 1) < SGU_GROUP_DIM
    for n in range(ROW_TILE // SGU_CHUNK):
        rows = slice(n * SGU_CHUNK, (n + 1) * SGU_CHUNK)
        tiles = []
        for t in range(SGU_DIM // LANES):
            vt = v[rows, t * LANES:(t + 1) * LANES]
            tiles.append(jnp.where(first_half, _mm(ws_ref[2 * t], vt), _mm(ws_ref[2 * t + 1], vt)))
        mixed = jnp.concatenate(tiles, axis=1) + bs_ref[...]
        o_ref[rows, :] = (_gelu_tanh(zu_ref[rows, :].astype(_F32)) * mixed).astype(o_ref.dtype)


def _sgu(z, ln_g, ln_b, w_s, b_tile):
    batch = z.shape[0]
    first = Z_SGU // SGU_DIM
    vec = pl.BlockSpec((1, SGU_DIM), lambda b, i: (0, 0))
    return pl.pallas_call(
        _sgu_kernel,
        grid=(batch, N_ROW_TILES),
        in_specs=[
            pl.BlockSpec((None, ROW_TILE, SGU_DIM), lambda b, i: (b, i, first)),
            pl.BlockSpec((None, ROW_TILE, SGU_DIM), lambda b, i: (b, i, first + 1)),
            vec, vec,
            pl.BlockSpec((SGU_GROUPS, SGU_CHUNK, SGU_CHUNK), lambda b, i: (0, 0, 0)),
            pl.BlockSpec((SGU_CHUNK, SGU_DIM), lambda b, i: (0, 0)),
        ],
        out_specs=pl.BlockSpec((None, ROW_TILE, SGU_DIM), lambda b, i: (b, i, 0)),
        out_shape=jax.ShapeDtypeStruct((batch, T_ALL, SGU_DIM), _BF),
        compiler_params=_params("arbitrary", "arbitrary"),
        name="sgu",
    )(z, z, ln_g, ln_b, w_s, b_tile)


def _merge_kernel(x_ref, yf_ref, yb_ref, r_ref, k_ref, v_ref, gz_ref, cv_ref, c_ref,
                  g0_ref, g1_ref, g2_ref, gate_ref, gup_ref, gng_ref, gnb_ref, rk_ref, lng_ref, lnb_ref,
                  wro_ref, wco_ref, wso_ref, wm_ref, o_ref):
    c = _mm(c_ref[...], wso_ref[...])
    cb = _silu(_standardize(cv_ref[...], LN_EPS) * lng_ref[...] + lnb_ref[...])
    b = _mm(cb, wco_ref[...])
    a = _rwkv_readout(yf_ref, yb_ref, r_ref, k_ref, v_ref, gz_ref, gup_ref, gng_ref, gnb_ref, rk_ref,
                      wro_ref)
    m = (a * _sigmoid(g0_ref[...].astype(_F32)) + b * _sigmoid(g1_ref[...].astype(_F32))
         + c * _sigmoid(g2_ref[...].astype(_F32)))
    o_ref[...] = x_ref[...] + gate_ref[...] * _mm(m, wm_ref[...])


def _merge(x_all, y_fwd, y_bwd, zr, conv, c_pre, z, gate1, g_up, gn_g, gn_b, r_k, ln_g, ln_b,
           w_ro, w_co, w_so, w_m):
    batch = x_all.shape[0]

    def rows(width, col=0):
        return pl.BlockSpec((None, ROW_TILE, width), lambda b, i: (b, i, col))

    def group(idx):
        return pl.BlockSpec((None, HEAD_PAIRS, ROW_TILE, LANES), lambda b, i: (b, idx, i, 0))

    def whole(shape):
        return pl.BlockSpec(shape, lambda b, i: (0, 0))

    vec = whole((1, D_MODEL))
    return pl.pallas_call(
        _merge_kernel,
        grid=(batch, N_ROW_TILES),
        in_specs=[
            rows(D_MODEL), group(0), group(0), group(0), group(1), group(2),
            pl.BlockSpec((None, None, ROW_TILE, LANES), lambda b, i: (b, TILE_G, i, 0)),
            rows(CONV_DIM), rows(SGU_DIM),
            rows(D_MODEL, 0), rows(D_MODEL, 1), rows(D_MODEL, 2),
            _mod_spec(2), whole((GATE_LORA, D_MODEL)), vec, vec, vec,
            whole((1, CONV_DIM)), whole((1, CONV_DIM)),
            whole((D_MODEL, D_MODEL)), whole((CONV_DIM, D_MODEL)), whole((SGU_DIM, D_MODEL)),
            whole((D_MODEL, D_MODEL)),
        ],
        out_specs=rows(D_MODEL),
        out_shape=jax.ShapeDtypeStruct((batch, T_ALL, D_MODEL), _F32),
        compiler_params=_params("arbitrary", "arbitrary"),
        name="merge",
    )(x_all, y_fwd, y_bwd, zr, zr, zr, zr, conv, c_pre, z, z, z, gate1, g_up, gn_g, gn_b, r_k,
      ln_g, ln_b, w_ro, w_co, w_so, w_m)


def _ffn_kernel(last, x_ref, g_ref, sh_ref, sc_ref, gate_ref, win_ref, wout_ref, fg_ref, o_ref):
    x = x_ref[...]
    h = _rms_mod(x, g_ref[...], sh_ref[...], sc_ref[...]).astype(_BF)
    def gate_up(f):
        cols = slice(f * FF_CHUNK, (f + 1) * FF_CHUNK)
        up_cols = slice(D_FF + f * FF_CHUNK, D_FF + (f + 1) * FF_CHUNK)
        return _mm(h, win_ref[:, cols]), _mm(h, win_ref[:, up_cols])

    n_chunks = D_FF // FF_CHUNK
    acc = jnp.zeros((ROW_TILE, D_MODEL), _F32)
    pending = gate_up(0)
    for f in range(n_chunks):
        g, u = pending
        if f + 1 < n_chunks:
            pending = gate_up(f + 1)
        acc = acc + _mm(_silu(g) * u, wout_ref[f * FF_CHUNK:(f + 1) * FF_CHUNK, :])
    out = x + gate_ref[...] * acc
    if last:
        out = out * lax.rsqrt(jnp.mean(out * out, -1, keepdims=True) + NORM_EPS) * fg_ref[...]
    o_ref[...] = out


def _ffn(x_all, g, shift, scale, gate2, w_in, w_out, final_g, last):
    batch = x_all.shape[0]
    rows = pl.BlockSpec((None, ROW_TILE, D_MODEL), lambda b, i: (b, i, 0))
    vec = pl.BlockSpec((1, D_MODEL), lambda b, i: (0, 0))
    return pl.pallas_call(
        functools.partial(_ffn_kernel, last),
        grid=(batch, LAT_ROW_TILES if last else N_ROW_TILES),
        in_specs=[
            rows, vec,
            _mod_spec(2), _mod_spec(2), _mod_spec(2),
            pl.BlockSpec((D_MODEL, 2 * D_FF), lambda b, i: (0, 0)),
            pl.BlockSpec((D_FF, D_MODEL), lambda b, i: (0, 0)),
            vec,
        ],
        out_specs=rows,
        out_shape=jax.ShapeDtypeStruct((batch, SEQ if last else T_ALL, D_MODEL), _F32),
        compiler_params=_params("arbitrary", "arbitrary"),
        name="swiglu_final" if last else "swiglu",
    )(x_all, g, shift, scale, gate2, w_in, w_out, final_g)


def _split_w_in(w):
    off_conv = RWKV_COLS
    off_gate = off_conv + 2 * CONV_DIM + 2 * SGU_DIM
    return jnp.concatenate([w[:, off_gate:], w[:, off_conv:off_gate]], axis=1), w[:, :off_conv]


def _direction_padded(w_up):
    zero = jnp.zeros_like(w_up[0])
    return jnp.stack([jnp.concatenate([w_up[0], zero], 0), jnp.concatenate([zero, w_up[1]], 0)])


def kernel(x, c, ctx, c_ctx, w_mod, b_mod, norm1_g, norm2_g, w_in, rwkv_shift, rwkv_w0, rwkv_w_up, rwkv_a0, rwkv_a_up, rwkv_g_up, rwkv_k_k, rwkv_k_a, rwkv_r_k, rwkv_gn_g, rwkv_gn_b, rwkv_out, conv_dw, conv_dw_b, conv_ln_g, conv_ln_b, conv_out, sgu_ln_g, sgu_ln_b, sgu_w, sgu_b, sgu_out, w_merge, ffn_w_in, ffn_w_out, final_norm_g):
    batch = x.shape[0]
    depth = w_mod.shape[0]
    assert x.shape[1:] == (SEQ, D_MODEL) and ctx.shape[1:] == (CTX_LEN, D_MODEL)

    rows = -(-(batch + 1) // 8) * 8
    cvec = jnp.zeros((rows, D_MODEL), _F32).at[:batch].set(c).at[batch].set(c_ctx)
    mod = _modulation(cvec, w_mod.astype(_BF), b_mod[:, None, :])
    mod_lat = mod[:, :batch].reshape(depth, batch, 6, D_MODEL)
    mod_ctx = jnp.broadcast_to(mod[:, batch].reshape(depth, 1, 6, D_MODEL), mod_lat.shape)
    mod_tab = jnp.stack([mod_lat, mod_ctx], axis=2).transpose(0, 3, 1, 2, 4)
    mod_tab = mod_tab.reshape(depth, 6, 2 * batch, 1, D_MODEL)

    idx = jnp.arange(SCAN_CHUNK)
    incl = jnp.stack([idx[:, None] >= idx[None, :], idx[:, None] <= idx[None, :]]).astype(_F32)
    strict = jnp.stack([idx[:, None] > idx[None, :], idx[:, None] < idx[None, :]]).astype(_F32)
    m_top = jnp.concatenate([jnp.concatenate([strict, -strict], 2),
                             jnp.concatenate([-strict, strict], 2)], 1)
    m_bot = jnp.concatenate([jnp.concatenate([incl, -incl], 2)] * 2, 1).astype(_BF)
    m_top = m_top.astype(_BF)
    m_incl = incl.astype(_BF)

    x_all = jnp.concatenate([x, ctx], axis=1)
    for l in range(depth):
        sh1, sc1, g1, sh2, sc2, g2 = (mod_tab[l, i] for i in range(6))
        w_z, w_r = _split_w_in(w_in[l])
        z = _in_projection(x_all, norm1_g[l][None], sh1, sc1, w_z.astype(_BF))
        zr = _in_projection_rwkv(x_all, norm1_g[l][None], sh1, sc1, w_r.astype(_BF), rwkv_shift[l])
        y_fwd, y_bwd = _wkv_scan(zr, _direction_padded(rwkv_w_up[l]).astype(_BF), rwkv_w0[l][:, None, :],
                      _direction_padded(rwkv_a_up[l]).astype(_BF), rwkv_a0[l][:, None, :],
                      rwkv_k_k[l][None], rwkv_k_a[l][None], m_incl, m_top, m_bot)
        conv = _conformer_conv(z, conv_dw[l], conv_dw_b[l][None], l % 2 == 0)
        b_tile = jnp.repeat(sgu_b[l].T, SGU_GROUP_DIM, axis=1)
        c_pre = _sgu(z, sgu_ln_g[l][None], sgu_ln_b[l][None], sgu_w[l].astype(_BF), b_tile)
        x_all = _merge(x_all, y_fwd, y_bwd, zr, conv, c_pre, z, g1, rwkv_g_up[l].astype(_BF),
                       rwkv_gn_g[l][None], rwkv_gn_b[l][None], rwkv_r_k[l][None],
                       conv_ln_g[l][None], conv_ln_b[l][None],
                       rwkv_out[l].astype(_BF), conv_out[l].astype(_BF), sgu_out[l].astype(_BF),
                       w_merge[l].astype(_BF))
        x_all = _ffn(x_all, norm2_g[l][None], sh2, sc2, g2, ffn_w_in[l].astype(_BF),
                     ffn_w_out[l].astype(_BF), final_norm_g[None], l == depth - 1)
    return x_all
```

```python
import functools
import itertools
import math

import jax
import jax.numpy as jnp
from jax import lax
from jax.experimental import pallas as pl
from jax.experimental.pallas import tpu as pltpu

D_MODEL = 1024
SEQ = 2048
CTX_LEN = 256
T_ALL = SEQ + CTX_LEN
GRID_W = 64
GRID_H = SEQ // GRID_W

HEAD_DIM = 64
HEADS = D_MODEL // HEAD_DIM
HEAD_PAIRS = HEADS // 2
DECAY_LORA = 64
ICLR_LORA = 64
GATE_LORA = 128
CONV_DIM = D_MODEL // 2
CONV_WIDTH = 31
CONV_HALF = CONV_WIDTH // 2
SGU_DIM = D_MODEL // 2
SGU_GROUPS = 8
SGU_GROUP_DIM = SGU_DIM // SGU_GROUPS
SGU_CHUNK = 128
D_FF = ((8 * D_MODEL // 3 + 255) // 256) * 256
NORM_EPS = 1e-6
LN_EPS = 1e-5
GN_EPS = 64e-5

RWKV_COLS = 3 * D_MODEL + 2 * DECAY_LORA + 2 * ICLR_LORA + GATE_LORA

LANES = 128
Z_GATE = 0
Z_CONV = 3 * D_MODEL
Z_SGU = Z_CONV + 2 * CONV_DIM
Z_COLS = Z_SGU + 2 * SGU_DIM
RWKV_TILES = RWKV_COLS // LANES
TILE_W = 3 * HEAD_PAIRS
TILE_A = TILE_W + 1
TILE_G = TILE_A + 1

ROW_TILE = 256
N_ROW_TILES = T_ALL // ROW_TILE
LAT_ROW_TILES = SEQ // ROW_TILE
SCAN_CHUNK = 64
N_CHUNKS = T_ALL // SCAN_CHUNK
LAT_CHUNKS = SEQ // SCAN_CHUNK
FF_CHUNK = 256

VMEM_LIMIT = 56 * 1024 * 1024

_BF = jnp.bfloat16
_F32 = jnp.float32


def _mm(a, b):
    return jnp.dot(a.astype(_BF), b.astype(_BF), preferred_element_type=_F32)


def _sigmoid(x):
    return jax.nn.sigmoid(x)


def _silu(x):
    return x * jax.nn.sigmoid(x)


def _gelu_tanh(x):
    return 0.5 * x * (1.0 + jnp.tanh(math.sqrt(2.0 / math.pi) * (x + 0.044715 * (x * x * x))))


def _standardize(x, eps):
    xc = x - jnp.mean(x, -1, keepdims=True)
    return xc * lax.rsqrt(jnp.mean(xc * xc, -1, keepdims=True) + eps)


def _rms_mod(x, g, shift, scale):
    y = x * lax.rsqrt(jnp.mean(x * x, -1, keepdims=True) + NORM_EPS) * g
    return y * (1.0 + scale) + shift


def _params(*sem):
    return pltpu.CompilerParams(dimension_semantics=sem, vmem_limit_bytes=VMEM_LIMIT)


def _mod_kernel(c_ref, w_ref, b_ref, o_ref):
    o_ref[...] = _mm(_silu(c_ref[...]), w_ref[...]) + b_ref[...]


def _modulation(cvec, w_mod, b_mod):
    depth = w_mod.shape[0]
    rows = cvec.shape[0]
    return pl.pallas_call(
        _mod_kernel,
        grid=(depth, 6),
        in_specs=[
            pl.BlockSpec((rows, D_MODEL), lambda l, j: (0, 0)),
            pl.BlockSpec((None, D_MODEL, D_MODEL), lambda l, j: (l, 0, j)),
            pl.BlockSpec((None, 1, D_MODEL), lambda l, j: (l, 0, j)),
        ],
        out_specs=pl.BlockSpec((None, rows, D_MODEL), lambda l, j: (l, 0, j)),
        out_shape=jax.ShapeDtypeStruct((depth, rows, 6 * D_MODEL), _F32),
        compiler_params=_params("arbitrary", "arbitrary"),
        name="modulation",
    )(cvec, w_mod, b_mod)


def _mod_spec(grid_rank):
    if grid_rank == 2:
        return pl.BlockSpec((None, 1, D_MODEL), lambda b, i: (2 * b + i // LAT_ROW_TILES, 0, 0))
    return pl.BlockSpec((None, 1, D_MODEL), lambda j, b, i: (2 * b + i // LAT_ROW_TILES, 0, 0))


def _inproj_kernel(x_ref, g_ref, sh_ref, sc_ref, w_ref, o_ref):
    h = _rms_mod(x_ref[...], g_ref[...], sh_ref[...], sc_ref[...])
    o_ref[...] = _mm(h, w_ref[...]).astype(o_ref.dtype)


def _in_projection(x_all, g, shift, scale, w):
    batch = x_all.shape[0]
    n_col = 1
    tn = Z_COLS // n_col
    return pl.pallas_call(
        _inproj_kernel,
        grid=(n_col, batch, N_ROW_TILES),
        in_specs=[
            pl.BlockSpec((None, ROW_TILE, D_MODEL), lambda j, b, i: (b, i, 0)),
            pl.BlockSpec((1, D_MODEL), lambda j, b, i: (0, 0)),
            _mod_spec(3),
            _mod_spec(3),
            pl.BlockSpec((D_MODEL, tn), lambda j, b, i: (0, j)),
        ],
        out_specs=pl.BlockSpec((None, ROW_TILE, tn), lambda j, b, i: (b, i, j)),
        out_shape=jax.ShapeDtypeStruct((batch, T_ALL, Z_COLS), _BF),
        compiler_params=_params("arbitrary", "arbitrary", "arbitrary"),
        name="in_projection",
    )(x_all, g, shift, scale, w)


HALO = 8
HALO_BLOCKS = ROW_TILE // HALO
SHIFT_TILES = 4


def _inproj_shift_kernel(x_ref, xp_ref, xn_ref, g_ref, sh_ref, sc_ref, w_ref, ws_ref, o_ref):
    i = pl.program_id(1)
    starts = (i == 0) | (i == LAT_ROW_TILES)
    ends = (i == LAT_ROW_TILES - 1) | (i == N_ROW_TILES - 1)
    g, sh, sc = g_ref[...], sh_ref[...], sc_ref[...]
    h_prev = _rms_mod(xp_ref[...], g, sh, sc) * jnp.where(starts, 0.0, 1.0)
    h_next = _rms_mod(xn_ref[...], g, sh, sc) * jnp.where(ends, 0.0, 1.0)
    h = jnp.concatenate([h_prev, _rms_mod(x_ref[...], g, sh, sc), h_next], axis=0).astype(_BF)
    n_rows = ROW_TILE + 2 * HALO
    inner = slice(HALO, HALO + ROW_TILE)
    for j0 in range(0, RWKV_TILES, SHIFT_TILES):
        n_tiles = min(SHIFT_TILES, RWKV_TILES - j0)
        cols = slice(j0 * LANES, (j0 + n_tiles) * LANES)
        z = jnp.dot(h, w_ref[:, cols], preferred_element_type=_F32)
        w = ws_ref[:, cols]
        zr = (pltpu.roll(z, 1, 0)[inner] * w[0:1] + z[inner] * w[1:2]
              + pltpu.roll(z, n_rows - 1, 0)[inner] * w[2:3])
        for t in range(n_tiles):
            o_ref[j0 + t] = zr[:, t * LANES:(t + 1) * LANES].astype(o_ref.dtype)


def _in_projection_rwkv(x_all, g, shift, scale, w, w_shift):
    batch = x_all.shape[0]
    last_block = T_ALL // HALO - 1
    vec = pl.BlockSpec((1, D_MODEL), lambda b, i: (0, 0))
    return pl.pallas_call(
        _inproj_shift_kernel,
        grid=(batch, N_ROW_TILES),
        in_specs=[
            pl.BlockSpec((None, ROW_TILE, D_MODEL), lambda b, i: (b, i, 0)),
            pl.BlockSpec((None, HALO, D_MODEL),
                         lambda b, i: (b, jnp.maximum(i * HALO_BLOCKS - 1, 0), 0)),
            pl.BlockSpec((None, HALO, D_MODEL),
                         lambda b, i: (b, jnp.minimum((i + 1) * HALO_BLOCKS, last_block), 0)),
            vec, _mod_spec(2), _mod_spec(2),
            pl.BlockSpec((D_MODEL, RWKV_COLS), lambda b, i: (0, 0)),
            pl.BlockSpec((3, RWKV_COLS), lambda b, i: (0, 0)),
        ],
        out_specs=pl.BlockSpec((None, RWKV_TILES, ROW_TILE, LANES), lambda b, i: (b, 0, i, 0)),
        out_shape=jax.ShapeDtypeStruct((batch, RWKV_TILES, T_ALL, LANES), _BF),
        compiler_params=_params("arbitrary", "arbitrary"),
        name="in_projection_rwkv",
    )(x_all, x_all, x_all, g, shift, scale, w, w_shift)


def _head_sum(x):
    lane = lax.broadcasted_iota(jnp.int32, x.shape, x.ndim - 1)
    lo = jnp.sum(jnp.where(lane < HEAD_DIM, x, 0.0), -1, keepdims=True)
    tot = jnp.sum(x, -1, keepdims=True)
    return jnp.where(lane < HEAD_DIM, lo, tot - lo)


def _bmm(a, b):
    return lax.dot_general(a.astype(_BF), b.astype(_BF), (((2,), (1,)), ((0,), (0,))),
                           preferred_element_type=_F32)


def _bmm_tn(a, b):
    return lax.dot_general(a.astype(_BF), b.astype(_BF), (((1,), (1,)), ((0,), (0,))),
                           preferred_element_type=_F32)


N_SLOT_REFS = 5


def _scan_kernel(*refs):
    tokens = (refs[0:2], refs[2:4])
    wup_ref, w0_ref, aup_ref, a0_ref, kk_ref, ka_ref, mi_ref, mt_ref, mb_ref = refs[4:13]
    y_refs = refs[13:15]
    s_ref = refs[15]
    slots = (refs[16:16 + N_SLOT_REFS], refs[16 + N_SLOT_REFS:])
    step = pl.program_id(1)

    @pl.when(step == 0)
    def _():
        s_ref[...] = jnp.zeros_like(s_ref)
        for ref in slots[1]:
            ref[...] = jnp.zeros_like(ref)

    def run(prep, solve):
        pieces = itertools.chain.from_iterable(
            _scan_prepare(tokens[d], (wup_ref.at[d], w0_ref.at[d], aup_ref.at[d], a0_ref.at[d],
                                      kk_ref, ka_ref, mi_ref.at[d]), slots[prep], d)
            for d in range(2))
        next(pieces)
        for _ in _scan_solve(slots[solve], mt_ref, mb_ref, y_refs, s_ref):
            next(pieces, None)
        for _ in pieces:
            pass

    @pl.when(step % 2 == 0)
    def _():
        run(0, 1)

    @pl.when(step % 2 == 1)
    def _():
        run(1, 0)


PREP_PAIRS = 8
PREP_PIECES = HEAD_PAIRS // PREP_PAIRS


def _scan_prepare(token_refs, param_refs, slot, d):
    rkv_ref, lora_ref = token_refs
    r_ref, k_ref, v_ref = (rkv_ref.at[pl.ds(i * HEAD_PAIRS, HEAD_PAIRS)] for i in range(3))
    lw_ref, la_ref = lora_ref.at[0], lora_ref.at[1]
    wup_ref, w0_ref, aup_ref, a0_ref, kk_ref, ka_ref, mi_ref = param_refs
    lhs_ref, rk_ref, vv_ref, ke_ref, dt_ref = slot
    width = PREP_PAIRS * LANES

    def tiles(x):
        return jnp.stack([x[:, i * LANES:(i + 1) * LANES] for i in range(PREP_PAIRS)])

    lw_act = jnp.tanh(lw_ref[...].astype(_F32))
    la_in = la_ref[...]
    m_incl = mi_ref[...]
    for g in range(PREP_PIECES):
        cols = pl.ds(g * width, width)
        pairs = pl.ds(g * PREP_PAIRS, PREP_PAIRS)
        xw = w0_ref[:, cols] + _mm(lw_act, wup_ref[:, cols])
        ld = -(math.exp(-0.5) * math.log2(math.e)) * _sigmoid(xw)
        a = tiles(_sigmoid(a0_ref[:, cols] + _mm(la_in, aup_ref[:, cols])))
        hi = ld.astype(_BF)
        rem = ld - hi.astype(_F32)
        mid = rem.astype(_BF)
        lo = (rem - mid.astype(_F32)).astype(_BF)
        cl = (jnp.dot(m_incl, hi, preferred_element_type=_F32)
              + jnp.dot(m_incl, mid, preferred_element_type=_F32)
              + jnp.dot(m_incl, lo, preferred_element_type=_F32))
        ld_tot = jnp.sum(ld, 0, keepdims=True)
        dec_tot = jnp.exp2(ld_tot)
        e_in = tiles(jnp.exp2(cl))
        e_out = tiles(jnp.exp2(-cl))
        e_ex = tiles(jnp.exp2(cl - ld))
        e_end = tiles(jnp.exp2(ld_tot - cl))

        r = r_ref[pairs].astype(_F32)
        k = k_ref[pairs].astype(_F32)
        kk = k * tiles(kk_ref[:, cols])
        kk = kk * lax.rsqrt(jnp.maximum(_head_sum(kk * kk), 1e-24))
        k_d = k * (1.0 + (a - 1.0) * tiles(ka_ref[:, cols]))
        kka = kk * a

        rows = pl.ds(d * HEAD_PAIRS + g * PREP_PAIRS, PREP_PAIRS)
        lhs_ref[rows] = jnp.concatenate([kk * e_ex, r * e_in], axis=1).astype(_BF)
        rk_ref[rows] = jnp.swapaxes(jnp.concatenate([k_d * e_out, kka * e_out], axis=1),
                                    1, 2).astype(_BF)
        vv_ref[rows] = v_ref[pairs]
        ke_ref[rows] = jnp.concatenate([k_d * e_end, -(kka * e_end)], axis=1).astype(_BF)
        dt_ref[rows] = tiles(dec_tot)
        yield


def _by_direction(x, m_ref):
    return jnp.concatenate([x[:HEAD_PAIRS] * m_ref[0], x[HEAD_PAIRS:] * m_ref[1]], axis=0)


def _scan_solve(slot, mt_ref, mb_ref, y_refs, s_ref):
    c = SCAN_CHUNK
    lhs_ref, rk_ref, vv_ref, ke_ref, dt_ref = slot
    lhs = lhs_ref[...]
    rk = rk_ref[...]
    v = vv_ref[...]
    zero = jnp.zeros((), _BF)
    even = lax.broadcasted_iota(jnp.int32, (1, 1, LANES), 2) < HEAD_DIM
    kkt, rt = lhs[:, :c], lhs[:, c:]
    main = _bmm(jnp.concatenate([jnp.where(even, kkt, zero), jnp.where(even, rt, zero),
                                 jnp.where(even, zero, rt), jnp.where(even, zero, kkt)], axis=1), rk)
    yield
    sbd = s_ref[...]
    ls = _bmm(lhs, jnp.swapaxes(sbd, 1, 2))
    yield
    top_odd = pltpu.roll(main[:, 3 * c:], HEAD_DIM, 2)
    main = main[:, :3 * c].astype(_BF)
    top = _by_direction(jnp.concatenate([main[:, :c], top_odd.astype(_BF)], axis=1), mt_ref)
    bot = _by_direction(main[:, c:], mb_ref)
    row_even = lax.broadcasted_iota(jnp.int32, (1, 2 * c, LANES), 1) < c
    is_x = row_even == even
    vv = jnp.concatenate([v, v], axis=1)
    out = _bmm(jnp.where(is_x, top, zero), vv)
    yield
    def both_heads(o):
        return jnp.where(even, o[:, :c], o[:, c:])

    def operand(x, n_part):
        xb = x.astype(_BF)
        return jnp.where(is_x, jnp.concatenate([xb, xb], axis=1), n_part)

    x = ls[:, :c] + both_heads(out)
    zb = operand(x, top)
    n_steps = int(math.log2(c))
    for i in range(n_steps):
        out = _bmm(jnp.where(is_x, zero, zb), jnp.concatenate([zb[:, c:], zb[:, :c]], axis=1))
        yield
        x = x + both_heads(out)
        if i + 1 < n_steps:
            zb = operand(x, out.astype(_BF))
    vu = jnp.concatenate([v, x.astype(_BF)], axis=1)
    yy = _bmm(bot, vu)
    upd = _bmm_tn(vu, ke_ref[...])
    yield
    y = ls[:, c:] + jnp.where(even, yy[:, :c], yy[:, c:])
    y_refs[0][...] = y[:HEAD_PAIRS]
    y_refs[1][...] = y[HEAD_PAIRS:]
    blk_row = lax.broadcasted_iota(jnp.int32, (1, LANES, LANES), 1) < HEAD_DIM
    blk_col = lax.broadcasted_iota(jnp.int32, (1, LANES, LANES), 2) < HEAD_DIM
    s_ref[...] = jnp.where(blk_row == blk_col, sbd * dt_ref[...] + upd, 0.0)


def _scan_chunk_index(d, s):
    return (s + LAT_CHUNKS) % N_CHUNKS if d == 0 else N_CHUNKS - 1 - s


def _wkv_scan(zr, wup, w0, aup, a0, k_k, k_a, m_incl, m_top, m_bot):
    batch = zr.shape[0]
    c = SCAN_CHUNK

    def prep_chunk(d, s):
        return _scan_chunk_index(d, jnp.minimum(s, N_CHUNKS - 1))

    def solve_chunk(d, s):
        return _scan_chunk_index(d, jnp.maximum(s - 1, 0))

    def tokens(d):
        assert TILE_W % 2 == 0 and TILE_A == TILE_W + 1
        return [pl.BlockSpec((None, TILE_W, c, LANES), lambda b, s: (b, 0, prep_chunk(d, s), 0)),
                pl.BlockSpec((None, 2, c, LANES), lambda b, s: (b, TILE_W // 2, prep_chunk(d, s), 0))]

    def whole(*shape):
        return pl.BlockSpec(shape, lambda b, s: (0,) * len(shape))

    def y_spec(d):
        return pl.BlockSpec((None, HEAD_PAIRS, c, LANES), lambda b, s: (b, 0, solve_chunk(d, s), 0))

    both = 2 * HEAD_PAIRS
    slot = [pltpu.VMEM((both, 2 * c, LANES), _BF), pltpu.VMEM((both, 2 * c, LANES), _BF),
            pltpu.VMEM((both, c, LANES), _BF), pltpu.VMEM((both, 2 * c, LANES), _BF),
            pltpu.VMEM((both, 1, LANES), _F32)]
    assert len(slot) == N_SLOT_REFS
    y_shape = jax.ShapeDtypeStruct((batch, HEAD_PAIRS, T_ALL, LANES), _F32)
    return pl.pallas_call(
        _scan_kernel,
        grid=(batch, N_CHUNKS + 1),
        in_specs=tokens(0) + tokens(1) + [
            whole(2, LANES, D_MODEL), whole(2, 1, D_MODEL),
            whole(2, LANES, D_MODEL), whole(2, 1, D_MODEL),
            whole(1, D_MODEL), whole(1, D_MODEL),
            whole(2, c, c), whole(2, 2 * c, 2 * c), whole(2, 2 * c, 2 * c),
        ],
        out_specs=[y_spec(0), y_spec(1)],
        out_shape=[y_shape, y_shape],
        scratch_shapes=[pltpu.VMEM((both, LANES, LANES), _F32)] + slot * 2,
        compiler_params=_params("arbitrary", "arbitrary"),
        name="wkv_scan",
    )(*([zr] * 4), wup, w0, aup, a0, k_k, k_a, m_incl, m_top, m_bot)


READOUT_PAIRS = 2


def _rwkv_readout(yf_ref, yb_ref, r_ref, k_ref, v_ref, gz_ref, gup_ref, gng_ref, gnb_ref, rk_ref,
                  wro_ref):
    gate = _mm(_sigmoid(gz_ref[...].astype(_F32)), gup_ref[...])
    acc = jnp.zeros((ROW_TILE, D_MODEL), _F32)
    for piece in range(HEAD_PAIRS // READOUT_PAIRS):
        parts = []
        for hp in range(piece * READOUT_PAIRS, (piece + 1) * READOUT_PAIRS):
            cols = slice(hp * LANES, (hp + 1) * LANES)
            y = yf_ref[hp] + yb_ref[hp]
            mean = _head_sum(y) * (1.0 / HEAD_DIM)
            yc = y - mean
            var = _head_sum(yc * yc) * (1.0 / HEAD_DIM)
            yn = yc * lax.rsqrt(var + GN_EPS) * gng_ref[:, cols] + gnb_ref[:, cols]
            bonus = (_head_sum(r_ref[hp].astype(_F32) * k_ref[hp].astype(_F32) * rk_ref[:, cols])
                     * v_ref[hp].astype(_F32))
            parts.append(((yn + bonus) * gate[:, cols]).astype(_BF))
        rows = slice(piece * READOUT_PAIRS * LANES, (piece + 1) * READOUT_PAIRS * LANES)
        acc = acc + _mm(jnp.concatenate(parts, axis=1), wro_ref[rows, :])
    return acc


_H_SLOT = GRID_W + 16
_H_LEAD = 16
_V_PAD = CONV_HALF * GRID_W
_CTX_LEAD = 16


def _conv_taps(pad_ref, w, base, length, stride):
    acc = None
    for j in range(CONV_WIDTH):
        term = pad_ref[pl.ds(base + (j - CONV_HALF) * stride, length), :] * w[j:j + 1]
        acc = term if acc is None else acc + term
    return acc


def _glu(zv_ref, zg_ref, rows):
    return zv_ref[rows, :].astype(_F32) * _sigmoid(zg_ref[rows, :].astype(_F32))


def _conv_kernel(horizontal, zv_ref, zg_ref, w_ref, b_ref, o_ref, lat_ref, ctx_ref):
    w = w_ref[...]
    bias = b_ref[...]
    lat_ref[...] = jnp.zeros_like(lat_ref)
    ctx_ref[...] = jnp.zeros_like(ctx_ref)
    ctx_ref[pl.ds(_CTX_LEAD, CTX_LEN), :] = (
        _glu(zv_ref, zg_ref, pl.ds(SEQ, CTX_LEN)))
    o_ref[pl.ds(SEQ, CTX_LEN), :] = _conv_taps(ctx_ref, w, _CTX_LEAD, CTX_LEN, 1) + bias
    if horizontal:
        for row in range(GRID_H):
            src = pl.ds(row * GRID_W, GRID_W)
            lat_ref[pl.ds(_H_LEAD + row * _H_SLOT, GRID_W), :] = (
                _glu(zv_ref, zg_ref, src))
        for row in range(GRID_H):
            o_ref[pl.ds(row * GRID_W, GRID_W), :] = (
                _conv_taps(lat_ref, w, _H_LEAD + row * _H_SLOT, GRID_W, 1) + bias)
    else:
        lat_ref[pl.ds(_V_PAD, SEQ), :] = (
            _glu(zv_ref, zg_ref, pl.ds(0, SEQ)))
        blk = 4 * GRID_W
        for i in range(SEQ // blk):
            o_ref[pl.ds(i * blk, blk), :] = _conv_taps(lat_ref, w, _V_PAD + i * blk, blk, GRID_W) + bias


def _conformer_conv(z, dw, dw_b, horizontal):
    batch = z.shape[0]
    first = Z_CONV // LANES
    n_tiles = CONV_DIM // LANES
    lat_rows = (_H_LEAD + GRID_H * _H_SLOT) if horizontal else (SEQ + 2 * _V_PAD)
    return pl.pallas_call(
        functools.partial(_conv_kernel, horizontal),
        grid=(batch, n_tiles),
        in_specs=[
            pl.BlockSpec((None, T_ALL, LANES), lambda b, j: (b, 0, first + j)),
            pl.BlockSpec((None, T_ALL, LANES), lambda b, j: (b, 0, first + n_tiles + j)),
            pl.BlockSpec((CONV_WIDTH, LANES), lambda b, j: (0, j)),
            pl.BlockSpec((1, LANES), lambda b, j: (0, j)),
        ],
        out_specs=pl.BlockSpec((None, T_ALL, LANES), lambda b, j: (b, 0, j)),
        out_shape=jax.ShapeDtypeStruct((batch, T_ALL, CONV_DIM), _F32),
        scratch_shapes=[pltpu.VMEM((lat_rows, LANES), _F32),
                        pltpu.VMEM((CTX_LEN + 2 * _CTX_LEAD, LANES), _F32)],
        compiler_params=_params("arbitrary", "arbitrary"),
        name="conformer_conv_h" if horizontal else "conformer_conv_v",
    )(z, z, dw, dw_b)


SGU_ROW_TILE = 6 * SGU_CHUNK
assert T_ALL % SGU_ROW_TILE == 0 and SEQ % SGU_CHUNK == 0


def _sgu_kernel(zu_ref, zv_ref, lng_ref, lnb_ref, ws_ref, bs_ref, o_ref):
    v = _standardize(_gelu_tanh(zv_ref[...].astype(_F32)), LN_EPS) * lng_ref[...] + lnb_ref[...]
    v = v.astype(_BF)
    first_half = lax.broadcasted_iota(jnp.int32, (SGU_CHUNK, LANES), 1) < SGU_GROUP_DIM
    for n in range(SGU_ROW_TILE // SGU_CHUNK):
        rows = slice(n * SGU_CHUNK, (n + 1) * SGU_CHUNK)
        tiles = []
        for t in range(SGU_DIM // LANES):
            vt = v[rows, t * LANES:(t + 1) * LANES]
            tiles.append(jnp.where(first_half, _mm(ws_ref[2 * t], vt), _mm(ws_ref[2 * t + 1], vt)))
        mixed = jnp.concatenate(tiles, axis=1) + bs_ref[...]
        o_ref[rows, :] = (_gelu_tanh(zu_ref[rows, :].astype(_F32)) * mixed).astype(o_ref.dtype)


def _sgu(z, ln_g, ln_b, w_s, b_tile):
    batch = z.shape[0]
    first = Z_SGU // SGU_DIM
    vec = pl.BlockSpec((1, SGU_DIM), lambda b, i: (0, 0))
    return pl.pallas_call(
        _sgu_kernel,
        grid=(batch, T_ALL // SGU_ROW_TILE),
        in_specs=[
            pl.BlockSpec((None, SGU_ROW_TILE, SGU_DIM), lambda b, i: (b, i, first)),
            pl.BlockSpec((None, SGU_ROW_TILE, SGU_DIM), lambda b, i: (b, i, first + 1)),
            vec, vec,
            pl.BlockSpec((SGU_GROUPS, SGU_CHUNK, SGU_CHUNK), lambda b, i: (0, 0, 0)),
            pl.BlockSpec((SGU_CHUNK, SGU_DIM), lambda b, i: (0, 0)),
        ],
        out_specs=pl.BlockSpec((None, SGU_ROW_TILE, SGU_DIM), lambda b, i: (b, i, 0)),
        out_shape=jax.ShapeDtypeStruct((batch, T_ALL, SGU_DIM), _BF),
        compiler_params=_params("arbitrary", "arbitrary"),
        name="sgu",
    )(z, z, ln_g, ln_b, w_s, b_tile)


def _merge_kernel(x_ref, yf_ref, yb_ref, r_ref, k_ref, v_ref, gz_ref, cv_ref, c_ref,
                  g0_ref, g1_ref, g2_ref, gate_ref, gup_ref, gng_ref, gnb_ref, rk_ref, lng_ref, lnb_ref,
                  wro_ref, wco_ref, wso_ref, wm_ref, o_ref):
    c = _mm(c_ref[...], wso_ref[...])
    cb = _silu(_standardize(cv_ref[...], LN_EPS) * lng_ref[...] + lnb_ref[...])
    b = _mm(cb, wco_ref[...])
    a = _rwkv_readout(yf_ref, yb_ref, r_ref, k_ref, v_ref, gz_ref, gup_ref, gng_ref, gnb_ref, rk_ref,
                      wro_ref)
    m = (a * _sigmoid(g0_ref[...].astype(_F32)) + b * _sigmoid(g1_ref[...].astype(_F32))
         + c * _sigmoid(g2_ref[...].astype(_F32)))
    o_ref[...] = x_ref[...] + gate_ref[...] * _mm(m, wm_ref[...])


def _merge(x_all, y_fwd, y_bwd, zr, conv, c_pre, z, gate1, g_up, gn_g, gn_b, r_k, ln_g, ln_b,
           w_ro, w_co, w_so, w_m):
    batch = x_all.shape[0]

    def rows(width, col=0):
        return pl.BlockSpec((None, ROW_TILE, width), lambda b, i: (b, i, col))

    def group(idx):
        return pl.BlockSpec((None, HEAD_PAIRS, ROW_TILE, LANES), lambda b, i: (b, idx, i, 0))

    def whole(shape):
        return pl.BlockSpec(shape, lambda b, i: (0, 0))

    vec = whole((1, D_MODEL))
    return pl.pallas_call(
        _merge_kernel,
        grid=(batch, N_ROW_TILES),
        in_specs=[
            rows(D_MODEL), group(0), group(0), group(0), group(1), group(2),
            pl.BlockSpec((None, None, ROW_TILE, LANES), lambda b, i: (b, TILE_G, i, 0)),
            rows(CONV_DIM), rows(SGU_DIM),
            rows(D_MODEL, 0), rows(D_MODEL, 1), rows(D_MODEL, 2),
            _mod_spec(2), whole((GATE_LORA, D_MODEL)), vec, vec, vec,
            whole((1, CONV_DIM)), whole((1, CONV_DIM)),
            whole((D_MODEL, D_MODEL)), whole((CONV_DIM, D_MODEL)), whole((SGU_DIM, D_MODEL)),
            whole((D_MODEL, D_MODEL)),
        ],
        out_specs=rows(D_MODEL),
        out_shape=jax.ShapeDtypeStruct((batch, T_ALL, D_MODEL), _F32),
        compiler_params=_params("arbitrary", "arbitrary"),
        name="merge",
    )(x_all, y_fwd, y_bwd, zr, zr, zr, zr, conv, c_pre, z, z, z, gate1, g_up, gn_g, gn_b, r_k,
      ln_g, ln_b, w_ro, w_co, w_so, w_m)


def _ffn_kernel(last, x_ref, g_ref, sh_ref, sc_ref, gate_ref, win_ref, wout_ref, fg_ref, o_ref):
    x = x_ref[...]
    h = _rms_mod(x, g_ref[...], sh_ref[...], sc_ref[...]).astype(_BF)
    def gate_up(f):
        cols = slice(f * FF_CHUNK, (f + 1) * FF_CHUNK)
        up_cols = slice(D_FF + f * FF_CHUNK, D_FF + (f + 1) * FF_CHUNK)
        return _mm(h, win_ref[:, cols]), _mm(h, win_ref[:, up_cols])

    n_chunks = D_FF // FF_CHUNK
    acc = jnp.zeros((ROW_TILE, D_MODEL), _F32)
    pending = gate_up(0)
    for f in range(n_chunks):
        g, u = pending
        if f + 1 < n_chunks:
            pending = gate_up(f + 1)
        acc = acc + _mm(_silu(g) * u, wout_ref[f * FF_CHUNK:(f + 1) * FF_CHUNK, :])
    out = x + gate_ref[...] * acc
    if last:
        out = out * lax.rsqrt(jnp.mean(out * out, -1, keepdims=True) + NORM_EPS) * fg_ref[...]
    o_ref[...] = out


def _ffn(x_all, g, shift, scale, gate2, w_in, w_out, final_g, last):
    batch = x_all.shape[0]
    rows = pl.BlockSpec((None, ROW_TILE, D_MODEL), lambda b, i: (b, i, 0))
    vec = pl.BlockSpec((1, D_MODEL), lambda b, i: (0, 0))
    return pl.pallas_call(
        functools.partial(_ffn_kernel, last),
        grid=(batch, LAT_ROW_TILES if last else N_ROW_TILES),
        in_specs=[
            rows, vec,
            _mod_spec(2), _mod_spec(2), _mod_spec(2),
            pl.BlockSpec((D_MODEL, 2 * D_FF), lambda b, i: (0, 0)),
            pl.BlockSpec((D_FF, D_MODEL), lambda b, i: (0, 0)),
            vec,
        ],
        out_specs=rows,
        out_shape=jax.ShapeDtypeStruct((batch, SEQ if last else T_ALL, D_MODEL), _F32),
        compiler_params=_params("arbitrary", "arbitrary"),
        name="swiglu_final" if last else "swiglu",
    )(x_all, g, shift, scale, gate2, w_in, w_out, final_g)


def _split_w_in(w):
    off_conv = RWKV_COLS
    off_gate = off_conv + 2 * CONV_DIM + 2 * SGU_DIM
    return jnp.concatenate([w[:, off_gate:], w[:, off_conv:off_gate]], axis=1), w[:, :off_conv]


def _direction_padded(w_up):
    zero = jnp.zeros_like(w_up[0])
    return jnp.stack([jnp.concatenate([w_up[0], zero], 0), jnp.concatenate([zero, w_up[1]], 0)])


def kernel(x, c, ctx, c_ctx, w_mod, b_mod, norm1_g, norm2_g, w_in, rwkv_shift, rwkv_w0, rwkv_w_up, rwkv_a0, rwkv_a_up, rwkv_g_up, rwkv_k_k, rwkv_k_a, rwkv_r_k, rwkv_gn_g, rwkv_gn_b, rwkv_out, conv_dw, conv_dw_b, conv_ln_g, conv_ln_b, conv_out, sgu_ln_g, sgu_ln_b, sgu_w, sgu_b, sgu_out, w_merge, ffn_w_in, ffn_w_out, final_norm_g):
    batch = x.shape[0]
    depth = w_mod.shape[0]
    assert x.shape[1:] == (SEQ, D_MODEL) and ctx.shape[1:] == (CTX_LEN, D_MODEL)

    rows = -(-(batch + 1) // 8) * 8
    cvec = jnp.zeros((rows, D_MODEL), _F32).at[:batch].set(c).at[batch].set(c_ctx)
    mod = _modulation(cvec, w_mod.astype(_BF), b_mod[:, None, :])
    mod_lat = mod[:, :batch].reshape(depth, batch, 6, D_MODEL)
    mod_ctx = jnp.broadcast_to(mod[:, batch].reshape(depth, 1, 6, D_MODEL), mod_lat.shape)
    mod_tab = jnp.stack([mod_lat, mod_ctx], axis=2).transpose(0, 3, 1, 2, 4)
    mod_tab = mod_tab.reshape(depth, 6, 2 * batch, 1, D_MODEL)

    idx = jnp.arange(SCAN_CHUNK)
    incl = jnp.stack([idx[:, None] >= idx[None, :], idx[:, None] <= idx[None, :]]).astype(_F32)
    strict = jnp.stack([idx[:, None] > idx[None, :], idx[:, None] < idx[None, :]]).astype(_F32)
    m_top = jnp.concatenate([jnp.concatenate([strict, -strict], 2),
                             jnp.concatenate([-strict, strict], 2)], 1)
    m_bot = jnp.concatenate([jnp.concatenate([incl, -incl], 2)] * 2, 1).astype(_BF)
    m_top = m_top.astype(_BF)
    m_incl = incl.astype(_BF)

    x_all = jnp.concatenate([x, ctx], axis=1)
    for l in range(depth):
        sh1, sc1, g1, sh2, sc2, g2 = (mod_tab[l, i] for i in range(6))
        w_z, w_r = _split_w_in(w_in[l])
        z = _in_projection(x_all, norm1_g[l][None], sh1, sc1, w_z.astype(_BF))
        zr = _in_projection_rwkv(x_all, norm1_g[l][None], sh1, sc1, w_r.astype(_BF), rwkv_shift[l])
        y_fwd, y_bwd = _wkv_scan(zr, _direction_padded(rwkv_w_up[l]).astype(_BF), rwkv_w0[l][:, None, :],
                      _direction_padded(rwkv_a_up[l]).astype(_BF), rwkv_a0[l][:, None, :],
                      rwkv_k_k[l][None], rwkv_k_a[l][None], m_incl, m_top, m_bot)
        conv = _conformer_conv(z, conv_dw[l], conv_dw_b[l][None], l % 2 == 0)
        b_tile = jnp.repeat(sgu_b[l].T, SGU_GROUP_DIM, axis=1)
        c_pre = _sgu(z, sgu_ln_g[l][None], sgu_ln_b[l][None], sgu_w[l].astype(_BF), b_tile)
        x_all = _merge(x_all, y_fwd, y_bwd, zr, conv, c_pre, z, g1, rwkv_g_up[l].astype(_BF),
                       rwkv_gn_g[l][None], rwkv_gn_b[l][None], rwkv_r_k[l][None],
                       conv_ln_g[l][None], conv_ln_b[l][None],
                       rwkv_out[l].astype(_BF), conv_out[l].astype(_BF), sgu_out[l].astype(_BF),
                       w_merge[l].astype(_BF))
        x_all = _ffn(x_all, norm2_g[l][None], sh2, sc2, g2, ffn_w_in[l].astype(_BF),
                     ffn_w_out[l].astype(_BF), final_norm_g[None], l == depth - 1)
    return x_all
```

```python
import functools
import itertools
import math

import jax
import jax.numpy as jnp
from jax import lax
from jax.experimental import pallas as pl
from jax.experimental.pallas import tpu as pltpu

D_MODEL = 1024
SEQ = 2048
CTX_LEN = 256
T_ALL = SEQ + CTX_LEN
GRID_W = 64
GRID_H = SEQ // GRID_W

HEAD_DIM = 64
HEADS = D_MODEL // HEAD_DIM
HEAD_PAIRS = HEADS // 2
DECAY_LORA = 64
ICLR_LORA = 64
GATE_LORA = 128
CONV_DIM = D_MODEL // 2
CONV_WIDTH = 31
CONV_HALF = CONV_WIDTH // 2
SGU_DIM = D_MODEL // 2
SGU_GROUPS = 8
SGU_GROUP_DIM = SGU_DIM // SGU_GROUPS
SGU_CHUNK = 128
D_FF = ((8 * D_MODEL // 3 + 255) // 256) * 256
NORM_EPS = 1e-6
LN_EPS = 1e-5
GN_EPS = 64e-5

RWKV_COLS = 3 * D_MODEL + 2 * DECAY_LORA + 2 * ICLR_LORA + GATE_LORA

LANES = 128
Z_GATE = 0
Z_CONV = 3 * D_MODEL
Z_SGU = Z_CONV + 2 * CONV_DIM
Z_COLS = Z_SGU + 2 * SGU_DIM
RWKV_TILES = RWKV_COLS // LANES
TILE_W = 3 * HEAD_PAIRS
TILE_A = TILE_W + 1
TILE_G = TILE_A + 1

ROW_TILE = 256
N_ROW_TILES = T_ALL // ROW_TILE
LAT_ROW_TILES = SEQ // ROW_TILE
SCAN_CHUNK = 64
N_CHUNKS = T_ALL // SCAN_CHUNK
LAT_CHUNKS = SEQ // SCAN_CHUNK
FF_CHUNK = 256

VMEM_LIMIT = 56 * 1024 * 1024

_BF = jnp.bfloat16
_F32 = jnp.float32


def _mm(a, b):
    return jnp.dot(a.astype(_BF), b.astype(_BF), preferred_element_type=_F32)


def _sigmoid(x):
    return jax.nn.sigmoid(x)


def _silu(x):
    return x * jax.nn.sigmoid(x)


def _gelu_tanh(x):
    return 0.5 * x * (1.0 + jnp.tanh(math.sqrt(2.0 / math.pi) * (x + 0.044715 * (x * x * x))))


def _standardize(x, eps):
    xc = x - jnp.mean(x, -1, keepdims=True)
    return xc * lax.rsqrt(jnp.mean(xc * xc, -1, keepdims=True) + eps)


def _rms_mod(x, g, shift, scale):
    y = x * lax.rsqrt(jnp.mean(x * x, -1, keepdims=True) + NORM_EPS) * g
    return y * (1.0 + scale) + shift


def _params(*sem):
    return pltpu.CompilerParams(dimension_semantics=sem, vmem_limit_bytes=VMEM_LIMIT)


def _mod_kernel(c_ref, w_ref, b_ref, o_ref):
    o_ref[...] = _mm(_silu(c_ref[...]), w_ref[...]) + b_ref[...]


def _modulation(cvec, w_mod, b_mod):
    depth = w_mod.shape[0]
    rows = cvec.shape[0]
    return pl.pallas_call(
        _mod_kernel,
        grid=(depth, 6),
        in_specs=[
            pl.BlockSpec((rows, D_MODEL), lambda l, j: (0, 0)),
            pl.BlockSpec((None, D_MODEL, D_MODEL), lambda l, j: (l, 0, j)),
            pl.BlockSpec((None, 1, D_MODEL), lambda l, j: (l, 0, j)),
        ],
        out_specs=pl.BlockSpec((None, rows, D_MODEL), lambda l, j: (l, 0, j)),
        out_shape=jax.ShapeDtypeStruct((depth, rows, 6 * D_MODEL), _F32),
        compiler_params=_params("arbitrary", "arbitrary"),
        name="modulation",
    )(cvec, w_mod, b_mod)


def _mod_spec(grid_rank):
    if grid_rank == 2:
        return pl.BlockSpec((None, 1, D_MODEL), lambda b, i: (2 * b + i // LAT_ROW_TILES, 0, 0))
    return pl.BlockSpec((None, 1, D_MODEL), lambda j, b, i: (2 * b + i // LAT_ROW_TILES, 0, 0))


WIDE_ROW_TILE = 3 * ROW_TILE
assert T_ALL % WIDE_ROW_TILE == 0 and Z_COLS % D_MODEL == 0


def _inproj_kernel(x_ref, g_ref, shl_ref, scl_ref, shc_ref, scc_ref, w_ref, o_ref):
    row = pl.program_id(1) * WIDE_ROW_TILE + lax.broadcasted_iota(jnp.int32, (WIDE_ROW_TILE, 1), 0)
    is_ctx = row >= SEQ
    h = _rms_mod(x_ref[...], g_ref[...], jnp.where(is_ctx, shc_ref[...], shl_ref[...]),
                 jnp.where(is_ctx, scc_ref[...], scl_ref[...])).astype(_BF)
    for j in range(Z_COLS // D_MODEL):
        cols = slice(j * D_MODEL, (j + 1) * D_MODEL)
        o_ref[:, cols] = _mm(h, w_ref[:, cols]).astype(o_ref.dtype)


def _in_projection(x_all, g, shift, scale, w):
    batch = x_all.shape[0]

    def mod_row(stream):
        return pl.BlockSpec((None, 1, D_MODEL), lambda b, i: (2 * b + stream, 0, 0))

    return pl.pallas_call(
        _inproj_kernel,
        grid=(batch, T_ALL // WIDE_ROW_TILE),
        in_specs=[
            pl.BlockSpec((None, WIDE_ROW_TILE, D_MODEL), lambda b, i: (b, i, 0)),
            pl.BlockSpec((1, D_MODEL), lambda b, i: (0, 0)),
            mod_row(0), mod_row(0), mod_row(1), mod_row(1),
            pl.BlockSpec((D_MODEL, Z_COLS), lambda b, i: (0, 0)),
        ],
        out_specs=pl.BlockSpec((None, WIDE_ROW_TILE, Z_COLS), lambda b, i: (b, i, 0)),
        out_shape=jax.ShapeDtypeStruct((batch, T_ALL, Z_COLS), _BF),
        compiler_params=_params("arbitrary", "arbitrary"),
        name="in_projection",
    )(x_all, g, shift, scale, shift, scale, w)


HALO = 8
HALO_BLOCKS = ROW_TILE // HALO
SHIFT_TILES = 4


def _inproj_shift_kernel(x_ref, xp_ref, xn_ref, g_ref, sh_ref, sc_ref, w_ref, ws_ref, o_ref):
    i = pl.program_id(1)
    starts = (i == 0) | (i == LAT_ROW_TILES)
    ends = (i == LAT_ROW_TILES - 1) | (i == N_ROW_TILES - 1)
    g, sh, sc = g_ref[...], sh_ref[...], sc_ref[...]
    h_prev = _rms_mod(xp_ref[...], g, sh, sc) * jnp.where(starts, 0.0, 1.0)
    h_next = _rms_mod(xn_ref[...], g, sh, sc) * jnp.where(ends, 0.0, 1.0)
    h = jnp.concatenate([h_prev, _rms_mod(x_ref[...], g, sh, sc), h_next], axis=0).astype(_BF)
    n_rows = ROW_TILE + 2 * HALO
    inner = slice(HALO, HALO + ROW_TILE)
    for j0 in range(0, RWKV_TILES, SHIFT_TILES):
        n_tiles = min(SHIFT_TILES, RWKV_TILES - j0)
        cols = slice(j0 * LANES, (j0 + n_tiles) * LANES)
        z = jnp.dot(h, w_ref[:, cols], preferred_element_type=_F32)
        w = ws_ref[:, cols]
        zr = (pltpu.roll(z, 1, 0)[inner] * w[0:1] + z[inner] * w[1:2]
              + pltpu.roll(z, n_rows - 1, 0)[inner] * w[2:3])
        for t in range(n_tiles):
            o_ref[j0 + t] = zr[:, t * LANES:(t + 1) * LANES].astype(o_ref.dtype)


def _in_projection_rwkv(x_all, g, shift, scale, w, w_shift):
    batch = x_all.shape[0]
    last_block = T_ALL // HALO - 1
    vec = pl.BlockSpec((1, D_MODEL), lambda b, i: (0, 0))
    return pl.pallas_call(
        _inproj_shift_kernel,
        grid=(batch, N_ROW_TILES),
        in_specs=[
            pl.BlockSpec((None, ROW_TILE, D_MODEL), lambda b, i: (b, i, 0)),
            pl.BlockSpec((None, HALO, D_MODEL),
                         lambda b, i: (b, jnp.maximum(i * HALO_BLOCKS - 1, 0), 0)),
            pl.BlockSpec((None, HALO, D_MODEL),
                         lambda b, i: (b, jnp.minimum((i + 1) * HALO_BLOCKS, last_block), 0)),
            vec, _mod_spec(2), _mod_spec(2),
            pl.BlockSpec((D_MODEL, RWKV_COLS), lambda b, i: (0, 0)),
            pl.BlockSpec((3, RWKV_COLS), lambda b, i: (0, 0)),
        ],
        out_specs=pl.BlockSpec((None, RWKV_TILES, ROW_TILE, LANES), lambda b, i: (b, 0, i, 0)),
        out_shape=jax.ShapeDtypeStruct((batch, RWKV_TILES, T_ALL, LANES), _BF),
        compiler_params=_params("arbitrary", "arbitrary"),
        name="in_projection_rwkv",
    )(x_all, x_all, x_all, g, shift, scale, w, w_shift)


def _head_sum(x):
    lane = lax.broadcasted_iota(jnp.int32, x.shape, x.ndim - 1)
    lo = jnp.sum(jnp.where(lane < HEAD_DIM, x, 0.0), -1, keepdims=True)
    tot = jnp.sum(x, -1, keepdims=True)
    return jnp.where(lane < HEAD_DIM, lo, tot - lo)


def _bmm(a, b):
    return lax.dot_general(a.astype(_BF), b.astype(_BF), (((2,), (1,)), ((0,), (0,))),
                           preferred_element_type=_F32)


def _bmm_tn(a, b):
    return lax.dot_general(a.astype(_BF), b.astype(_BF), (((1,), (1,)), ((0,), (0,))),
                           preferred_element_type=_F32)


N_SLOT_REFS = 5


def _scan_kernel(*refs):
    tokens = (refs[0:2], refs[2:4])
    wup_ref, w0_ref, aup_ref, a0_ref, kk_ref, ka_ref, mi_ref, mt_ref, mb_ref = refs[4:13]
    y_refs = refs[13:15]
    s_ref = refs[15]
    slots = (refs[16:16 + N_SLOT_REFS], refs[16 + N_SLOT_REFS:])
    step = pl.program_id(1)

    @pl.when(step == 0)
    def _():
        s_ref[...] = jnp.zeros_like(s_ref)
        for ref in slots[1]:
            ref[...] = jnp.zeros_like(ref)

    def run(prep, solve):
        pieces = itertools.chain.from_iterable(
            _scan_prepare(tokens[d], (wup_ref.at[d], w0_ref.at[d], aup_ref.at[d], a0_ref.at[d],
                                      kk_ref, ka_ref, mi_ref.at[d]), slots[prep], d)
            for d in range(2))
        next(pieces)
        for _ in _scan_solve(slots[solve], mt_ref, mb_ref, y_refs, s_ref):
            next(pieces, None)
        for _ in pieces:
            pass

    @pl.when(step % 2 == 0)
    def _():
        run(0, 1)

    @pl.when(step % 2 == 1)
    def _():
        run(1, 0)


PREP_PAIRS = 8
PREP_PIECES = HEAD_PAIRS // PREP_PAIRS


def _scan_prepare(token_refs, param_refs, slot, d):
    rkv_ref, lora_ref = token_refs
    r_ref, k_ref, v_ref = (rkv_ref.at[pl.ds(i * HEAD_PAIRS, HEAD_PAIRS)] for i in range(3))
    lw_ref, la_ref = lora_ref.at[0], lora_ref.at[1]
    wup_ref, w0_ref, aup_ref, a0_ref, kk_ref, ka_ref, mi_ref = param_refs
    lhs_ref, rk_ref, vv_ref, ke_ref, dt_ref = slot
    width = PREP_PAIRS * LANES

    def tiles(x):
        return jnp.stack([x[:, i * LANES:(i + 1) * LANES] for i in range(PREP_PAIRS)])

    lw_act = jnp.tanh(lw_ref[...].astype(_F32))
    la_in = la_ref[...]
    m_incl = mi_ref[...]
    for g in range(PREP_PIECES):
        cols = pl.ds(g * width, width)
        pairs = pl.ds(g * PREP_PAIRS, PREP_PAIRS)
        xw = w0_ref[:, cols] + _mm(lw_act, wup_ref[:, cols])
        ld = -(math.exp(-0.5) * math.log2(math.e)) * _sigmoid(xw)
        a = tiles(_sigmoid(a0_ref[:, cols] + _mm(la_in, aup_ref[:, cols])))
        hi = ld.astype(_BF)
        rem = ld - hi.astype(_F32)
        mid = rem.astype(_BF)
        lo = (rem - mid.astype(_F32)).astype(_BF)
        cl = (jnp.dot(m_incl, hi, preferred_element_type=_F32)
              + jnp.dot(m_incl, mid, preferred_element_type=_F32)
              + jnp.dot(m_incl, lo, preferred_element_type=_F32))
        ld_tot = jnp.sum(ld, 0, keepdims=True)
        dec_tot = jnp.exp2(ld_tot)
        e_in = tiles(jnp.exp2(cl))
        e_out = tiles(jnp.exp2(-cl))
        e_ex = tiles(jnp.exp2(cl - ld))
        e_end = tiles(jnp.exp2(ld_tot - cl))

        r = r_ref[pairs].astype(_F32)
        k = k_ref[pairs].astype(_F32)
        kk = k * tiles(kk_ref[:, cols])
        kk = kk * lax.rsqrt(jnp.maximum(_head_sum(kk * kk), 1e-24))
        k_d = k * (1.0 + (a - 1.0) * tiles(ka_ref[:, cols]))
        kka = kk * a

        rows = pl.ds(d * HEAD_PAIRS + g * PREP_PAIRS, PREP_PAIRS)
        lhs_ref[rows] = jnp.concatenate([kk * e_ex, r * e_in], axis=1).astype(_BF)
        rk_ref[rows] = jnp.swapaxes(jnp.concatenate([k_d * e_out, kka * e_out], axis=1),
                                    1, 2).astype(_BF)
        vv_ref[rows] = v_ref[pairs]
        ke_ref[rows] = jnp.concatenate([k_d * e_end, -(kka * e_end)], axis=1).astype(_BF)
        dt_ref[rows] = tiles(dec_tot)
        yield


def _by_direction(x, m_ref):
    return jnp.concatenate([x[:HEAD_PAIRS] * m_ref[0], x[HEAD_PAIRS:] * m_ref[1]], axis=0)


def _scan_solve(slot, mt_ref, mb_ref, y_refs, s_ref):
    c = SCAN_CHUNK
    lhs_ref, rk_ref, vv_ref, ke_ref, dt_ref = slot
    lhs = lhs_ref[...]
    rk = rk_ref[...]
    v = vv_ref[...]
    zero = jnp.zeros((), _BF)
    even = lax.broadcasted_iota(jnp.int32, (1, 1, LANES), 2) < HEAD_DIM
    kkt, rt = lhs[:, :c], lhs[:, c:]
    main = _bmm(jnp.concatenate([jnp.where(even, kkt, zero), jnp.where(even, rt, zero),
                                 jnp.where(even, zero, rt), jnp.where(even, zero, kkt)], axis=1), rk)
    yield
    sbd = s_ref[...]
    ls = _bmm(lhs, jnp.swapaxes(sbd, 1, 2))
    yield
    top_odd = pltpu.roll(main[:, 3 * c:], HEAD_DIM, 2)
    main = main[:, :3 * c].astype(_BF)
    top = _by_direction(jnp.concatenate([main[:, :c], top_odd.astype(_BF)], axis=1), mt_ref)
    bot = _by_direction(main[:, c:], mb_ref)
    row_even = lax.broadcasted_iota(jnp.int32, (1, 2 * c, LANES), 1) < c
    is_x = row_even == even
    vv = jnp.concatenate([v, v], axis=1)
    out = _bmm(jnp.where(is_x, top, zero), vv)
    yield
    def both_heads(o):
        return jnp.where(even, o[:, :c], o[:, c:])

    def operand(x, n_part):
        xb = x.astype(_BF)
        return jnp.where(is_x, jnp.concatenate([xb, xb], axis=1), n_part)

    x = ls[:, :c] + both_heads(out)
    zb = operand(x, top)
    n_steps = int(math.log2(c))
    for i in range(n_steps):
        out = _bmm(jnp.where(is_x, zero, zb), jnp.concatenate([zb[:, c:], zb[:, :c]], axis=1))
        yield
        x = x + both_heads(out)
        if i + 1 < n_steps:
            zb = operand(x, out.astype(_BF))
    vu = jnp.concatenate([v, x.astype(_BF)], axis=1)
    yy = _bmm(bot, vu)
    upd = _bmm_tn(vu, ke_ref[...])
    yield
    y = ls[:, c:] + jnp.where(even, yy[:, :c], yy[:, c:])
    y_refs[0][...] = y[:HEAD_PAIRS]
    y_refs[1][...] = y[HEAD_PAIRS:]
    blk_row = lax.broadcasted_iota(jnp.int32, (1, LANES, LANES), 1) < HEAD_DIM
    blk_col = lax.broadcasted_iota(jnp.int32, (1, LANES, LANES), 2) < HEAD_DIM
    s_ref[...] = jnp.where(blk_row == blk_col, sbd * dt_ref[...] + upd, 0.0)


def _scan_chunk_index(d, s):
    return (s + LAT_CHUNKS) % N_CHUNKS if d == 0 else N_CHUNKS - 1 - s


def _wkv_scan(zr, wup, w0, aup, a0, k_k, k_a, m_incl, m_top, m_bot):
    batch = zr.shape[0]
    c = SCAN_CHUNK

    def prep_chunk(d, s):
        return _scan_chunk_index(d, jnp.minimum(s, N_CHUNKS - 1))

    def solve_chunk(d, s):
        return _scan_chunk_index(d, jnp.maximum(s - 1, 0))

    def tokens(d):
        assert TILE_W % 2 == 0 and TILE_A == TILE_W + 1
        return [pl.BlockSpec((None, TILE_W, c, LANES), lambda b, s: (b, 0, prep_chunk(d, s), 0)),
                pl.BlockSpec((None, 2, c, LANES), lambda b, s: (b, TILE_W // 2, prep_chunk(d, s), 0))]

    def whole(*shape):
        return pl.BlockSpec(shape, lambda b, s: (0,) * len(shape))

    def y_spec(d):
        return pl.BlockSpec((None, HEAD_PAIRS, c, LANES), lambda b, s: (b, 0, solve_chunk(d, s), 0))

    both = 2 * HEAD_PAIRS
    slot = [pltpu.VMEM((both, 2 * c, LANES), _BF), pltpu.VMEM((both, 2 * c, LANES), _BF),
            pltpu.VMEM((both, c, LANES), _BF), pltpu.VMEM((both, 2 * c, LANES), _BF),
            pltpu.VMEM((both, 1, LANES), _F32)]
    assert len(slot) == N_SLOT_REFS
    y_shape = jax.ShapeDtypeStruct((batch, HEAD_PAIRS, T_ALL, LANES), _F32)
    return pl.pallas_call(
        _scan_kernel,
        grid=(batch, N_CHUNKS + 1),
        in_specs=tokens(0) + tokens(1) + [
            whole(2, LANES, D_MODEL), whole(2, 1, D_MODEL),
            whole(2, LANES, D_MODEL), whole(2, 1, D_MODEL),
            whole(1, D_MODEL), whole(1, D_MODEL),
            whole(2, c, c), whole(2, 2 * c, 2 * c), whole(2, 2 * c, 2 * c),
        ],
        out_specs=[y_spec(0), y_spec(1)],
        out_shape=[y_shape, y_shape],
        scratch_shapes=[pltpu.VMEM((both, LANES, LANES), _F32)] + slot * 2,
        compiler_params=_params("arbitrary", "arbitrary"),
        name="wkv_scan",
    )(*([zr] * 4), wup, w0, aup, a0, k_k, k_a, m_incl, m_top, m_bot)


READOUT_PAIRS = 2


def _rwkv_readout(yf_ref, yb_ref, r_ref, k_ref, v_ref, gz_ref, gup_ref, gng_ref, gnb_ref, rk_ref,
                  wro_ref):
    gate = _mm(_sigmoid(gz_ref[...].astype(_F32)), gup_ref[...])
    acc = jnp.zeros((ROW_TILE, D_MODEL), _F32)
    for piece in range(HEAD_PAIRS // READOUT_PAIRS):
        parts = []
        for hp in range(piece * READOUT_PAIRS, (piece + 1) * READOUT_PAIRS):
            cols = slice(hp * LANES, (hp + 1) * LANES)
            y = yf_ref[hp] + yb_ref[hp]
            mean = _head_sum(y) * (1.0 / HEAD_DIM)
            yc = y - mean
            var = _head_sum(yc * yc) * (1.0 / HEAD_DIM)
            yn = yc * lax.rsqrt(var + GN_EPS) * gng_ref[:, cols] + gnb_ref[:, cols]
            bonus = (_head_sum(r_ref[hp].astype(_F32) * k_ref[hp].astype(_F32) * rk_ref[:, cols])
                     * v_ref[hp].astype(_F32))
            parts.append(((yn + bonus) * gate[:, cols]).astype(_BF))
        rows = slice(piece * READOUT_PAIRS * LANES, (piece + 1) * READOUT_PAIRS * LANES)
        acc = acc + _mm(jnp.concatenate(parts, axis=1), wro_ref[rows, :])
    return acc


_H_SLOT = GRID_W + 16
_H_LEAD = 16
_V_PAD = CONV_HALF * GRID_W
_CTX_LEAD = 16


def _conv_taps(pad_ref, w, base, length, stride):
    acc = None
    for j in range(CONV_WIDTH):
        term = pad_ref[pl.ds(base + (j - CONV_HALF) * stride, length), :] * w[j:j + 1]
        acc = term if acc is None else acc + term
    return acc


def _glu(zv_ref, zg_ref, rows):
    return zv_ref[rows, :].astype(_F32) * _sigmoid(zg_ref[rows, :].astype(_F32))


def _conv_kernel(horizontal, zv_ref, zg_ref, w_ref, b_ref, o_ref, lat_ref, ctx_ref):
    w = w_ref[...]
    bias = b_ref[...]
    lat_ref[...] = jnp.zeros_like(lat_ref)
    ctx_ref[...] = jnp.zeros_like(ctx_ref)
    ctx_ref[pl.ds(_CTX_LEAD, CTX_LEN), :] = (
        _glu(zv_ref, zg_ref, pl.ds(SEQ, CTX_LEN)))
    o_ref[pl.ds(SEQ, CTX_LEN), :] = _conv_taps(ctx_ref, w, _CTX_LEAD, CTX_LEN, 1) + bias
    if horizontal:
        for row in range(GRID_H):
            src = pl.ds(row * GRID_W, GRID_W)
            lat_ref[pl.ds(_H_LEAD + row * _H_SLOT, GRID_W), :] = (
                _glu(zv_ref, zg_ref, src))
        for row in range(GRID_H):
            o_ref[pl.ds(row * GRID_W, GRID_W), :] = (
                _conv_taps(lat_ref, w, _H_LEAD + row * _H_SLOT, GRID_W, 1) + bias)
    else:
        lat_ref[pl.ds(_V_PAD, SEQ), :] = (
            _glu(zv_ref, zg_ref, pl.ds(0, SEQ)))
        blk = 4 * GRID_W
        for i in range(SEQ // blk):
            o_ref[pl.ds(i * blk, blk), :] = _conv_taps(lat_ref, w, _V_PAD + i * blk, blk, GRID_W) + bias


def _conformer_conv(z, dw, dw_b, horizontal):
    batch = z.shape[0]
    first = Z_CONV // LANES
    n_tiles = CONV_DIM // LANES
    lat_rows = (_H_LEAD + GRID_H * _H_SLOT) if horizontal else (SEQ + 2 * _V_PAD)
    return pl.pallas_call(
        functools.partial(_conv_kernel, horizontal),
        grid=(batch, n_tiles),
        in_specs=[
            pl.BlockSpec((None, T_ALL, LANES), lambda b, j: (b, 0, first + j)),
            pl.BlockSpec((None, T_ALL, LANES), lambda b, j: (b, 0, first + n_tiles + j)),
            pl.BlockSpec((CONV_WIDTH, LANES), lambda b, j: (0, j)),
            pl.BlockSpec((1, LANES), lambda b, j: (0, j)),
        ],
        out_specs=pl.BlockSpec((None, T_ALL, LANES), lambda b, j: (b, 0, j)),
        out_shape=jax.ShapeDtypeStruct((batch, T_ALL, CONV_DIM), _F32),
        scratch_shapes=[pltpu.VMEM((lat_rows, LANES), _F32),
                        pltpu.VMEM((CTX_LEN + 2 * _CTX_LEAD, LANES), _F32)],
        compiler_params=_params("arbitrary", "arbitrary"),
        name="conformer_conv_h" if horizontal else "conformer_conv_v",
    )(z, z, dw, dw_b)


SGU_ROW_TILE = 6 * SGU_CHUNK
assert T_ALL % SGU_ROW_TILE == 0 and SEQ % SGU_CHUNK == 0


def _sgu_kernel(zu_ref, zv_ref, lng_ref, lnb_ref, ws_ref, bs_ref, o_ref):
    v = _standardize(_gelu_tanh(zv_ref[...].astype(_F32)), LN_EPS) * lng_ref[...] + lnb_ref[...]
    v = v.astype(_BF)
    first_half = lax.broadcasted_iota(jnp.int32, (SGU_CHUNK, LANES), 1) < SGU_GROUP_DIM
    for n in range(SGU_ROW_TILE // SGU_CHUNK):
        rows = slice(n * SGU_CHUNK, (n + 1) * SGU_CHUNK)
        tiles = []
        for t in range(SGU_DIM // LANES):
            vt = v[rows, t * LANES:(t + 1) * LANES]
            tiles.append(jnp.where(first_half, _mm(ws_ref[2 * t], vt), _mm(ws_ref[2 * t + 1], vt)))
        mixed = jnp.concatenate(tiles, axis=1) + bs_ref[...]
        o_ref[rows, :] = (_gelu_tanh(zu_ref[rows, :].astype(_F32)) * mixed).astype(o_ref.dtype)


def _sgu(z, ln_g, ln_b, w_s, b_tile):
    batch = z.shape[0]
    first = Z_SGU // SGU_DIM
    vec = pl.BlockSpec((1, SGU_DIM), lambda b, i: (0, 0))
    return pl.pallas_call(
        _sgu_kernel,
        grid=(batch, T_ALL // SGU_ROW_TILE),
        in_specs=[
            pl.BlockSpec((None, SGU_ROW_TILE, SGU_DIM), lambda b, i: (b, i, first)),
            pl.BlockSpec((None, SGU_ROW_TILE, SGU_DIM), lambda b, i: (b, i, first + 1)),
            vec, vec,
            pl.BlockSpec((SGU_GROUPS, SGU_CHUNK, SGU_CHUNK), lambda b, i: (0, 0, 0)),
            pl.BlockSpec((SGU_CHUNK, SGU_DIM), lambda b, i: (0, 0)),
        ],
        out_specs=pl.BlockSpec((None, SGU_ROW_TILE, SGU_DIM), lambda b, i: (b, i, 0)),
        out_shape=jax.ShapeDtypeStruct((batch, T_ALL, SGU_DIM), _BF),
        compiler_params=_params("arbitrary", "arbitrary"),
        name="sgu",
    )(z, z, ln_g, ln_b, w_s, b_tile)


def _merge_kernel(x_ref, yf_ref, yb_ref, r_ref, k_ref, v_ref, gz_ref, cv_ref, c_ref,
                  g0_ref, g1_ref, g2_ref, gate_ref, gup_ref, gng_ref, gnb_ref, rk_ref, lng_ref, lnb_ref,
                  wro_ref, wco_ref, wso_ref, wm_ref, o_ref):
    c = _mm(c_ref[...], wso_ref[...])
    cb = _silu(_standardize(cv_ref[...], LN_EPS) * lng_ref[...] + lnb_ref[...])
    b = _mm(cb, wco_ref[...])
    a = _rwkv_readout(yf_ref, yb_ref, r_ref, k_ref, v_ref, gz_ref, gup_ref, gng_ref, gnb_ref, rk_ref,
                      wro_ref)
    m = (a * _sigmoid(g0_ref[...].astype(_F32)) + b * _sigmoid(g1_ref[...].astype(_F32))
         + c * _sigmoid(g2_ref[...].astype(_F32)))
    o_ref[...] = x_ref[...] + gate_ref[...] * _mm(m, wm_ref[...])


def _merge(x_all, y_fwd, y_bwd, zr, conv, c_pre, z, gate1, g_up, gn_g, gn_b, r_k, ln_g, ln_b,
           w_ro, w_co, w_so, w_m):
    batch = x_all.shape[0]

    def rows(width, col=0):
        return pl.BlockSpec((None, ROW_TILE, width), lambda b, i: (b, i, col))

    def group(idx):
        return pl.BlockSpec((None, HEAD_PAIRS, ROW_TILE, LANES), lambda b, i: (b, idx, i, 0))

    def whole(shape):
        return pl.BlockSpec(shape, lambda b, i: (0, 0))

    vec = whole((1, D_MODEL))
    return pl.pallas_call(
        _merge_kernel,
        grid=(batch, N_ROW_TILES),
        in_specs=[
            rows(D_MODEL), group(0), group(0), group(0), group(1), group(2),
            pl.BlockSpec((None, None, ROW_TILE, LANES), lambda b, i: (b, TILE_G, i, 0)),
            rows(CONV_DIM), rows(SGU_DIM),
            rows(D_MODEL, 0), rows(D_MODEL, 1), rows(D_MODEL, 2),
            _mod_spec(2), whole((GATE_LORA, D_MODEL)), vec, vec, vec,
            whole((1, CONV_DIM)), whole((1, CONV_DIM)),
            whole((D_MODEL, D_MODEL)), whole((CONV_DIM, D_MODEL)), whole((SGU_DIM, D_MODEL)),
            whole((D_MODEL, D_MODEL)),
        ],
        out_specs=rows(D_MODEL),
        out_shape=jax.ShapeDtypeStruct((batch, T_ALL, D_MODEL), _F32),
        compiler_params=_params("arbitrary", "arbitrary"),
        name="merge",
    )(x_all, y_fwd, y_bwd, zr, zr, zr, zr, conv, c_pre, z, z, z, gate1, g_up, gn_g, gn_b, r_k,
      ln_g, ln_b, w_ro, w_co, w_so, w_m)


def _ffn_kernel(last, x_ref, g_ref, sh_ref, sc_ref, gate_ref, win_ref, wout_ref, fg_ref, o_ref):
    x = x_ref[...]
    h = _rms_mod(x, g_ref[...], sh_ref[...], sc_ref[...]).astype(_BF)
    def gate_up(f):
        cols = slice(f * FF_CHUNK, (f + 1) * FF_CHUNK)
        up_cols = slice(D_FF + f * FF_CHUNK, D_FF + (f + 1) * FF_CHUNK)
        return _mm(h, win_ref[:, cols]), _mm(h, win_ref[:, up_cols])

    n_chunks = D_FF // FF_CHUNK
    acc = jnp.zeros((ROW_TILE, D_MODEL), _F32)
    pending = gate_up(0)
    for f in range(n_chunks):
        g, u = pending
        if f + 1 < n_chunks:
            pending = gate_up(f + 1)
        acc = acc + _mm(_silu(g) * u, wout_ref[f * FF_CHUNK:(f + 1) * FF_CHUNK, :])
    out = x + gate_ref[...] * acc
    if last:
        out = out * lax.rsqrt(jnp.mean(out * out, -1, keepdims=True) + NORM_EPS) * fg_ref[...]
    o_ref[...] = out


def _ffn(x_all, g, shift, scale, gate2, w_in, w_out, final_g, last):
    batch = x_all.shape[0]
    rows = pl.BlockSpec((None, ROW_TILE, D_MODEL), lambda b, i: (b, i, 0))
    vec = pl.BlockSpec((1, D_MODEL), lambda b, i: (0, 0))
    return pl.pallas_call(
        functools.partial(_ffn_kernel, last),
        grid=(batch, LAT_ROW_TILES if last else N_ROW_TILES),
        in_specs=[
            rows, vec,
            _mod_spec(2), _mod_spec(2), _mod_spec(2),
            pl.BlockSpec((D_MODEL, 2 * D_FF), lambda b, i: (0, 0)),
            pl.BlockSpec((D_FF, D_MODEL), lambda b, i: (0, 0)),
            vec,
        ],
        out_specs=rows,
        out_shape=jax.ShapeDtypeStruct((batch, SEQ if last else T_ALL, D_MODEL), _F32),
        compiler_params=_params("arbitrary", "arbitrary"),
        name="swiglu_final" if last else "swiglu",
    )(x_all, g, shift, scale, gate2, w_in, w_out, final_g)


def _split_w_in(w):
    off_conv = RWKV_COLS
    off_gate = off_conv + 2 * CONV_DIM + 2 * SGU_DIM
    return jnp.concatenate([w[:, off_gate:], w[:, off_conv:off_gate]], axis=1), w[:, :off_conv]


def _direction_padded(w_up):
    zero = jnp.zeros_like(w_up[0])
    return jnp.stack([jnp.concatenate([w_up[0], zero], 0), jnp.concatenate([zero, w_up[1]], 0)])


def kernel(x, c, ctx, c_ctx, w_mod, b_mod, norm1_g, norm2_g, w_in, rwkv_shift, rwkv_w0, rwkv_w_up, rwkv_a0, rwkv_a_up, rwkv_g_up, rwkv_k_k, rwkv_k_a, rwkv_r_k, rwkv_gn_g, rwkv_gn_b, rwkv_out, conv_dw, conv_dw_b, conv_ln_g, conv_ln_b, conv_out, sgu_ln_g, sgu_ln_b, sgu_w, sgu_b, sgu_out, w_merge, ffn_w_in, ffn_w_out, final_norm_g):
    batch = x.shape[0]
    depth = w_mod.shape[0]
    assert x.shape[1:] == (SEQ, D_MODEL) and ctx.shape[1:] == (CTX_LEN, D_MODEL)

    rows = -(-(batch + 1) // 8) * 8
    cvec = jnp.zeros((rows, D_MODEL), _F32).at[:batch].set(c).at[batch].set(c_ctx)
    mod = _modulation(cvec, w_mod.astype(_BF), b_mod[:, None, :])
    mod_lat = mod[:, :batch].reshape(depth, batch, 6, D_MODEL)
    mod_ctx = jnp.broadcast_to(mod[:, batch].reshape(depth, 1, 6, D_MODEL), mod_lat.shape)
    mod_tab = jnp.stack([mod_lat, mod_ctx], axis=2).transpose(0, 3, 1, 2, 4)
    mod_tab = mod_tab.reshape(depth, 6, 2 * batch, 1, D_MODEL)

    idx = jnp.arange(SCAN_CHUNK)
    incl = jnp.stack([idx[:, None] >= idx[None, :], idx[:, None] <= idx[None, :]]).astype(_F32)
    strict = jnp.stack([idx[:, None] > idx[None, :], idx[:, None] < idx[None, :]]).astype(_F32)
    m_top = jnp.concatenate([jnp.concatenate([strict, -strict], 2),
                             jnp.concatenate([-strict, strict], 2)], 1)
    m_bot = jnp.concatenate([jnp.concatenate([incl, -incl], 2)] * 2, 1).astype(_BF)
    m_top = m_top.astype(_BF)
    m_incl = incl.astype(_BF)

    x_all = jnp.concatenate([x, ctx], axis=1)
    for l in range(depth):
        sh1, sc1, g1, sh2, sc2, g2 = (mod_tab[l, i] for i in range(6))
        w_z, w_r = _split_w_in(w_in[l])
        z = _in_projection(x_all, norm1_g[l][None], sh1, sc1, w_z.astype(_BF))
        zr = _in_projection_rwkv(x_all, norm1_g[l][None], sh1, sc1, w_r.astype(_BF), rwkv_shift[l])
        y_fwd, y_bwd = _wkv_scan(zr, _direction_padded(rwkv_w_up[l]).astype(_BF), rwkv_w0[l][:, None, :],
                      _direction_padded(rwkv_a_up[l]).astype(_BF), rwkv_a0[l][:, None, :],
                      rwkv_k_k[l][None], rwkv_k_a[l][None], m_incl, m_top, m_bot)
        conv = _conformer_conv(z, conv_dw[l], conv_dw_b[l][None], l % 2 == 0)
        b_tile = jnp.repeat(sgu_b[l].T, SGU_GROUP_DIM, axis=1)
        c_pre = _sgu(z, sgu_ln_g[l][None], sgu_ln_b[l][None], sgu_w[l].astype(_BF), b_tile)
        x_all = _merge(x_all, y_fwd, y_bwd, zr, conv, c_pre, z, g1, rwkv_g_up[l].astype(_BF),
                       rwkv_gn_g[l][None], rwkv_gn_b[l][None], rwkv_r_k[l][None],
                       conv_ln_g[l][None], conv_ln_b[l][None],
                       rwkv_out[l].astype(_BF), conv_out[l].astype(_BF), sgu_out[l].astype(_BF),
                       w_merge[l].astype(_BF))
        x_all = _ffn(x_all, norm2_g[l][None], sh2, sc2, g2, ffn_w_in[l].astype(_BF),
                     ffn_w_out[l].astype(_BF), final_norm_g[None], l == depth - 1)
    return x_all
```

```python
import functools
import itertools
import math

import jax
import jax.numpy as jnp
from jax import lax
from jax.experimental import pallas as pl
from jax.experimental.pallas import tpu as pltpu

D_MODEL = 1024
SEQ = 2048
CTX_LEN = 256
T_ALL = SEQ + CTX_LEN
GRID_W = 64
GRID_H = SEQ // GRID_W

HEAD_DIM = 64
HEADS = D_MODEL // HEAD_DIM
HEAD_PAIRS = HEADS // 2
DECAY_LORA = 64
ICLR_LORA = 64
GATE_LORA = 128
CONV_DIM = D_MODEL // 2
CONV_WIDTH = 31
CONV_HALF = CONV_WIDTH // 2
SGU_DIM = D_MODEL // 2
SGU_GROUPS = 8
SGU_GROUP_DIM = SGU_DIM // SGU_GROUPS
SGU_CHUNK = 128
D_FF = ((8 * D_MODEL // 3 + 255) // 256) * 256
NORM_EPS = 1e-6
LN_EPS = 1e-5
GN_EPS = 64e-5

RWKV_COLS = 3 * D_MODEL + 2 * DECAY_LORA + 2 * ICLR_LORA + GATE_LORA

LANES = 128
Z_GATE = 0
Z_CONV = 3 * D_MODEL
Z_SGU = Z_CONV + 2 * CONV_DIM
Z_COLS = Z_SGU + 2 * SGU_DIM
RWKV_TILES = RWKV_COLS // LANES
TILE_W = 3 * HEAD_PAIRS
TILE_A = TILE_W + 1
TILE_G = TILE_A + 1

ROW_TILE = 256
N_ROW_TILES = T_ALL // ROW_TILE
LAT_ROW_TILES = SEQ // ROW_TILE
SCAN_CHUNK = 64
N_CHUNKS = T_ALL // SCAN_CHUNK
LAT_CHUNKS = SEQ // SCAN_CHUNK
FF_CHUNK = 256

VMEM_LIMIT = 56 * 1024 * 1024

_BF = jnp.bfloat16
_F32 = jnp.float32


def _mm(a, b):
    return jnp.dot(a.astype(_BF), b.astype(_BF), preferred_element_type=_F32)


def _sigmoid(x):
    return jax.nn.sigmoid(x)


def _silu(x):
    return x * jax.nn.sigmoid(x)


def _gelu_tanh(x):
    return 0.5 * x * (1.0 + jnp.tanh(math.sqrt(2.0 / math.pi) * (x + 0.044715 * (x * x * x))))


def _standardize(x, eps):
    xc = x - jnp.mean(x, -1, keepdims=True)
    return xc * lax.rsqrt(jnp.mean(xc * xc, -1, keepdims=True) + eps)


def _rms_mod(x, g, shift, scale):
    y = x * lax.rsqrt(jnp.mean(x * x, -1, keepdims=True) + NORM_EPS) * g
    return y * (1.0 + scale) + shift


def _params(*sem):
    return pltpu.CompilerParams(dimension_semantics=sem, vmem_limit_bytes=VMEM_LIMIT)


def _mod_kernel(c_ref, w_ref, b_ref, o_ref):
    o_ref[...] = _mm(_silu(c_ref[...]), w_ref[...]) + b_ref[...]


def _modulation(cvec, w_mod, b_mod):
    depth = w_mod.shape[0]
    rows = cvec.shape[0]
    return pl.pallas_call(
        _mod_kernel,
        grid=(depth, 6),
        in_specs=[
            pl.BlockSpec((rows, D_MODEL), lambda l, j: (0, 0)),
            pl.BlockSpec((None, D_MODEL, D_MODEL), lambda l, j: (l, 0, j)),
            pl.BlockSpec((None, 1, D_MODEL), lambda l, j: (l, 0, j)),
        ],
        out_specs=pl.BlockSpec((None, rows, D_MODEL), lambda l, j: (l, 0, j)),
        out_shape=jax.ShapeDtypeStruct((depth, rows, 6 * D_MODEL), _F32),
        compiler_params=_params("arbitrary", "arbitrary"),
        name="modulation",
    )(cvec, w_mod, b_mod)


def _mod_spec(grid_rank):
    if grid_rank == 2:
        return pl.BlockSpec((None, 1, D_MODEL), lambda b, i: (2 * b + i // LAT_ROW_TILES, 0, 0))
    return pl.BlockSpec((None, 1, D_MODEL), lambda j, b, i: (2 * b + i // LAT_ROW_TILES, 0, 0))


def _inproj_kernel(x_ref, g_ref, sh_ref, sc_ref, w_ref, o_ref):
    h = _rms_mod(x_ref[...], g_ref[...], sh_ref[...], sc_ref[...])
    o_ref[...] = _mm(h, w_ref[...]).astype(o_ref.dtype)


def _in_projection(x_all, g, shift, scale, w):
    batch = x_all.shape[0]
    n_col = 1
    tn = Z_COLS // n_col
    return pl.pallas_call(
        _inproj_kernel,
        grid=(n_col, batch, N_ROW_TILES),
        in_specs=[
            pl.BlockSpec((None, ROW_TILE, D_MODEL), lambda j, b, i: (b, i, 0)),
            pl.BlockSpec((1, D_MODEL), lambda j, b, i: (0, 0)),
            _mod_spec(3),
            _mod_spec(3),
            pl.BlockSpec((D_MODEL, tn), lambda j, b, i: (0, j)),
        ],
        out_specs=pl.BlockSpec((None, ROW_TILE, tn), lambda j, b, i: (b, i, j)),
        out_shape=jax.ShapeDtypeStruct((batch, T_ALL, Z_COLS), _BF),
        compiler_params=_params("arbitrary", "arbitrary", "arbitrary"),
        name="in_projection",
    )(x_all, g, shift, scale, w)


HALO = 8
HALO_BLOCKS = ROW_TILE // HALO
SHIFT_TILES = 4


def _inproj_shift_kernel(x_ref, xp_ref, xn_ref, g_ref, sh_ref, sc_ref, w_ref, ws_ref, o_ref):
    i = pl.program_id(1)
    starts = (i == 0) | (i == LAT_ROW_TILES)
    ends = (i == LAT_ROW_TILES - 1) | (i == N_ROW_TILES - 1)
    g, sh, sc = g_ref[...], sh_ref[...], sc_ref[...]
    h_prev = _rms_mod(xp_ref[...], g, sh, sc) * jnp.where(starts, 0.0, 1.0)
    h_next = _rms_mod(xn_ref[...], g, sh, sc) * jnp.where(ends, 0.0, 1.0)
    h = jnp.concatenate([h_prev, _rms_mod(x_ref[...], g, sh, sc), h_next], axis=0).astype(_BF)
    n_rows = ROW_TILE + 2 * HALO
    inner = slice(HALO, HALO + ROW_TILE)
    for j0 in range(0, RWKV_TILES, SHIFT_TILES):
        n_tiles = min(SHIFT_TILES, RWKV_TILES - j0)
        cols = slice(j0 * LANES, (j0 + n_tiles) * LANES)
        z = jnp.dot(h, w_ref[:, cols], preferred_element_type=_F32)
        w = ws_ref[:, cols]
        zr = (pltpu.roll(z, 1, 0)[inner] * w[0:1] + z[inner] * w[1:2]
              + pltpu.roll(z, n_rows - 1, 0)[inner] * w[2:3])
        for t in range(n_tiles):
            o_ref[j0 + t] = zr[:, t * LANES:(t + 1) * LANES].astype(o_ref.dtype)


def _in_projection_rwkv(x_all, g, shift, scale, w, w_shift):
    batch = x_all.shape[0]
    last_block = T_ALL // HALO - 1
    vec = pl.BlockSpec((1, D_MODEL), lambda b, i: (0, 0))
    return pl.pallas_call(
        _inproj_shift_kernel,
        grid=(batch, N_ROW_TILES),
        in_specs=[
            pl.BlockSpec((None, ROW_TILE, D_MODEL), lambda b, i: (b, i, 0)),
            pl.BlockSpec((None, HALO, D_MODEL),
                         lambda b, i: (b, jnp.maximum(i * HALO_BLOCKS - 1, 0), 0)),
            pl.BlockSpec((None, HALO, D_MODEL),
                         lambda b, i: (b, jnp.minimum((i + 1) * HALO_BLOCKS, last_block), 0)),
            vec, _mod_spec(2), _mod_spec(2),
            pl.BlockSpec((D_MODEL, RWKV_COLS), lambda b, i: (0, 0)),
            pl.BlockSpec((3, RWKV_COLS), lambda b, i: (0, 0)),
        ],
        out_specs=pl.BlockSpec((None, RWKV_TILES, ROW_TILE, LANES), lambda b, i: (b, 0, i, 0)),
        out_shape=jax.ShapeDtypeStruct((batch, RWKV_TILES, T_ALL, LANES), _BF),
        compiler_params=_params("arbitrary", "arbitrary"),
        name="in_projection_rwkv",
    )(x_all, x_all, x_all, g, shift, scale, w, w_shift)


def _head_sum(x):
    lane = lax.broadcasted_iota(jnp.int32, x.shape, x.ndim - 1)
    lo = jnp.sum(jnp.where(lane < HEAD_DIM, x, 0.0), -1, keepdims=True)
    tot = jnp.sum(x, -1, keepdims=True)
    return jnp.where(lane < HEAD_DIM, lo, tot - lo)


def _bmm(a, b):
    return lax.dot_general(a.astype(_BF), b.astype(_BF), (((2,), (1,)), ((0,), (0,))),
                           preferred_element_type=_F32)


def _bmm_tn(a, b):
    return lax.dot_general(a.astype(_BF), b.astype(_BF), (((1,), (1,)), ((0,), (0,))),
                           preferred_element_type=_F32)


N_SLOT_REFS = 5


def _scan_kernel(*refs):
    tokens = (refs[0:2], refs[2:4])
    wup_ref, w0_ref, aup_ref, a0_ref, kk_ref, ka_ref, mi_ref, mt_ref, mb_ref = refs[4:13]
    y_refs = refs[13:15]
    s_ref = refs[15]
    slots = (refs[16:16 + N_SLOT_REFS], refs[16 + N_SLOT_REFS:])
    step = pl.program_id(1)

    def run(prep, solve):
        pieces = iter(())
        if prep is not None:
            pieces = itertools.chain.from_iterable(
                _scan_prepare(tokens[d], (wup_ref.at[d], w0_ref.at[d], aup_ref.at[d], a0_ref.at[d],
                                          kk_ref, ka_ref, mi_ref.at[d]), slots[prep], d)
                for d in range(2))
            next(pieces)
        if solve is not None:
            for _ in _scan_solve(slots[solve], mt_ref, mb_ref, y_refs, s_ref):
                next(pieces, None)
        for _ in pieces:
            pass

    assert N_CHUNKS % 2 == 0

    @pl.when(step == 0)
    def _():
        s_ref[...] = jnp.zeros_like(s_ref)
        run(0, None)

    @pl.when((step % 2 == 0) & (step > 0) & (step < N_CHUNKS))
    def _():
        run(0, 1)

    @pl.when(step % 2 == 1)
    def _():
        run(1, 0)

    @pl.when(step == N_CHUNKS)
    def _():
        run(None, 1)


PREP_PAIRS = 8
PREP_PIECES = HEAD_PAIRS // PREP_PAIRS


def _scan_prepare(token_refs, param_refs, slot, d):
    rkv_ref, lora_ref = token_refs
    r_ref, k_ref, v_ref = (rkv_ref.at[pl.ds(i * HEAD_PAIRS, HEAD_PAIRS)] for i in range(3))
    lw_ref, la_ref = lora_ref.at[0], lora_ref.at[1]
    wup_ref, w0_ref, aup_ref, a0_ref, kk_ref, ka_ref, mi_ref = param_refs
    lhs_ref, rk_ref, vv_ref, ke_ref, dt_ref = slot
    width = PREP_PAIRS * LANES

    def tiles(x):
        return jnp.stack([x[:, i * LANES:(i + 1) * LANES] for i in range(PREP_PAIRS)])

    lw_act = jnp.tanh(lw_ref[...].astype(_F32))
    la_in = la_ref[...]
    m_incl = mi_ref[...]
    for g in range(PREP_PIECES):
        cols = pl.ds(g * width, width)
        pairs = pl.ds(g * PREP_PAIRS, PREP_PAIRS)
        xw = w0_ref[:, cols] + _mm(lw_act, wup_ref[:, cols])
        ld = -(math.exp(-0.5) * math.log2(math.e)) * _sigmoid(xw)
        a = tiles(_sigmoid(a0_ref[:, cols] + _mm(la_in, aup_ref[:, cols])))
        hi = ld.astype(_BF)
        rem = ld - hi.astype(_F32)
        mid = rem.astype(_BF)
        lo = (rem - mid.astype(_F32)).astype(_BF)
        cl = (jnp.dot(m_incl, hi, preferred_element_type=_F32)
              + jnp.dot(m_incl, mid, preferred_element_type=_F32)
              + jnp.dot(m_incl, lo, preferred_element_type=_F32))
        ld_tot = jnp.sum(ld, 0, keepdims=True)
        dec_tot = jnp.exp2(ld_tot)
        e_in = tiles(jnp.exp2(cl))
        e_out = tiles(jnp.exp2(-cl))
        e_ex = tiles(jnp.exp2(cl - ld))
        e_end = tiles(jnp.exp2(ld_tot - cl))

        r = r_ref[pairs].astype(_F32)
        k = k_ref[pairs].astype(_F32)
        kk = k * tiles(kk_ref[:, cols])
        kk = kk * lax.rsqrt(jnp.maximum(_head_sum(kk * kk), 1e-24))
        k_d = k * (1.0 + (a - 1.0) * tiles(ka_ref[:, cols]))
        kka = kk * a

        rows = pl.ds(d * HEAD_PAIRS + g * PREP_PAIRS, PREP_PAIRS)
        lhs_ref[rows] = jnp.concatenate([kk * e_ex, r * e_in], axis=1).astype(_BF)
        rk_ref[rows] = jnp.swapaxes(jnp.concatenate([k_d * e_out, kka * e_out], axis=1),
                                    1, 2).astype(_BF)
        vv_ref[rows] = v_ref[pairs]
        ke_ref[rows] = jnp.concatenate([k_d * e_end, -(kka * e_end)], axis=1).astype(_BF)
        dt_ref[rows] = tiles(dec_tot)
        yield


def _by_direction(x, m_ref):
    return jnp.concatenate([x[:HEAD_PAIRS] * m_ref[0], x[HEAD_PAIRS:] * m_ref[1]], axis=0)


def _scan_solve(slot, mt_ref, mb_ref, y_refs, s_ref):
    c = SCAN_CHUNK
    lhs_ref, rk_ref, vv_ref, ke_ref, dt_ref = slot
    lhs = lhs_ref[...]
    rk = rk_ref[...]
    v = vv_ref[...]
    zero = jnp.zeros((), _BF)
    even = lax.broadcasted_iota(jnp.int32, (1, 1, LANES), 2) < HEAD_DIM
    kkt, rt = lhs[:, :c], lhs[:, c:]
    main = _bmm(jnp.concatenate([jnp.where(even, kkt, zero), jnp.where(even, rt, zero),
                                 jnp.where(even, zero, rt), jnp.where(even, zero, kkt)], axis=1), rk)
    yield
    sbd = s_ref[...]
    ls = _bmm(lhs, jnp.swapaxes(sbd, 1, 2))
    yield
    top_odd = pltpu.roll(main[:, 3 * c:], HEAD_DIM, 2)
    main = main[:, :3 * c].astype(_BF)
    top = _by_direction(jnp.concatenate([main[:, :c], top_odd.astype(_BF)], axis=1), mt_ref)
    bot = _by_direction(main[:, c:], mb_ref)
    row_even = lax.broadcasted_iota(jnp.int32, (1, 2 * c, LANES), 1) < c
    is_x = row_even == even
    vv = jnp.concatenate([v, v], axis=1)
    out = _bmm(jnp.where(is_x, top, zero), vv)
    yield
    def both_heads(o):
        return jnp.where(even, o[:, :c], o[:, c:])

    def operand(x, n_part):
        xb = x.astype(_BF)
        return jnp.where(is_x, jnp.concatenate([xb, xb], axis=1), n_part)

    x = ls[:, :c] + both_heads(out)
    zb = operand(x, top)
    n_steps = int(math.log2(c))
    for i in range(n_steps):
        out = _bmm(jnp.where(is_x, zero, zb), jnp.concatenate([zb[:, c:], zb[:, :c]], axis=1))
        yield
        x = x + both_heads(out)
        if i + 1 < n_steps:
            zb = operand(x, out.astype(_BF))
    vu = jnp.concatenate([v, x.astype(_BF)], axis=1)
    yy = _bmm(bot, vu)
    upd = _bmm_tn(vu, ke_ref[...])
    yield
    y = ls[:, c:] + jnp.where(even, yy[:, :c], yy[:, c:])
    y_refs[0][...] = y[:HEAD_PAIRS]
    y_refs[1][...] = y[HEAD_PAIRS:]
    blk_row = lax.broadcasted_iota(jnp.int32, (1, LANES, LANES), 1) < HEAD_DIM
    blk_col = lax.broadcasted_iota(jnp.int32, (1, LANES, LANES), 2) < HEAD_DIM
    s_ref[...] = jnp.where(blk_row == blk_col, sbd * dt_ref[...] + upd, 0.0)


def _scan_chunk_index(d, s):
    return (s + LAT_CHUNKS) % N_CHUNKS if d == 0 else N_CHUNKS - 1 - s


def _wkv_scan(zr, wup, w0, aup, a0, k_k, k_a, m_incl, m_top, m_bot):
    batch = zr.shape[0]
    c = SCAN_CHUNK

    def prep_chunk(d, s):
        return _scan_chunk_index(d, jnp.minimum(s, N_CHUNKS - 1))

    def solve_chunk(d, s):
        return _scan_chunk_index(d, jnp.maximum(s - 1, 0))

    def tokens(d):
        assert TILE_W % 2 == 0 and TILE_A == TILE_W + 1
        return [pl.BlockSpec((None, TILE_W, c, LANES), lambda b, s: (b, 0, prep_chunk(d, s), 0)),
                pl.BlockSpec((None, 2, c, LANES), lambda b, s: (b, TILE_W // 2, prep_chunk(d, s), 0))]

    def whole(*shape):
        return pl.BlockSpec(shape, lambda b, s: (0,) * len(shape))

    def y_spec(d):
        return pl.BlockSpec((None, HEAD_PAIRS, c, LANES), lambda b, s: (b, 0, solve_chunk(d, s), 0))

    both = 2 * HEAD_PAIRS
    slot = [pltpu.VMEM((both, 2 * c, LANES), _BF), pltpu.VMEM((both, 2 * c, LANES), _BF),
            pltpu.VMEM((both, c, LANES), _BF), pltpu.VMEM((both, 2 * c, LANES), _BF),
            pltpu.VMEM((both, 1, LANES), _F32)]
    assert len(slot) == N_SLOT_REFS
    y_shape = jax.ShapeDtypeStruct((batch, HEAD_PAIRS, T_ALL, LANES), _F32)
    return pl.pallas_call(
        _scan_kernel,
        grid=(batch, N_CHUNKS + 1),
        in_specs=tokens(0) + tokens(1) + [
            whole(2, LANES, D_MODEL), whole(2, 1, D_MODEL),
            whole(2, LANES, D_MODEL), whole(2, 1, D_MODEL),
            whole(1, D_MODEL), whole(1, D_MODEL),
            whole(2, c, c), whole(2, 2 * c, 2 * c), whole(2, 2 * c, 2 * c),
        ],
        out_specs=[y_spec(0), y_spec(1)],
        out_shape=[y_shape, y_shape],
        scratch_shapes=[pltpu.VMEM((both, LANES, LANES), _F32)] + slot * 2,
        compiler_params=_params("arbitrary", "arbitrary"),
        name="wkv_scan",
    )(*([zr] * 4), wup, w0, aup, a0, k_k, k_a, m_incl, m_top, m_bot)


READOUT_PAIRS = 2


def _rwkv_readout(yf_ref, yb_ref, r_ref, k_ref, v_ref, gz_ref, gup_ref, gng_ref, gnb_ref, rk_ref,
                  wro_ref):
    gate = _mm(_sigmoid(gz_ref[...].astype(_F32)), gup_ref[...])
    acc = jnp.zeros((ROW_TILE, D_MODEL), _F32)
    for piece in range(HEAD_PAIRS // READOUT_PAIRS):
        parts = []
        for hp in range(piece * READOUT_PAIRS, (piece + 1) * READOUT_PAIRS):
            cols = slice(hp * LANES, (hp + 1) * LANES)
            y = yf_ref[hp] + yb_ref[hp]
            mean = _head_sum(y) * (1.0 / HEAD_DIM)
            yc = y - mean
            var = _head_sum(yc * yc) * (1.0 / HEAD_DIM)
            yn = yc * lax.rsqrt(var + GN_EPS) * gng_ref[:, cols] + gnb_ref[:, cols]
            bonus = (_head_sum(r_ref[hp].astype(_F32) * k_ref[hp].astype(_F32) * rk_ref[:, cols])
                     * v_ref[hp].astype(_F32))
            parts.append(((yn + bonus) * gate[:, cols]).astype(_BF))
        rows = slice(piece * READOUT_PAIRS * LANES, (piece + 1) * READOUT_PAIRS * LANES)
        acc = acc + _mm(jnp.concatenate(parts, axis=1), wro_ref[rows, :])
    return acc


_H_SLOT = GRID_W + 16
_H_LEAD = 16
_V_PAD = CONV_HALF * GRID_W
_CTX_LEAD = 16


def _conv_taps(pad_ref, w, base, length, stride):
    acc = None
    for j in range(CONV_WIDTH):
        term = pad_ref[pl.ds(base + (j - CONV_HALF) * stride, length), :] * w[j:j + 1]
        acc = term if acc is None else acc + term
    return acc


def _glu(zv_ref, zg_ref, rows):
    return zv_ref[rows, :].astype(_F32) * _sigmoid(zg_ref[rows, :].astype(_F32))


def _conv_kernel(horizontal, zv_ref, zg_ref, w_ref, b_ref, o_ref, lat_ref, ctx_ref):
    w = w_ref[...]
    bias = b_ref[...]
    lat_ref[...] = jnp.zeros_like(lat_ref)
    ctx_ref[...] = jnp.zeros_like(ctx_ref)
    ctx_ref[pl.ds(_CTX_LEAD, CTX_LEN), :] = (
        _glu(zv_ref, zg_ref, pl.ds(SEQ, CTX_LEN)))
    o_ref[pl.ds(SEQ, CTX_LEN), :] = _conv_taps(ctx_ref, w, _CTX_LEAD, CTX_LEN, 1) + bias
    if horizontal:
        for row in range(GRID_H):
            src = pl.ds(row * GRID_W, GRID_W)
            lat_ref[pl.ds(_H_LEAD + row * _H_SLOT, GRID_W), :] = (
                _glu(zv_ref, zg_ref, src))
        for row in range(GRID_H):
            o_ref[pl.ds(row * GRID_W, GRID_W), :] = (
                _conv_taps(lat_ref, w, _H_LEAD + row * _H_SLOT, GRID_W, 1) + bias)
    else:
        lat_ref[pl.ds(_V_PAD, SEQ), :] = (
            _glu(zv_ref, zg_ref, pl.ds(0, SEQ)))
        blk = 4 * GRID_W
        for i in range(SEQ // blk):
            o_ref[pl.ds(i * blk, blk), :] = _conv_taps(lat_ref, w, _V_PAD + i * blk, blk, GRID_W) + bias


def _conformer_conv(z, dw, dw_b, horizontal):
    batch = z.shape[0]
    first = Z_CONV // LANES
    n_tiles = CONV_DIM // LANES
    lat_rows = (_H_LEAD + GRID_H * _H_SLOT) if horizontal else (SEQ + 2 * _V_PAD)
    return pl.pallas_call(
        functools.partial(_conv_kernel, horizontal),
        grid=(batch, n_tiles),
        in_specs=[
            pl.BlockSpec((None, T_ALL, LANES), lambda b, j: (b, 0, first + j)),
            pl.BlockSpec((None, T_ALL, LANES), lambda b, j: (b, 0, first + n_tiles + j)),
            pl.BlockSpec((CONV_WIDTH, LANES), lambda b, j: (0, j)),
            pl.BlockSpec((1, LANES), lambda b, j: (0, j)),
        ],
        out_specs=pl.BlockSpec((None, T_ALL, LANES), lambda b, j: (b, 0, j)),
        out_shape=jax.ShapeDtypeStruct((batch, T_ALL, CONV_DIM), _F32),
        scratch_shapes=[pltpu.VMEM((lat_rows, LANES), _F32),
                        pltpu.VMEM((CTX_LEN + 2 * _CTX_LEAD, LANES), _F32)],
        compiler_params=_params("arbitrary", "arbitrary"),
        name="conformer_conv_h" if horizontal else "conformer_conv_v",
    )(z, z, dw, dw_b)


SGU_ROW_TILE = 6 * SGU_CHUNK
assert T_ALL % SGU_ROW_TILE == 0 and SEQ % SGU_CHUNK == 0


def _sgu_kernel(zu_ref, zv_ref, lng_ref, lnb_ref, ws_ref, bs_ref, o_ref):
    v = _standardize(_gelu_tanh(zv_ref[...].astype(_F32)), LN_EPS) * lng_ref[...] + lnb_ref[...]
    v = v.astype(_BF)
    first_half = lax.broadcasted_iota(jnp.int32, (SGU_CHUNK, LANES), 1) < SGU_GROUP_DIM
    for n in range(SGU_ROW_TILE // SGU_CHUNK):
        rows = slice(n * SGU_CHUNK, (n + 1) * SGU_CHUNK)
        tiles = []
        for t in range(SGU_DIM // LANES):
            vt = v[rows, t * LANES:(t + 1) * LANES]
            tiles.append(jnp.where(first_half, _mm(ws_ref[2 * t], vt), _mm(ws_ref[2 * t + 1], vt)))
        mixed = jnp.concatenate(tiles, axis=1) + bs_ref[...]
        o_ref[rows, :] = (_gelu_tanh(zu_ref[rows, :].astype(_F32)) * mixed).astype(o_ref.dtype)


def _sgu(z, ln_g, ln_b, w_s, b_tile):
    batch = z.shape[0]
    first = Z_SGU // SGU_DIM
    vec = pl.BlockSpec((1, SGU_DIM), lambda b, i: (0, 0))
    return pl.pallas_call(
        _sgu_kernel,
        grid=(batch, T_ALL // SGU_ROW_TILE),
        in_specs=[
            pl.BlockSpec((None, SGU_ROW_TILE, SGU_DIM), lambda b, i: (b, i, first)),
            pl.BlockSpec((None, SGU_ROW_TILE, SGU_DIM), lambda b, i: (b, i, first + 1)),
            vec, vec,
            pl.BlockSpec((SGU_GROUPS, SGU_CHUNK, SGU_CHUNK), lambda b, i: (0, 0, 0)),
            pl.BlockSpec((SGU_CHUNK, SGU_DIM), lambda b, i: (0, 0)),
        ],
        out_specs=pl.BlockSpec((None, SGU_ROW_TILE, SGU_DIM), lambda b, i: (b, i, 0)),
        out_shape=jax.ShapeDtypeStruct((batch, T_ALL, SGU_DIM), _BF),
        compiler_params=_params("arbitrary", "arbitrary"),
        name="sgu",
    )(z, z, ln_g, ln_b, w_s, b_tile)


def _merge_kernel(x_ref, yf_ref, yb_ref, r_ref, k_ref, v_ref, gz_ref, cv_ref, c_ref,
                  g0_ref, g1_ref, g2_ref, gate_ref, gup_ref, gng_ref, gnb_ref, rk_ref, lng_ref, lnb_ref,
                  wro_ref, wco_ref, wso_ref, wm_ref, o_ref):
    c = _mm(c_ref[...], wso_ref[...])
    cb = _silu(_standardize(cv_ref[...], LN_EPS) * lng_ref[...] + lnb_ref[...])
    b = _mm(cb, wco_ref[...])
    a = _rwkv_readout(yf_ref, yb_ref, r_ref, k_ref, v_ref, gz_ref, gup_ref, gng_ref, gnb_ref, rk_ref,
                      wro_ref)
    m = (a * _sigmoid(g0_ref[...].astype(_F32)) + b * _sigmoid(g1_ref[...].astype(_F32))
         + c * _sigmoid(g2_ref[...].astype(_F32)))
    o_ref[...] = x_ref[...] + gate_ref[...] * _mm(m, wm_ref[...])


def _merge(x_all, y_fwd, y_bwd, zr, conv, c_pre, z, gate1, g_up, gn_g, gn_b, r_k, ln_g, ln_b,
           w_ro, w_co, w_so, w_m):
    batch = x_all.shape[0]

    def rows(width, col=0):
        return pl.BlockSpec((None, ROW_TILE, width), lambda b, i: (b, i, col))

    def group(idx):
        return pl.BlockSpec((None, HEAD_PAIRS, ROW_TILE, LANES), lambda b, i: (b, idx, i, 0))

    def whole(shape):
        return pl.BlockSpec(shape, lambda b, i: (0, 0))

    vec = whole((1, D_MODEL))
    return pl.pallas_call(
        _merge_kernel,
        grid=(batch, N_ROW_TILES),
        in_specs=[
            rows(D_MODEL), group(0), group(0), group(0), group(1), group(2),
            pl.BlockSpec((None, None, ROW_TILE, LANES), lambda b, i: (b, TILE_G, i, 0)),
            rows(CONV_DIM), rows(SGU_DIM),
            rows(D_MODEL, 0), rows(D_MODEL, 1), rows(D_MODEL, 2),
            _mod_spec(2), whole((GATE_LORA, D_MODEL)), vec, vec, vec,
            whole((1, CONV_DIM)), whole((1, CONV_DIM)),
            whole((D_MODEL, D_MODEL)), whole((CONV_DIM, D_MODEL)), whole((SGU_DIM, D_MODEL)),
            whole((D_MODEL, D_MODEL)),
        ],
        out_specs=rows(D_MODEL),
        out_shape=jax.ShapeDtypeStruct((batch, T_ALL, D_MODEL), _F32),
        compiler_params=_params("arbitrary", "arbitrary"),
        name="merge",
    )(x_all, y_fwd, y_bwd, zr, zr, zr, zr, conv, c_pre, z, z, z, gate1, g_up, gn_g, gn_b, r_k,
      ln_g, ln_b, w_ro, w_co, w_so, w_m)


def _ffn_kernel(last, x_ref, g_ref, sh_ref, sc_ref, gate_ref, win_ref, wout_ref, fg_ref, o_ref):
    x = x_ref[...]
    h = _rms_mod(x, g_ref[...], sh_ref[...], sc_ref[...]).astype(_BF)
    def gate_up(f):
        cols = slice(f * FF_CHUNK, (f + 1) * FF_CHUNK)
        up_cols = slice(D_FF + f * FF_CHUNK, D_FF + (f + 1) * FF_CHUNK)
        return _mm(h, win_ref[:, cols]), _mm(h, win_ref[:, up_cols])

    n_chunks = D_FF // FF_CHUNK
    acc = jnp.zeros((ROW_TILE, D_MODEL), _F32)
    pending = gate_up(0)
    for f in range(n_chunks):
        g, u = pending
        if f + 1 < n_chunks:
            pending = gate_up(f + 1)
        acc = acc + _mm(_silu(g) * u, wout_ref[f * FF_CHUNK:(f + 1) * FF_CHUNK, :])
    out = x + gate_ref[...] * acc
    if last:
        out = out * lax.rsqrt(jnp.mean(out * out, -1, keepdims=True) + NORM_EPS) * fg_ref[...]
    o_ref[...] = out


def _ffn(x_all, g, shift, scale, gate2, w_in, w_out, final_g, last):
    batch = x_all.shape[0]
    rows = pl.BlockSpec((None, ROW_TILE, D_MODEL), lambda b, i: (b, i, 0))
    vec = pl.BlockSpec((1, D_MODEL), lambda b, i: (0, 0))
    return pl.pallas_call(
        functools.partial(_ffn_kernel, last),
        grid=(batch, LAT_ROW_TILES if last else N_ROW_TILES),
        in_specs=[
            rows, vec,
            _mod_spec(2), _mod_spec(2), _mod_spec(2),
            pl.BlockSpec((D_MODEL, 2 * D_FF), lambda b, i: (0, 0)),
            pl.BlockSpec((D_FF, D_MODEL), lambda b, i: (0, 0)),
            vec,
        ],
        out_specs=rows,
        out_shape=jax.ShapeDtypeStruct((batch, SEQ if last else T_ALL, D_MODEL), _F32),
        compiler_params=_params("arbitrary", "arbitrary"),
        name="swiglu_final" if last else "swiglu",
    )(x_all, g, shift, scale, gate2, w_in, w_out, final_g)


def _split_w_in(w):
    off_conv = RWKV_COLS
    off_gate = off_conv + 2 * CONV_DIM + 2 * SGU_DIM
    return jnp.concatenate([w[:, off_gate:], w[:, off_conv:off_gate]], axis=1), w[:, :off_conv]


def _direction_padded(w_up):
    zero = jnp.zeros_like(w_up[0])
    return jnp.stack([jnp.concatenate([w_up[0], zero], 0), jnp.concatenate([zero, w_up[1]], 0)])


def kernel(x, c, ctx, c_ctx, w_mod, b_mod, norm1_g, norm2_g, w_in, rwkv_shift, rwkv_w0, rwkv_w_up, rwkv_a0, rwkv_a_up, rwkv_g_up, rwkv_k_k, rwkv_k_a, rwkv_r_k, rwkv_gn_g, rwkv_gn_b, rwkv_out, conv_dw, conv_dw_b, conv_ln_g, conv_ln_b, conv_out, sgu_ln_g, sgu_ln_b, sgu_w, sgu_b, sgu_out, w_merge, ffn_w_in, ffn_w_out, final_norm_g):
    batch = x.shape[0]
    depth = w_mod.shape[0]
    assert x.shape[1:] == (SEQ, D_MODEL) and ctx.shape[1:] == (CTX_LEN, D_MODEL)

    rows = -(-(batch + 1) // 8) * 8
    cvec = jnp.zeros((rows, D_MODEL), _F32).at[:batch].set(c).at[batch].set(c_ctx)
    mod = _modulation(cvec, w_mod.astype(_BF), b_mod[:, None, :])
    mod_lat = mod[:, :batch].reshape(depth, batch, 6, D_MODEL)
    mod_ctx = jnp.broadcast_to(mod[:, batch].reshape(depth, 1, 6, D_MODEL), mod_lat.shape)
    mod_tab = jnp.stack([mod_lat, mod_ctx], axis=2).transpose(0, 3, 1, 2, 4)
    mod_tab = mod_tab.reshape(depth, 6, 2 * batch, 1, D_MODEL)

    idx = jnp.arange(SCAN_CHUNK)
    incl = jnp.stack([idx[:, None] >= idx[None, :], idx[:, None] <= idx[None, :]]).astype(_F32)
    strict = jnp.stack([idx[:, None] > idx[None, :], idx[:, None] < idx[None, :]]).astype(_F32)
    m_top = jnp.concatenate([jnp.concatenate([strict, -strict], 2),
                             jnp.concatenate([-strict, strict], 2)], 1)
    m_bot = jnp.concatenate([jnp.concatenate([incl, -incl], 2)] * 2, 1).astype(_BF)
    m_top = m_top.astype(_BF)
    m_incl = incl.astype(_BF)

    x_all = jnp.concatenate([x, ctx], axis=1)
    for l in range(depth):
        sh1, sc1, g1, sh2, sc2, g2 = (mod_tab[l, i] for i in range(6))
        w_z, w_r = _split_w_in(w_in[l])
        z = _in_projection(x_all, norm1_g[l][None], sh1, sc1, w_z.astype(_BF))
        zr = _in_projection_rwkv(x_all, norm1_g[l][None], sh1, sc1, w_r.astype(_BF), rwkv_shift[l])
        y_fwd, y_bwd = _wkv_scan(zr, _direction_padded(rwkv_w_up[l]).astype(_BF), rwkv_w0[l][:, None, :],
                      _direction_padded(rwkv_a_up[l]).astype(_BF), rwkv_a0[l][:, None, :],
                      rwkv_k_k[l][None], rwkv_k_a[l][None], m_incl, m_top, m_bot)
        conv = _conformer_conv(z, conv_dw[l], conv_dw_b[l][None], l % 2 == 0)
        b_tile = jnp.repeat(sgu_b[l].T, SGU_GROUP_DIM, axis=1)
        c_pre = _sgu(z, sgu_ln_g[l][None], sgu_ln_b[l][None], sgu_w[l].astype(_BF), b_tile)
        x_all = _merge(x_all, y_fwd, y_bwd, zr, conv, c_pre, z, g1, rwkv_g_up[l].astype(_BF),
                       rwkv_gn_g[l][None], rwkv_gn_b[l][None], rwkv_r_k[l][None],
                       conv_ln_g[l][None], conv_ln_b[l][None],
                       rwkv_out[l].astype(_BF), conv_out[l].astype(_BF), sgu_out[l].astype(_BF),
                       w_merge[l].astype(_BF))
        x_all = _ffn(x_all, norm2_g[l][None], sh2, sc2, g2, ffn_w_in[l].astype(_BF),
                     ffn_w_out[l].astype(_BF), final_norm_g[None], l == depth - 1)
    return x_all
```

```python
import functools
import itertools
import math

import jax
import jax.numpy as jnp
from jax import lax
from jax.experimental import pallas as pl
from jax.experimental.pallas import tpu as pltpu

D_MODEL = 1024
SEQ = 2048
CTX_LEN = 256
T_ALL = SEQ + CTX_LEN
GRID_W = 64
GRID_H = SEQ // GRID_W

HEAD_DIM = 64
HEADS = D_MODEL // HEAD_DIM
HEAD_PAIRS = HEADS // 2
DECAY_LORA = 64
ICLR_LORA = 64
GATE_LORA = 128
CONV_DIM = D_MODEL // 2
CONV_WIDTH = 31
CONV_HALF = CONV_WIDTH // 2
SGU_DIM = D_MODEL // 2
SGU_GROUPS = 8
SGU_GROUP_DIM = SGU_DIM // SGU_GROUPS
SGU_CHUNK = 128
D_FF = ((8 * D_MODEL // 3 + 255) // 256) * 256
NORM_EPS = 1e-6
LN_EPS = 1e-5
GN_EPS = 64e-5

RWKV_COLS = 3 * D_MODEL + 2 * DECAY_LORA + 2 * ICLR_LORA + GATE_LORA

LANES = 128
Z_GATE = 0
Z_CONV = 3 * D_MODEL
Z_SGU = Z_CONV + 2 * CONV_DIM
Z_COLS = Z_SGU + 2 * SGU_DIM
RWKV_TILES = RWKV_COLS // LANES
TILE_W = 3 * HEAD_PAIRS
TILE_A = TILE_W + 1
TILE_G = TILE_A + 1

ROW_TILE = 256
N_ROW_TILES = T_ALL // ROW_TILE
LAT_ROW_TILES = SEQ // ROW_TILE
SCAN_CHUNK = 64
N_CHUNKS = T_ALL // SCAN_CHUNK
LAT_CHUNKS = SEQ // SCAN_CHUNK
FF_CHUNK = 256

VMEM_LIMIT = 56 * 1024 * 1024

_BF = jnp.bfloat16
_F32 = jnp.float32


def _mm(a, b):
    return jnp.dot(a.astype(_BF), b.astype(_BF), preferred_element_type=_F32)


def _sigmoid(x):
    return jax.nn.sigmoid(x)


def _silu(x):
    return x * jax.nn.sigmoid(x)


def _gelu_tanh(x):
    return 0.5 * x * (1.0 + jnp.tanh(math.sqrt(2.0 / math.pi) * (x + 0.044715 * (x * x * x))))


def _standardize(x, eps):
    xc = x - jnp.mean(x, -1, keepdims=True)
    return xc * lax.rsqrt(jnp.mean(xc * xc, -1, keepdims=True) + eps)


def _rms_mod(x, g, shift, scale):
    y = x * lax.rsqrt(jnp.mean(x * x, -1, keepdims=True) + NORM_EPS) * g
    return y * (1.0 + scale) + shift


def _params(*sem):
    return pltpu.CompilerParams(dimension_semantics=sem, vmem_limit_bytes=VMEM_LIMIT)


def _mod_kernel(c_ref, w_ref, b_ref, o_ref):
    o_ref[...] = _mm(_silu(c_ref[...]), w_ref[...]) + b_ref[...]


def _modulation(cvec, w_mod, b_mod):
    depth = w_mod.shape[0]
    rows = cvec.shape[0]
    return pl.pallas_call(
        _mod_kernel,
        grid=(depth, 6),
        in_specs=[
            pl.BlockSpec((rows, D_MODEL), lambda l, j: (0, 0)),
            pl.BlockSpec((None, D_MODEL, D_MODEL), lambda l, j: (l, 0, j)),
            pl.BlockSpec((None, 1, D_MODEL), lambda l, j: (l, 0, j)),
        ],
        out_specs=pl.BlockSpec((None, rows, D_MODEL), lambda l, j: (l, 0, j)),
        out_shape=jax.ShapeDtypeStruct((depth, rows, 6 * D_MODEL), _F32),
        compiler_params=_params("arbitrary", "arbitrary"),
        name="modulation",
    )(cvec, w_mod, b_mod)


def _mod_spec(grid_rank):
    if grid_rank == 2:
        return pl.BlockSpec((None, 1, D_MODEL), lambda b, i: (2 * b + i // LAT_ROW_TILES, 0, 0))
    return pl.BlockSpec((None, 1, D_MODEL), lambda j, b, i: (2 * b + i // LAT_ROW_TILES, 0, 0))


def _inproj_kernel(x_ref, g_ref, sh_ref, sc_ref, w_ref, o_ref):
    h = _rms_mod(x_ref[...], g_ref[...], sh_ref[...], sc_ref[...])
    o_ref[...] = _mm(h, w_ref[...]).astype(o_ref.dtype)


def _in_projection(x_all, g, shift, scale, w):
    batch = x_all.shape[0]
    n_col = 1
    tn = Z_COLS // n_col
    return pl.pallas_call(
        _inproj_kernel,
        grid=(n_col, batch, N_ROW_TILES),
        in_specs=[
            pl.BlockSpec((None, ROW_TILE, D_MODEL), lambda j, b, i: (b, i, 0)),
            pl.BlockSpec((1, D_MODEL), lambda j, b, i: (0, 0)),
            _mod_spec(3),
            _mod_spec(3),
            pl.BlockSpec((D_MODEL, tn), lambda j, b, i: (0, j)),
        ],
        out_specs=pl.BlockSpec((None, ROW_TILE, tn), lambda j, b, i: (b, i, j)),
        out_shape=jax.ShapeDtypeStruct((batch, T_ALL, Z_COLS), _BF),
        compiler_params=_params("arbitrary", "arbitrary", "arbitrary"),
        name="in_projection",
    )(x_all, g, shift, scale, w)


HALO = 8
HALO_BLOCKS = ROW_TILE // HALO
SHIFT_TILES = 4


def _inproj_shift_kernel(x_ref, xp_ref, xn_ref, g_ref, sh_ref, sc_ref, w_ref, ws_ref, o_ref):
    i = pl.program_id(1)
    starts = (i == 0) | (i == LAT_ROW_TILES)
    ends = (i == LAT_ROW_TILES - 1) | (i == N_ROW_TILES - 1)
    g, sh, sc = g_ref[...], sh_ref[...], sc_ref[...]
    h_prev = _rms_mod(xp_ref[...], g, sh, sc) * jnp.where(starts, 0.0, 1.0)
    h_next = _rms_mod(xn_ref[...], g, sh, sc) * jnp.where(ends, 0.0, 1.0)
    h = jnp.concatenate([h_prev, _rms_mod(x_ref[...], g, sh, sc), h_next], axis=0).astype(_BF)
    n_rows = ROW_TILE + 2 * HALO
    inner = slice(HALO, HALO + ROW_TILE)
    for j0 in range(0, RWKV_TILES, SHIFT_TILES):
        n_tiles = min(SHIFT_TILES, RWKV_TILES - j0)
        cols = slice(j0 * LANES, (j0 + n_tiles) * LANES)
        z = jnp.dot(h, w_ref[:, cols], preferred_element_type=_F32)
        w = ws_ref[:, cols]
        zr = (pltpu.roll(z, 1, 0)[inner] * w[0:1] + z[inner] * w[1:2]
              + pltpu.roll(z, n_rows - 1, 0)[inner] * w[2:3])
        for t in range(n_tiles):
            o_ref[j0 + t] = zr[:, t * LANES:(t + 1) * LANES].astype(o_ref.dtype)


def _in_projection_rwkv(x_all, g, shift, scale, w, w_shift):
    batch = x_all.shape[0]
    last_block = T_ALL // HALO - 1
    vec = pl.BlockSpec((1, D_MODEL), lambda b, i: (0, 0))
    return pl.pallas_call(
        _inproj_shift_kernel,
        grid=(batch, N_ROW_TILES),
        in_specs=[
            pl.BlockSpec((None, ROW_TILE, D_MODEL), lambda b, i: (b, i, 0)),
            pl.BlockSpec((None, HALO, D_MODEL),
                         lambda b, i: (b, jnp.maximum(i * HALO_BLOCKS - 1, 0), 0)),
            pl.BlockSpec((None, HALO, D_MODEL),
                         lambda b, i: (b, jnp.minimum((i + 1) * HALO_BLOCKS, last_block), 0)),
            vec, _mod_spec(2), _mod_spec(2),
            pl.BlockSpec((D_MODEL, RWKV_COLS), lambda b, i: (0, 0)),
            pl.BlockSpec((3, RWKV_COLS), lambda b, i: (0, 0)),
        ],
        out_specs=pl.BlockSpec((None, RWKV_TILES, ROW_TILE, LANES), lambda b, i: (b, 0, i, 0)),
        out_shape=jax.ShapeDtypeStruct((batch, RWKV_TILES, T_ALL, LANES), _BF),
        compiler_params=_params("arbitrary", "arbitrary"),
        name="in_projection_rwkv",
    )(x_all, x_all, x_all, g, shift, scale, w, w_shift)


def _head_sum(x):
    lane = lax.broadcasted_iota(jnp.int32, x.shape, x.ndim - 1)
    lo = jnp.sum(jnp.where(lane < HEAD_DIM, x, 0.0), -1, keepdims=True)
    tot = jnp.sum(x, -1, keepdims=True)
    return jnp.where(lane < HEAD_DIM, lo, tot - lo)


def _bmm(a, b):
    return lax.dot_general(a.astype(_BF), b.astype(_BF), (((2,), (1,)), ((0,), (0,))),
                           preferred_element_type=_F32)


def _bmm_tn(a, b):
    return lax.dot_general(a.astype(_BF), b.astype(_BF), (((1,), (1,)), ((0,), (0,))),
                           preferred_element_type=_F32)


N_SLOT_REFS = 5


def _scan_kernel(*refs):
    tokens = (refs[0:2], refs[2:4])
    wup_ref, w0_ref, aup_ref, a0_ref, kk_ref, ka_ref, mi_ref, mt_ref, mb_ref = refs[4:13]
    y_refs = refs[13:15]
    s_ref = refs[15]
    slots = (refs[16:16 + N_SLOT_REFS], refs[16 + N_SLOT_REFS:])
    step = pl.program_id(1)

    def run(prep, solve):
        pieces = iter(())
        if prep is not None:
            pieces = itertools.chain.from_iterable(
                _scan_prepare(tokens[d], (wup_ref.at[d], w0_ref.at[d], aup_ref.at[d], a0_ref.at[d],
                                          kk_ref, ka_ref, mi_ref.at[d]), slots[prep], d)
                for d in range(2))
            next(pieces)
        if solve is not None:
            for _ in _scan_solve(slots[solve], mt_ref, mb_ref, y_refs, s_ref):
                next(pieces, None)
        for _ in pieces:
            pass

    assert N_CHUNKS % 2 == 0

    @pl.when(step == 0)
    def _():
        s_ref[...] = jnp.zeros_like(s_ref)
        run(0, None)

    @pl.when((step % 2 == 0) & (step > 0) & (step < N_CHUNKS))
    def _():
        run(0, 1)

    @pl.when(step % 2 == 1)
    def _():
        run(1, 0)

    @pl.when(step == N_CHUNKS)
    def _():
        run(None, 1)


PREP_PAIRS = 8
PREP_PIECES = HEAD_PAIRS // PREP_PAIRS


def _scan_prepare(token_refs, param_refs, slot, d):
    rkv_ref, lora_ref = token_refs
    r_ref, k_ref, v_ref = (rkv_ref.at[pl.ds(i * HEAD_PAIRS, HEAD_PAIRS)] for i in range(3))
    lw_ref, la_ref = lora_ref.at[0], lora_ref.at[1]
    wup_ref, w0_ref, aup_ref, a0_ref, kk_ref, ka_ref, mi_ref = param_refs
    lhs_ref, rk_ref, vv_ref, ke_ref, dt_ref = slot
    width = PREP_PAIRS * LANES

    def tiles(x):
        return jnp.stack([x[:, i * LANES:(i + 1) * LANES] for i in range(PREP_PAIRS)])

    lw_act = jnp.tanh(lw_ref[...].astype(_F32))
    la_in = la_ref[...]
    m_incl = mi_ref[...]
    for g in range(PREP_PIECES):
        cols = pl.ds(g * width, width)
        pairs = pl.ds(g * PREP_PAIRS, PREP_PAIRS)
        xw = w0_ref[:, cols] + _mm(lw_act, wup_ref[:, cols])
        ld = -(math.exp(-0.5) * math.log2(math.e)) * _sigmoid(xw)
        a = tiles(_sigmoid(a0_ref[:, cols] + _mm(la_in, aup_ref[:, cols])))
        hi = ld.astype(_BF)
        rem = ld - hi.astype(_F32)
        mid = rem.astype(_BF)
        lo = (rem - mid.astype(_F32)).astype(_BF)
        cl = (jnp.dot(m_incl, hi, preferred_element_type=_F32)
              + jnp.dot(m_incl, mid, preferred_element_type=_F32)
              + jnp.dot(m_incl, lo, preferred_element_type=_F32))
        ld_tot = jnp.sum(ld, 0, keepdims=True)
        dec_tot = jnp.exp2(ld_tot)
        e_in = tiles(jnp.exp2(cl))
        e_out = tiles(jnp.exp2(-cl))
        e_ex = tiles(jnp.exp2(cl - ld))
        e_end = tiles(jnp.exp2(ld_tot - cl))

        r = r_ref[pairs].astype(_F32)
        k = k_ref[pairs].astype(_F32)
        kk = k * tiles(kk_ref[:, cols])
        kk = kk * lax.rsqrt(jnp.maximum(_head_sum(kk * kk), 1e-24))
        k_d = k * (1.0 + (a - 1.0) * tiles(ka_ref[:, cols]))
        kka = kk * a

        rows = pl.ds(d * HEAD_PAIRS + g * PREP_PAIRS, PREP_PAIRS)
        lhs_ref[rows] = jnp.concatenate([kk * e_ex, r * e_in], axis=1).astype(_BF)
        rk_ref[rows] = jnp.swapaxes(jnp.concatenate([k_d * e_out, kka * e_out], axis=1),
                                    1, 2).astype(_BF)
        vv_ref[rows] = v_ref[pairs]
        ke_ref[rows] = jnp.concatenate([k_d * e_end, -(kka * e_end)], axis=1).astype(_BF)
        dt_ref[rows] = tiles(dec_tot)
        yield


def _by_direction(x, m_ref):
    return jnp.concatenate([x[:HEAD_PAIRS] * m_ref[0], x[HEAD_PAIRS:] * m_ref[1]], axis=0)


def _scan_solve(slot, mt_ref, mb_ref, y_refs, s_ref):
    c = SCAN_CHUNK
    lhs_ref, rk_ref, vv_ref, ke_ref, dt_ref = slot
    lhs = lhs_ref[...]
    rk = rk_ref[...]
    v = vv_ref[...]
    zero = jnp.zeros((), _BF)
    even = lax.broadcasted_iota(jnp.int32, (1, 1, LANES), 2) < HEAD_DIM
    kkt, rt = lhs[:, :c], lhs[:, c:]
    main = _bmm(jnp.concatenate([jnp.where(even, kkt, zero), jnp.where(even, rt, zero),
                                 jnp.where(even, zero, rt), jnp.where(even, zero, kkt)], axis=1), rk)
    yield
    sbd = s_ref[...]
    ls = _bmm(lhs, jnp.swapaxes(sbd, 1, 2))
    yield
    top_odd = pltpu.roll(main[:, 3 * c:], HEAD_DIM, 2)
    main = main[:, :3 * c].astype(_BF)
    top = _by_direction(jnp.concatenate([main[:, :c], top_odd.astype(_BF)], axis=1), mt_ref)
    bot = _by_direction(main[:, c:], mb_ref)
    row_even = lax.broadcasted_iota(jnp.int32, (1, 2 * c, LANES), 1) < c
    is_x = row_even == even
    vv = jnp.concatenate([v, v], axis=1)
    out = _bmm(jnp.where(is_x, top, zero), vv)
    yield
    def both_heads(o):
        return jnp.where(even, o[:, :c], o[:, c:])

    def operand(x, n_part):
        xb = x.astype(_BF)
        return jnp.where(is_x, jnp.concatenate([xb, xb], axis=1), n_part)

    x = ls[:, :c] + both_heads(out)
    zb = operand(x, top)
    n_steps = int(math.log2(c))
    for i in range(n_steps):
        out = _bmm(jnp.where(is_x, zero, zb), jnp.concatenate([zb[:, c:], zb[:, :c]], axis=1))
        yield
        x = x + both_heads(out)
        if i + 1 < n_steps:
            zb = operand(x, out.astype(_BF))
    vu = jnp.concatenate([v, x.astype(_BF)], axis=1)
    yy = _bmm(bot, vu)
    upd = _bmm_tn(vu, ke_ref[...])
    yield
    y = ls[:, c:] + jnp.where(even, yy[:, :c], yy[:, c:])
    y_refs[0][...] = y[:HEAD_PAIRS]
    y_refs[1][...] = y[HEAD_PAIRS:]
    blk_row = lax.broadcasted_iota(jnp.int32, (1, LANES, LANES), 1) < HEAD_DIM
    blk_col = lax.broadcasted_iota(jnp.int32, (1, LANES, LANES), 2) < HEAD_DIM
    s_ref[...] = jnp.where(blk_row == blk_col, sbd * dt_ref[...] + upd, 0.0)


def _scan_chunk_index(d, s):
    return (s + LAT_CHUNKS) % N_CHUNKS if d == 0 else N_CHUNKS - 1 - s


def _wkv_scan(zr, wup, w0, aup, a0, k_k, k_a, m_incl, m_top, m_bot):
    batch = zr.shape[0]
    c = SCAN_CHUNK

    def prep_chunk(d, s):
        return _scan_chunk_index(d, jnp.minimum(s, N_CHUNKS - 1))

    def solve_chunk(d, s):
        return _scan_chunk_index(d, jnp.maximum(s - 1, 0))

    def tokens(d):
        assert TILE_W % 2 == 0 and TILE_A == TILE_W + 1
        return [pl.BlockSpec((None, TILE_W, c, LANES), lambda b, s: (b, 0, prep_chunk(d, s), 0)),
                pl.BlockSpec((None, 2, c, LANES), lambda b, s: (b, TILE_W // 2, prep_chunk(d, s), 0))]

    def whole(*shape):
        return pl.BlockSpec(shape, lambda b, s: (0,) * len(shape))

    def y_spec(d):
        return pl.BlockSpec((None, HEAD_PAIRS, c, LANES), lambda b, s: (b, 0, solve_chunk(d, s), 0))

    both = 2 * HEAD_PAIRS
    slot = [pltpu.VMEM((both, 2 * c, LANES), _BF), pltpu.VMEM((both, 2 * c, LANES), _BF),
            pltpu.VMEM((both, c, LANES), _BF), pltpu.VMEM((both, 2 * c, LANES), _BF),
            pltpu.VMEM((both, 1, LANES), _F32)]
    assert len(slot) == N_SLOT_REFS
    y_shape = jax.ShapeDtypeStruct((batch, HEAD_PAIRS, T_ALL, LANES), _F32)
    return pl.pallas_call(
        _scan_kernel,
        grid=(batch, N_CHUNKS + 1),
        in_specs=tokens(0) + tokens(1) + [
            whole(2, LANES, D_MODEL), whole(2, 1, D_MODEL),
            whole(2, LANES, D_MODEL), whole(2, 1, D_MODEL),
            whole(1, D_MODEL), whole(1, D_MODEL),
            whole(2, c, c), whole(2, 2 * c, 2 * c), whole(2, 2 * c, 2 * c),
        ],
        out_specs=[y_spec(0), y_spec(1)],
        out_shape=[y_shape, y_shape],
        scratch_shapes=[pltpu.VMEM((both, LANES, LANES), _F32)] + slot * 2,
        compiler_params=_params("arbitrary", "arbitrary"),
        name="wkv_scan",
    )(*([zr] * 4), wup, w0, aup, a0, k_k, k_a, m_incl, m_top, m_bot)


READOUT_PAIRS = 2


def _rwkv_readout(yf_ref, yb_ref, r_ref, k_ref, v_ref, gz_ref, gup_ref, gng_ref, gnb_ref, rk_ref,
                  wro_ref):
    gate = _mm(_sigmoid(gz_ref[...].astype(_F32)), gup_ref[...])
    acc = jnp.zeros((ROW_TILE, D_MODEL), _F32)
    for piece in range(HEAD_PAIRS // READOUT_PAIRS):
        parts = []
        for hp in range(piece * READOUT_PAIRS, (piece + 1) * READOUT_PAIRS):
            cols = slice(hp * LANES, (hp + 1) * LANES)
            y = yf_ref[hp] + yb_ref[hp]
            mean = _head_sum(y) * (1.0 / HEAD_DIM)
            yc = y - mean
            var = _head_sum(yc * yc) * (1.0 / HEAD_DIM)
            yn = yc * lax.rsqrt(var + GN_EPS) * gng_ref[:, cols] + gnb_ref[:, cols]
            bonus = (_head_sum(r_ref[hp].astype(_F32) * k_ref[hp].astype(_F32) * rk_ref[:, cols])
                     * v_ref[hp].astype(_F32))
            parts.append(((yn + bonus) * gate[:, cols]).astype(_BF))
        rows = slice(piece * READOUT_PAIRS * LANES, (piece + 1) * READOUT_PAIRS * LANES)
        acc = acc + _mm(jnp.concatenate(parts, axis=1), wro_ref[rows, :])
    return acc


_H_SLOT = GRID_W + 16
_H_LEAD = 16
_V_PAD = CONV_HALF * GRID_W
_CTX_LEAD = 16


def _conv_taps(pad_ref, w, base, length, stride):
    acc = None
    for j in range(CONV_WIDTH):
        term = pad_ref[pl.ds(base + (j - CONV_HALF) * stride, length), :] * w[j:j + 1]
        acc = term if acc is None else acc + term
    return acc


def _glu(zv_ref, zg_ref, rows):
    return zv_ref[rows, :].astype(_F32) * _sigmoid(zg_ref[rows, :].astype(_F32))


def _conv_kernel(horizontal, zv_ref, zg_ref, w_ref, b_ref, o_ref, lat_ref, ctx_ref):
    w = w_ref[...]
    bias = b_ref[...]
    lat_ref[...] = jnp.zeros_like(lat_ref)
    ctx_ref[...] = jnp.zeros_like(ctx_ref)
    ctx_ref[pl.ds(_CTX_LEAD, CTX_LEN), :] = (
        _glu(zv_ref, zg_ref, pl.ds(SEQ, CTX_LEN)))
    o_ref[pl.ds(SEQ, CTX_LEN), :] = _conv_taps(ctx_ref, w, _CTX_LEAD, CTX_LEN, 1) + bias
    if horizontal:
        for row in range(GRID_H):
            src = pl.ds(row * GRID_W, GRID_W)
            lat_ref[pl.ds(_H_LEAD + row * _H_SLOT, GRID_W), :] = (
                _glu(zv_ref, zg_ref, src))
        for row in range(GRID_H):
            o_ref[pl.ds(row * GRID_W, GRID_W), :] = (
                _conv_taps(lat_ref, w, _H_LEAD + row * _H_SLOT, GRID_W, 1) + bias)
    else:
        lat_ref[pl.ds(_V_PAD, SEQ), :] = (
            _glu(zv_ref, zg_ref, pl.ds(0, SEQ)))
        blk = 4 * GRID_W
        for i in range(SEQ // blk):
            o_ref[pl.ds(i * blk, blk), :] = _conv_taps(lat_ref, w, _V_PAD + i * blk, blk, GRID_W) + bias


def _conformer_conv(z, dw, dw_b, horizontal):
    batch = z.shape[0]
    first = Z_CONV // LANES
    n_tiles = CONV_DIM // LANES
    lat_rows = (_H_LEAD + GRID_H * _H_SLOT) if horizontal else (SEQ + 2 * _V_PAD)
    return pl.pallas_call(
        functools.partial(_conv_kernel, horizontal),
        grid=(batch, n_tiles),
        in_specs=[
            pl.BlockSpec((None, T_ALL, LANES), lambda b, j: (b, 0, first + j)),
            pl.BlockSpec((None, T_ALL, LANES), lambda b, j: (b, 0, first + n_tiles + j)),
            pl.BlockSpec((CONV_WIDTH, LANES), lambda b, j: (0, j)),
            pl.BlockSpec((1, LANES), lambda b, j: (0, j)),
        ],
        out_specs=pl.BlockSpec((None, T_ALL, LANES), lambda b, j: (b, 0, j)),
        out_shape=jax.ShapeDtypeStruct((batch, T_ALL, CONV_DIM), _F32),
        scratch_shapes=[pltpu.VMEM((lat_rows, LANES), _F32),
                        pltpu.VMEM((CTX_LEN + 2 * _CTX_LEAD, LANES), _F32)],
        compiler_params=_params("arbitrary", "arbitrary"),
        name="conformer_conv_h" if horizontal else "conformer_conv_v",
    )(z, z, dw, dw_b)


SGU_ROW_TILE = 6 * SGU_CHUNK
assert T_ALL % SGU_ROW_TILE == 0 and SEQ % SGU_CHUNK == 0


def _sgu_kernel(zu_ref, zv_ref, lng_ref, lnb_ref, ws_ref, bs_ref, o_ref):
    v = _standardize(_gelu_tanh(zv_ref[...].astype(_F32)), LN_EPS) * lng_ref[...] + lnb_ref[...]
    v = v.astype(_BF)
    first_half = lax.broadcasted_iota(jnp.int32, (SGU_CHUNK, LANES), 1) < SGU_GROUP_DIM
    for n in range(SGU_ROW_TILE // SGU_CHUNK):
        rows = slice(n * SGU_CHUNK, (n + 1) * SGU_CHUNK)
        tiles = []
        for t in range(SGU_DIM // LANES):
            vt = v[rows, t * LANES:(t + 1) * LANES]
            tiles.append(jnp.where(first_half, _mm(ws_ref[2 * t], vt), _mm(ws_ref[2 * t + 1], vt)))
        mixed = jnp.concatenate(tiles, axis=1) + bs_ref[...]
        o_ref[rows, :] = (_gelu_tanh(zu_ref[rows, :].astype(_F32)) * mixed).astype(o_ref.dtype)


def _sgu(z, ln_g, ln_b, w_s, b_tile):
    batch = z.shape[0]
    first = Z_SGU // SGU_DIM
    vec = pl.BlockSpec((1, SGU_DIM), lambda b, i: (0, 0))
    return pl.pallas_call(
        _sgu_kernel,
        grid=(batch, T_ALL // SGU_ROW_TILE),
        in_specs=[
            pl.BlockSpec((None, SGU_ROW_TILE, SGU_DIM), lambda b, i: (b, i, first)),
            pl.BlockSpec((None, SGU_ROW_TILE, SGU_DIM), lambda b, i: (b, i, first + 1)),
            vec, vec,
            pl.BlockSpec((SGU_GROUPS, SGU_CHUNK, SGU_CHUNK), lambda b, i: (0, 0, 0)),
            pl.BlockSpec((SGU_CHUNK, SGU_DIM), lambda b, i: (0, 0)),
        ],
        out_specs=pl.BlockSpec((None, SGU_ROW_TILE, SGU_DIM), lambda b, i: (b, i, 0)),
        out_shape=jax.ShapeDtypeStruct((batch, T_ALL, SGU_DIM), _BF),
        compiler_params=_params("arbitrary", "arbitrary"),
        name="sgu",
    )(z, z, ln_g, ln_b, w_s, b_tile)


def _merge_kernel(x_ref, yf_ref, yb_ref, r_ref, k_ref, v_ref, gz_ref, cv_ref, c_ref,
                  g0_ref, g1_ref, g2_ref, gate_ref, gup_ref, gng_ref, gnb_ref, rk_ref, lng_ref, lnb_ref,
                  wro_ref, wco_ref, wso_ref, wm_ref, o_ref):
    c = _mm(c_ref[...], wso_ref[...])
    cb = _silu(_standardize(cv_ref[...], LN_EPS) * lng_ref[...] + lnb_ref[...])
    b = _mm(cb, wco_ref[...])
    a = _rwkv_readout(yf_ref, yb_ref, r_ref, k_ref, v_ref, gz_ref, gup_ref, gng_ref, gnb_ref, rk_ref,
                      wro_ref)
    m = (a * _sigmoid(g0_ref[...].astype(_F32)) + b * _sigmoid(g1_ref[...].astype(_F32))
         + c * _sigmoid(g2_ref[...].astype(_F32)))
    o_ref[...] = x_ref[...] + gate_ref[...] * _mm(m, wm_ref[...])


def _merge(x_all, y_fwd, y_bwd, zr, conv, c_pre, z, gate1, g_up, gn_g, gn_b, r_k, ln_g, ln_b,
           w_ro, w_co, w_so, w_m):
    batch = x_all.shape[0]

    def rows(width, col=0):
        return pl.BlockSpec((None, ROW_TILE, width), lambda b, i: (b, i, col))

    def group(idx):
        return pl.BlockSpec((None, HEAD_PAIRS, ROW_TILE, LANES), lambda b, i: (b, idx, i, 0))

    def whole(shape):
        return pl.BlockSpec(shape, lambda b, i: (0, 0))

    vec = whole((1, D_MODEL))
    return pl.pallas_call(
        _merge_kernel,
        grid=(batch, N_ROW_TILES),
        in_specs=[
            rows(D_MODEL), group(0), group(0), group(0), group(1), group(2),
            pl.BlockSpec((None, None, ROW_TILE, LANES), lambda b, i: (b, TILE_G, i, 0)),
            rows(CONV_DIM), rows(SGU_DIM),
            rows(D_MODEL, 0), rows(D_MODEL, 1), rows(D_MODEL, 2),
            _mod_spec(2), whole((GATE_LORA, D_MODEL)), vec, vec, vec,
            whole((1, CONV_DIM)), whole((1, CONV_DIM)),
            whole((D_MODEL, D_MODEL)), whole((CONV_DIM, D_MODEL)), whole((SGU_DIM, D_MODEL)),
            whole((D_MODEL, D_MODEL)),
        ],
        out_specs=rows(D_MODEL),
        out_shape=jax.ShapeDtypeStruct((batch, T_ALL, D_MODEL), _F32),
        compiler_params=_params("arbitrary", "arbitrary"),
        name="merge",
    )(x_all, y_fwd, y_bwd, zr, zr, zr, zr, conv, c_pre, z, z, z, gate1, g_up, gn_g, gn_b, r_k,
      ln_g, ln_b, w_ro, w_co, w_so, w_m)


FFN_ROW_TILE = 3 * ROW_TILE
FFN_LAST_ROW_TILE = 4 * ROW_TILE


def _ffn_kernel(last, x_ref, g_ref, lat_ref, ctx_ref, win_ref, wout_ref, fg_ref, o_ref):
    x = x_ref[...]
    n_rows = x.shape[0]
    row = pl.program_id(1) * n_rows + lax.broadcasted_iota(jnp.int32, (n_rows, 1), 0)
    is_ctx = row >= SEQ
    sh, sc, gate = (jnp.where(is_ctx, ctx_ref[i], lat_ref[i]) for i in range(3))
    h = _rms_mod(x, g_ref[...], sh, sc).astype(_BF)
    def gate_up(f):
        cols = slice(f * FF_CHUNK, (f + 1) * FF_CHUNK)
        up_cols = slice(D_FF + f * FF_CHUNK, D_FF + (f + 1) * FF_CHUNK)
        return _mm(h, win_ref[:, cols]), _mm(h, win_ref[:, up_cols])

    n_chunks = D_FF // FF_CHUNK
    acc = jnp.zeros((n_rows, D_MODEL), _F32)
    pending = gate_up(0)
    for f in range(n_chunks):
        g, u = pending
        if f + 1 < n_chunks:
            pending = gate_up(f + 1)
        acc = acc + _mm(_silu(g) * u, wout_ref[f * FF_CHUNK:(f + 1) * FF_CHUNK, :])
    out = x + gate * acc
    if last:
        out = out * lax.rsqrt(jnp.mean(out * out, -1, keepdims=True) + NORM_EPS) * fg_ref[...]
    o_ref[...] = out


def _ffn(x_all, g, shift, scale, gate2, w_in, w_out, final_g, last):
    batch = x_all.shape[0]
    n_out = SEQ if last else T_ALL
    tile = FFN_LAST_ROW_TILE if last else FFN_ROW_TILE
    assert n_out % tile == 0
    rows = pl.BlockSpec((None, tile, D_MODEL), lambda b, i: (b, i, 0))
    vec = pl.BlockSpec((1, D_MODEL), lambda b, i: (0, 0))
    mods = jnp.stack([shift, scale, gate2], axis=1)

    def stream(which):
        return pl.BlockSpec((None, 3, 1, D_MODEL), lambda b, i: (2 * b + which, 0, 0, 0))

    once = pl.Buffered(1)
    return pl.pallas_call(
        functools.partial(_ffn_kernel, last),
        grid=(batch, n_out // tile),
        in_specs=[
            rows, vec, stream(0), stream(1),
            pl.BlockSpec((D_MODEL, 2 * D_FF), lambda b, i: (0, 0), pipeline_mode=once),
            pl.BlockSpec((D_FF, D_MODEL), lambda b, i: (0, 0), pipeline_mode=once),
            vec,
        ],
        out_specs=rows,
        out_shape=jax.ShapeDtypeStruct((batch, n_out, D_MODEL), _F32),
        compiler_params=_params("arbitrary", "arbitrary"),
        name="swiglu_final" if last else "swiglu",
    )(x_all, g, mods, mods, w_in, w_out, final_g)


def _split_w_in(w):
    off_conv = RWKV_COLS
    off_gate = off_conv + 2 * CONV_DIM + 2 * SGU_DIM
    return jnp.concatenate([w[:, off_gate:], w[:, off_conv:off_gate]], axis=1), w[:, :off_conv]


def _direction_padded(w_up):
    zero = jnp.zeros_like(w_up[0])
    return jnp.stack([jnp.concatenate([w_up[0], zero], 0), jnp.concatenate([zero, w_up[1]], 0)])


def kernel(x, c, ctx, c_ctx, w_mod, b_mod, norm1_g, norm2_g, w_in, rwkv_shift, rwkv_w0, rwkv_w_up, rwkv_a0, rwkv_a_up, rwkv_g_up, rwkv_k_k, rwkv_k_a, rwkv_r_k, rwkv_gn_g, rwkv_gn_b, rwkv_out, conv_dw, conv_dw_b, conv_ln_g, conv_ln_b, conv_out, sgu_ln_g, sgu_ln_b, sgu_w, sgu_b, sgu_out, w_merge, ffn_w_in, ffn_w_out, final_norm_g):
    batch = x.shape[0]
    depth = w_mod.shape[0]
    assert x.shape[1:] == (SEQ, D_MODEL) and ctx.shape[1:] == (CTX_LEN, D_MODEL)

    rows = -(-(batch + 1) // 8) * 8
    cvec = jnp.zeros((rows, D_MODEL), _F32).at[:batch].set(c).at[batch].set(c_ctx)
    mod = _modulation(cvec, w_mod.astype(_BF), b_mod[:, None, :])
    mod_lat = mod[:, :batch].reshape(depth, batch, 6, D_MODEL)
    mod_ctx = jnp.broadcast_to(mod[:, batch].reshape(depth, 1, 6, D_MODEL), mod_lat.shape)
    mod_tab = jnp.stack([mod_lat, mod_ctx], axis=2).transpose(0, 3, 1, 2, 4)
    mod_tab = mod_tab.reshape(depth, 6, 2 * batch, 1, D_MODEL)

    idx = jnp.arange(SCAN_CHUNK)
    incl = jnp.stack([idx[:, None] >= idx[None, :], idx[:, None] <= idx[None, :]]).astype(_F32)
    strict = jnp.stack([idx[:, None] > idx[None, :], idx[:, None] < idx[None, :]]).astype(_F32)
    m_top = jnp.concatenate([jnp.concatenate([strict, -strict], 2),
                             jnp.concatenate([-strict, strict], 2)], 1)
    m_bot = jnp.concatenate([jnp.concatenate([incl, -incl], 2)] * 2, 1).astype(_BF)
    m_top = m_top.astype(_BF)
    m_incl = incl.astype(_BF)

    x_all = jnp.concatenate([x, ctx], axis=1)
    for l in range(depth):
        sh1, sc1, g1, sh2, sc2, g2 = (mod_tab[l, i] for i in range(6))
        w_z, w_r = _split_w_in(w_in[l])
        z = _in_projection(x_all, norm1_g[l][None], sh1, sc1, w_z.astype(_BF))
        zr = _in_projection_rwkv(x_all, norm1_g[l][None], sh1, sc1, w_r.astype(_BF), rwkv_shift[l])
        y_fwd, y_bwd = _wkv_scan(zr, _direction_padded(rwkv_w_up[l]).astype(_BF), rwkv_w0[l][:, None, :],
                      _direction_padded(rwkv_a_up[l]).astype(_BF), rwkv_a0[l][:, None, :],
                      rwkv_k_k[l][None], rwkv_k_a[l][None], m_incl, m_top, m_bot)
        conv = _conformer_conv(z, conv_dw[l], conv_dw_b[l][None], l % 2 == 0)
        b_tile = jnp.repeat(sgu_b[l].T, SGU_GROUP_DIM, axis=1)
        c_pre = _sgu(z, sgu_ln_g[l][None], sgu_ln_b[l][None], sgu_w[l].astype(_BF), b_tile)
        x_all = _merge(x_all, y_fwd, y_bwd, zr, conv, c_pre, z, g1, rwkv_g_up[l].astype(_BF),
                       rwkv_gn_g[l][None], rwkv_gn_b[l][None], rwkv_r_k[l][None],
                       conv_ln_g[l][None], conv_ln_b[l][None],
                       rwkv_out[l].astype(_BF), conv_out[l].astype(_BF), sgu_out[l].astype(_BF),
                       w_merge[l].astype(_BF))
        x_all = _ffn(x_all, norm2_g[l][None], sh2, sc2, g2, ffn_w_in[l].astype(_BF),
                     ffn_w_out[l].astype(_BF), final_norm_g[None], l == depth - 1)
    return x_all
```
